```python
import jax, jax.numpy as jnp
from jax import lax
import numpy as np

D_MODEL = 1024
BATCH = 2
SEQ = 8192
DEPTH = 2

MEM_LEN = 256
HEAD_DIM = 64
DIL_HEADS = 6
DIL_PATTERNS = ((128, 1), (512, 4), (2048, 16))
DIL_BLOCK = 128
SGU_GROUPS = 4
SGU_GROUP_DIM = 64
SGU_CHUNK = 128
RET_HEADS = 6
RET_CHUNK = 128
ROPE_BASE = 10000.0
A_WIDTH = DIL_HEADS * HEAD_DIM
B_WIDTH = SGU_GROUPS * SGU_GROUP_DIM
C_WIDTH = RET_HEADS * HEAD_DIM
MIX_WIDTH = A_WIDTH + B_WIDTH + C_WIDTH
IN_SPLITS = (A_WIDTH, A_WIDTH, A_WIDTH, B_WIDTH, B_WIDTH, C_WIDTH, C_WIDTH, C_WIDTH, C_WIDTH)
IN_WIDTH = sum(IN_SPLITS)
X_HEADS = 4
X_HEAD_DIM = D_MODEL // X_HEADS
D_FF = 4 * D_MODEL
N_NORMS = 7
EPS = 1e-6
F32 = jnp.float32

kernel_name = "hybrid_dilated_sgu_retention_block"


def rms_norm(x, g):
    xf = x.astype(F32)
    y = xf * lax.rsqrt(jnp.mean(xf * xf, axis=-1, keepdims=True) + EPS)
    return (y * g.astype(F32)).astype(x.dtype)


def dilated_branch(q, k, v, window, dilation):
    B, S, H, E = q.shape
    L = S // dilation
    span = window // dilation
    assert span <= DIL_BLOCK
    nb = -(-L // DIL_BLOCK)
    Lp = nb * DIL_BLOCK

    def to_sub(t):
        t = t.reshape(B, L, dilation, H, E).transpose(0, 2, 1, 3, 4)
        t = jnp.pad(t, ((0, 0), (0, 0), (0, Lp - L), (0, 0), (0, 0)))
        return t.reshape(B, dilation, nb, DIL_BLOCK, H, E).astype(F32)

    def band(t):
        prev = jnp.pad(t, ((0, 0), (0, 0), (1, 0), (0, 0), (0, 0), (0, 0)))[:, :, :nb]
        return jnp.concatenate([prev, t], axis=3)

    qb = to_sub(q)
    kk = band(to_sub(k))
    vv = band(to_sub(v))
    s = jnp.einsum('brnqhe,brnkhe->brnhqk', qb, kk) * (E ** -0.5)
    qi = jnp.arange(DIL_BLOCK)[:, None] + DIL_BLOCK
    ki = jnp.arange(2 * DIL_BLOCK)[None, :]
    dist = qi - ki
    blk = jnp.arange(nb)[:, None, None]
    valid = (dist >= 0) & (dist <= span) & ((blk * DIL_BLOCK - DIL_BLOCK + ki) >= 0)
    s = jnp.where(valid[:, None], s, -jnp.inf)
    m = jnp.max(s, axis=-1, keepdims=True)
    p = jnp.exp(s - m)
    den = jnp.sum(p, axis=-1)
    den_t = den.transpose(0, 1, 2, 4, 3)
    o = jnp.einsum('brnhqk,brnkhe->brnqhe', p, vv) / den_t[..., None]
    lse = (m[..., 0] + jnp.log(den)).transpose(0, 1, 2, 4, 3)

    def from_sub(t):
        rest = t.shape[4:]
        t = t.reshape((B, dilation, Lp) + rest)[:, :, :L]
        return jnp.swapaxes(t, 1, 2).reshape((B, S) + rest)

    return from_sub(o), from_sub(lse)


def dilated_attention(q, k, v):
    outs, lses = [], []
    for window, dilation in DIL_PATTERNS:
        o, l = dilated_branch(q, k, v, window, dilation)
        outs.append(o)
        lses.append(l)
    w = jax.nn.softmax(jnp.stack(lses, 0), axis=0)
    out = jnp.sum(w[..., None] * jnp.stack(outs, 0), axis=0)
    B, S = out.shape[:2]
    return out.reshape(B, S, A_WIDTH)


def spatial_gating(u, v, w_s, b_s, ln_g, ln_b):
    B, S, _ = u.shape
    u = jax.nn.gelu(u.astype(F32))
    v = jax.nn.gelu(v.astype(F32))
    mu = jnp.mean(v, axis=-1, keepdims=True)
    var = jnp.mean(jnp.square(v - mu), axis=-1, keepdims=True)
    v = (v - mu) * lax.rsqrt(var + EPS) * ln_g.astype(F32) + ln_b.astype(F32)
    v = v.reshape(B, S // SGU_CHUNK, SGU_CHUNK, SGU_GROUPS, SGU_GROUP_DIM)
    w = w_s.astype(F32) * jnp.tril(jnp.ones((SGU_CHUNK, SGU_CHUNK), F32))
    s = jnp.einsum('gij,bnjgc->bnigc', w, v) + b_s.astype(F32).T[None, None, :, :, None]
    return u * s.reshape(B, S, B_WIDTH)


def retention(q, k, v, g, positions):
    B, S, H, E = q.shape
    half = E // 2
    inv = 1.0 / (ROPE_BASE ** jnp.linspace(0.0, 1.0, half, dtype=F32))
    ang = positions.astype(F32)[..., None] * inv
    cos = jnp.cos(ang)[:, :, None, :]
    sin = jnp.sin(ang)[:, :, None, :]

    def rot(t):
        t = t.astype(F32)
        t1, t2 = t[..., :half], t[..., half:]
        return jnp.concatenate([t1 * cos - t2 * sin, t2 * cos + t1 * sin], axis=-1)

    q = rot(q)
    k = rot(k) * (E ** -0.5)
    v = v.astype(F32)
    log_g = jnp.log1p(-jnp.power(2.0, -5.0 - jnp.arange(H, dtype=F32)))
    C = RET_CHUNK
    nc = S // C
    qc = q.reshape(B, nc, C, H, E)
    kc = k.reshape(B, nc, C, H, E)
    vc = v.reshape(B, nc, C, H, E)
    idx = jnp.arange(C, dtype=F32)
    rel = idx[:, None] - idx[None, :]
    decay = jnp.where(rel[None] >= 0,
                      jnp.exp(jnp.maximum(rel, 0.0)[None] * log_g[:, None, None]), 0.0)
    inner = jnp.einsum('bnihe,bnjhe->bnhij', qc, kc) * decay
    y_in = jnp.einsum('bnhij,bnjhe->bnihe', inner, vc)
    k_dec = jnp.exp((C - 1 - idx)[:, None] * log_g[None, :])
    kv = jnp.einsum('bnjhk,bnjhv->nbhkv', kc * k_dec[:, :, None], vc)
    chunk_decay = jnp.exp(C * log_g)[None, :, None, None]

    def step(state, kv_n):
        return chunk_decay * state + kv_n, state

    _, prev = lax.scan(step, jnp.zeros((B, H, E, E), F32), kv)
    prev = prev.transpose(1, 0, 2, 3, 4)
    q_dec = jnp.exp((idx + 1.0)[:, None] * log_g[None, :])
    y_x = jnp.einsum('bnihk,bnhkv->bnihv', qc * q_dec[:, :, None], prev)
    y = (y_in + y_x).reshape(B, S, H, E)
    mu = jnp.mean(y, axis=-1, keepdims=True)
    var = jnp.mean(jnp.square(y - mu), axis=-1, keepdims=True)
    y = (y - mu) * lax.rsqrt(var + EPS)
    return jax.nn.silu(g.astype(F32)) * y.reshape(B, S, H * E)


def memory_attention(h, m, wq, wkv, wo):
    B, S, _ = h.shape
    M = m.shape[1]
    q = (h @ wq).reshape(B, S, X_HEADS, X_HEAD_DIM).astype(F32)
    kv = (m @ wkv).reshape(B, M, 2, X_HEADS, X_HEAD_DIM).astype(F32)
    s = jnp.einsum('bshe,bmhe->bhsm', q, kv[:, :, 0]) * (X_HEAD_DIM ** -0.5)
    p = jax.nn.softmax(s, axis=-1)
    o = jnp.einsum('bhsm,bmhe->bshe', p, kv[:, :, 1]).reshape(B, S, D_MODEL)
    return o.astype(h.dtype) @ wo


def setup_inputs(seed: int = 0) -> dict:
    key = jax.random.key(seed)
    ks = jax.random.split(key, 16)
    nrm = jax.random.normal
    x = nrm(ks[0], (BATCH, SEQ, D_MODEL), F32)
    mem = nrm(ks[1], (BATCH, MEM_LEN, D_MODEL), F32)
    offset = jax.random.randint(ks[2], (BATCH, 1), 0, 4096, dtype=jnp.int32)
    positions = offset + jnp.arange(SEQ, dtype=jnp.int32)[None, :]
    norm_g = 1.0 + 0.02 * nrm(ks[3], (DEPTH, N_NORMS, D_MODEL), F32)
    w_in = nrm(ks[4], (DEPTH, D_MODEL, IN_WIDTH), F32) * D_MODEL ** -0.5
    sgu_w = nrm(ks[5], (DEPTH, SGU_GROUPS, SGU_CHUNK, SGU_CHUNK), F32) * SGU_CHUNK ** -0.5
    sgu_b = 1.0 + 0.01 * nrm(ks[6], (DEPTH, SGU_GROUPS, SGU_CHUNK), F32)
    sgu_ln_g = 1.0 + 0.02 * nrm(ks[7], (DEPTH, B_WIDTH), F32)
    sgu_ln_b = 0.01 * nrm(ks[8], (DEPTH, B_WIDTH), F32)
    w_out = nrm(ks[9], (DEPTH, MIX_WIDTH, D_MODEL), F32) * MIX_WIDTH ** -0.5
    x_wq = nrm(ks[10], (DEPTH, D_MODEL, D_MODEL), F32) * D_MODEL ** -0.5
    x_wkv = nrm(ks[11], (DEPTH, D_MODEL, 2 * D_MODEL), F32) * D_MODEL ** -0.5
    x_wo = nrm(ks[12], (DEPTH, D_MODEL, D_MODEL), F32) * D_MODEL ** -0.5
    w_up = nrm(ks[13], (DEPTH, D_MODEL, D_FF), F32) * D_MODEL ** -0.5
    w_down = nrm(ks[14], (DEPTH, D_FF, D_MODEL), F32) * D_FF ** -0.5
    return {"x": x, "mem": mem, "positions": positions, "norm_g": norm_g,
            "w_in": w_in, "sgu_w": sgu_w, "sgu_b": sgu_b, "sgu_ln_g": sgu_ln_g,
            "sgu_ln_b": sgu_ln_b, "w_out": w_out, "x_wq": x_wq, "x_wkv": x_wkv,
            "x_wo": x_wo, "w_up": w_up, "w_down": w_down}


def reference(x, mem, positions, norm_g, w_in, sgu_w, sgu_b, sgu_ln_g, sgu_ln_b,
              w_out, x_wq, x_wkv, x_wo, w_up, w_down):
    B, S, _ = x.shape
    cuts = [int(c) for c in np.cumsum(IN_SPLITS)[:-1]]
    for l in range(DEPTH):
        g = norm_g[l]
        h = rms_norm(x, g[0])
        z = h @ w_in[l]
        qa, ka, va, ub, vb, qc, kc, vc, gc = jnp.split(z, cuts, axis=-1)
        head = lambda t, n: t.reshape(B, S, n, HEAD_DIM)
        ya = dilated_attention(head(qa, DIL_HEADS), head(ka, DIL_HEADS), head(va, DIL_HEADS))
        yb = spatial_gating(ub, vb, sgu_w[l], sgu_b[l], sgu_ln_g[l], sgu_ln_b[l])
        yc = retention(head(qc, RET_HEADS), head(kc, RET_HEADS), head(vc, RET_HEADS), gc, positions)
        mix = jnp.concatenate([ya, yb, yc], axis=-1).astype(x.dtype)
        x = x + rms_norm(mix @ w_out[l], g[1])
        mem_n = rms_norm(mem, g[6])
        x = x + rms_norm(memory_attention(rms_norm(x, g[2]), mem_n, x_wq[l], x_wkv[l], x_wo[l]), g[3])
        h = rms_norm(x, g[4])
        f = jnp.square(jax.nn.relu(h @ w_up[l])) @ w_down[l]
        x = x + rms_norm(f, g[5])
    return x
```

```python
import functools

import jax
import jax.numpy as jnp
from jax import lax
from jax.experimental import pallas as pl
from jax.experimental.pallas import tpu as pltpu

F32 = jnp.float32
BF16 = jnp.bfloat16

D_MODEL = 1024
DEPTH = 2
MEM_LEN = 256
HEAD_DIM = 64
DIL_HEADS = 6
DIL_PATTERNS = ((128, 1), (512, 4), (2048, 16))
BLK = 128
SGU_GROUPS = 4
RET_HEADS = 6
ROPE_BASE = 10000.0
A_WIDTH = DIL_HEADS * HEAD_DIM
B_WIDTH = SGU_GROUPS * HEAD_DIM
C_WIDTH = RET_HEADS * HEAD_DIM
QKV_A = 3 * A_WIDTH
UV_B = 2 * B_WIDTH
QKVG_C = 4 * C_WIDTH
IN_WIDTH = QKV_A + UV_B + QKVG_C
X_HEADS = 4
X_HEAD_DIM = D_MODEL // X_HEADS
D_FF = 4 * D_MODEL
EPS = 1e-6

LANES = 128
N_PAIRS = A_WIDTH // LANES
VMEM_LIMIT = 56 * 1024 * 1024

ROW_TILE = 512
FF_CHUNK = 1024


def _params(*sem):
    return pltpu.CompilerParams(dimension_semantics=sem, vmem_limit_bytes=VMEM_LIMIT)


def _rms(x, g):
    ms = jnp.mean(x * x, axis=-1, keepdims=True)
    return x * lax.rsqrt(ms + EPS) * g


def _dot(a, b):
    return jnp.dot(a, b, preferred_element_type=F32)


def _dot_nt(a, b):
    return lax.dot_general(a, b, (((1,), (1,)), ((), ())), preferred_element_type=F32)


def _dot_tn(a, b):
    return lax.dot_general(a, b, (((0,), (0,)), ((), ())), preferred_element_type=F32)


def _lane_lo(shape):
    lane = lax.broadcasted_iota(jnp.int32, shape, len(shape) - 1)
    return (lane & HEAD_DIM) == 0


def _rope_kernel(pos_ref, inv_ref, cos_ref, sin_ref):
    ang = pos_ref[...].astype(F32) * inv_ref[...]
    lane = lax.broadcasted_iota(jnp.int32, ang.shape, 1)
    sign = jnp.where((lane & (HEAD_DIM // 2)) == 0, -1.0, 1.0)
    cos_ref[...] = jnp.cos(ang)
    sin_ref[...] = jnp.sin(ang) * sign


def _rope_tables(positions):
    n = positions.size
    half = HEAD_DIM // 2
    inv = 1.0 / (ROPE_BASE ** jnp.linspace(0.0, 1.0, half, dtype=F32))
    inv = jnp.tile(inv, LANES // half).reshape(1, LANES)
    tm = 1024
    return pl.pallas_call(
        _rope_kernel,
        grid=(n // tm,),
        in_specs=[pl.BlockSpec((tm, 1), lambda i: (i, 0)),
                  pl.BlockSpec((1, LANES), lambda i: (0, 0))],
        out_specs=[pl.BlockSpec((tm, LANES), lambda i: (i, 0))] * 2,
        out_shape=[jax.ShapeDtypeStruct((n, LANES), F32)] * 2,
        compiler_params=_params("parallel"),
        name="rope_tables",
    )(positions.reshape(n, 1), inv)


def _inproj_kernel(x_ref, g_ref, w_ref, a_ref, b_ref, c_ref):
    h = _rms(x_ref[...], g_ref[0:1, :]).astype(BF16)
    a_ref[...] = _dot(h, w_ref[:, :QKV_A]).astype(BF16)
    b_ref[...] = _dot(h, w_ref[:, QKV_A:QKV_A + UV_B])
    c_ref[...] = _dot(h, w_ref[:, QKV_A + UV_B:])


def _inproj(x, g, w):
    n = x.shape[0]
    tm = ROW_TILE
    return pl.pallas_call(
        _inproj_kernel,
        grid=(n // tm,),
        in_specs=[pl.BlockSpec((tm, D_MODEL), lambda i: (i, 0)),
                  pl.BlockSpec(g.shape, lambda i: (0, 0)),
                  pl.BlockSpec(w.shape, lambda i: (0, 0))],
        out_specs=[pl.BlockSpec((tm, QKV_A), lambda i: (i, 0)),
                   pl.BlockSpec((tm, UV_B), lambda i: (i, 0)),
                   pl.BlockSpec((tm, QKVG_C), lambda i: (i, 0))],
        out_shape=[jax.ShapeDtypeStruct((n, QKV_A), BF16),
                   jax.ShapeDtypeStruct((n, UV_B), F32),
                   jax.ShapeDtypeStruct((n, QKVG_C), F32)],
        compiler_params=_params("parallel"),
        name="inproj",
    )(x, g, w)


def _dil_kernel(q_ref, kp_ref, kc_ref, vp_ref, vc_ref, o_ref, l_ref):
    has_prev = pl.program_id(2) > 0
    qi = lax.broadcasted_iota(jnp.int32, (BLK, BLK), 0)
    ki = lax.broadcasted_iota(jnp.int32, (BLK, BLK), 1)
    valid_prev = (ki >= qi) & has_prev
    valid_cur = ki <= qi
    lo = _lane_lo((BLK, LANES))
    scale = HEAD_DIM ** -0.5
    for p in range(N_PAIRS):
        cols = slice(p * LANES, (p + 1) * LANES)
        q = q_ref[:, cols]
        kp, kc, vp, vc = kp_ref[:, cols], kc_ref[:, cols], vp_ref[:, cols], vc_ref[:, cols]
        outs, lses = [], []
        for first in (True, False):
            qm = jnp.where(lo if first else ~lo, q, jnp.zeros_like(q))
            sp = jnp.where(valid_prev, _dot_nt(qm, kp) * scale, -jnp.inf)
            sc = jnp.where(valid_cur, _dot_nt(qm, kc) * scale, -jnp.inf)
            m = jnp.maximum(jnp.max(sp, axis=-1, keepdims=True),
                            jnp.max(sc, axis=-1, keepdims=True))
            pp = jnp.exp(sp - m)
            pc = jnp.exp(sc - m)
            den = jnp.sum(pp, axis=-1, keepdims=True) + jnp.sum(pc, axis=-1, keepdims=True)
            o = _dot(pp.astype(BF16), vp) + _dot(pc.astype(BF16), vc)
            outs.append(o / den)
            lses.append(m + jnp.log(den))
        o_ref[:, cols] = jnp.where(lo, outs[0], outs[1])
        l_ref[:, cols] = jnp.where(lo, lses[0], lses[1])


def _dilated_branch(qkv, batch, seq, dilation):
    sub = seq // dilation
    nb = sub // BLK
    view = qkv.reshape(batch, sub, dilation * QKV_A)

    def spec(which, prev):
        def index(b, r, n):
            return (b, jnp.maximum(n - 1, 0) if prev else n, 3 * r + which)
        return pl.BlockSpec((None, BLK, A_WIDTH), index)

    out_spec = pl.BlockSpec((None, BLK, A_WIDTH), lambda b, r, n: (b, n, r))
    out_sds = jax.ShapeDtypeStruct((batch, sub, dilation * A_WIDTH), F32)
    o, lse = pl.pallas_call(
        _dil_kernel,
        grid=(batch, dilation, nb),
        in_specs=[spec(0, False), spec(1, True), spec(1, False), spec(2, True), spec(2, False)],
        out_specs=[out_spec, out_spec],
        out_shape=[out_sds, out_sds],
        compiler_params=_params("parallel", "parallel", "arbitrary"),
        name=f"dilated_d{dilation}",
    )(view, view, view, view, view)
    return o.reshape(batch * seq, A_WIDTH), lse.reshape(batch * seq, A_WIDTH)


def _rotate(t, cos, sin_signed, lo_half):
    partner = jnp.where(lo_half, pltpu.roll(t, LANES - HEAD_DIM // 2, 1),
                        pltpu.roll(t, HEAD_DIM // 2, 1))
    return t * cos + partner * sin_signed


def _mix_kernel(b_ref, c_ref, cos_ref, sin_ref, sw_ref, sb_ref, lg_ref, lb_ref,
                dec_ref, qdec_ref, kdec_ref, cdm_ref, yb_ref, yc_ref, state_ref):
    @pl.when(pl.program_id(1) == 0)
    def _():
        state_ref[...] = jnp.zeros_like(state_ref)

    lo = _lane_lo((BLK, LANES))
    row = lax.broadcasted_iota(jnp.int32, (BLK, BLK), 0)
    col = lax.broadcasted_iota(jnp.int32, (BLK, BLK), 1)
    tril = row >= col
    same_head = (row >= HEAD_DIM) == (col >= HEAD_DIM)

    u = jax.nn.gelu(b_ref[:, :B_WIDTH])
    v = jax.nn.gelu(b_ref[:, B_WIDTH:])
    mu = jnp.mean(v, axis=-1, keepdims=True)
    var = jnp.mean(jnp.square(v - mu), axis=-1, keepdims=True)
    v = (v - mu) * lax.rsqrt(var + EPS) * lg_ref[...] + lb_ref[...]
    for gp in range(B_WIDTH // LANES):
        cols = slice(gp * LANES, (gp + 1) * LANES)
        vp = v[:, cols].astype(BF16)
        s = [_dot(jnp.where(tril, sw_ref[2 * gp + e], 0.0).astype(BF16), vp) for e in range(2)]
        gate = jnp.where(lo, s[0], s[1]) + sb_ref[:, cols]
        yb_ref[:, cols] = (u[:, cols] * gate).astype(BF16)

    cos = cos_ref[...]
    sin = sin_ref[...]
    lo_half = (lax.broadcasted_iota(jnp.int32, (BLK, LANES), 1) & (HEAD_DIM // 2)) == 0
    for p in range(N_PAIRS):
        cols = slice(p * LANES, (p + 1) * LANES)
        q = _rotate(c_ref[:, p * LANES:(p + 1) * LANES], cos, sin, lo_half)
        k = _rotate(c_ref[:, C_WIDTH + p * LANES:C_WIDTH + (p + 1) * LANES], cos, sin, lo_half)
        k = k * (HEAD_DIM ** -0.5)
        vb = c_ref[:, 2 * C_WIDTH + p * LANES:2 * C_WIDTH + (p + 1) * LANES].astype(BF16)
        gate = c_ref[:, 3 * C_WIDTH + p * LANES:3 * C_WIDTH + (p + 1) * LANES]
        qb = q.astype(BF16)
        kb = k.astype(BF16)
        ys = []
        for e in range(2):
            qm = jnp.where(lo if e == 0 else ~lo, qb, jnp.zeros_like(qb))
            inner = _dot_nt(qm, kb) * dec_ref[2 * p + e]
            ys.append(_dot(inner.astype(BF16), vb))
        y_in = jnp.where(lo, ys[0], ys[1])
        state = state_ref[p]
        y_x = _dot((q * qdec_ref[:, cols]).astype(BF16), state.astype(BF16))
        kv = _dot_tn((k * kdec_ref[:, cols]).astype(BF16), vb)
        state_ref[p] = state * cdm_ref[p] + jnp.where(same_head, kv, 0.0)
        y = y_in + y_x
        inv_n = 1.0 / HEAD_DIM
        mu_lo = jnp.sum(jnp.where(lo, y, 0.0), axis=-1, keepdims=True) * inv_n
        mu_hi = jnp.sum(jnp.where(lo, 0.0, y), axis=-1, keepdims=True) * inv_n
        yc_ = y - jnp.where(lo, mu_lo, mu_hi)
        sq = yc_ * yc_
        var_lo = jnp.sum(jnp.where(lo, sq, 0.0), axis=-1, keepdims=True) * inv_n
        var_hi = jnp.sum(jnp.where(lo, 0.0, sq), axis=-1, keepdims=True) * inv_n
        yn = yc_ * lax.rsqrt(jnp.where(lo, var_lo, var_hi) + EPS)
        yc_ref[:, cols] = (jax.nn.silu(gate) * yn).astype(BF16)


def _retention_tables():
    heads = jnp.arange(RET_HEADS, dtype=F32)
    log_g = jnp.log1p(-jnp.power(2.0, -5.0 - heads))
    idx = jnp.arange(BLK, dtype=F32)
    rel = idx[:, None] - idx[None, :]
    decay = jnp.where(rel[None] >= 0,
                      jnp.exp(jnp.maximum(rel, 0.0)[None] * log_g[:, None, None]), 0.0)
    k_dec = jnp.exp((BLK - 1 - idx)[:, None] * log_g[None, :])
    q_dec = jnp.exp((idx + 1.0)[:, None] * log_g[None, :])
    k_dec = jnp.repeat(k_dec, HEAD_DIM, axis=1)
    q_dec = jnp.repeat(q_dec, HEAD_DIM, axis=1)
    chunk_decay = jnp.exp(BLK * log_g)
    per_row = jnp.repeat(chunk_decay, HEAD_DIM).reshape(N_PAIRS, LANES)
    half = jnp.arange(LANES) >= HEAD_DIM
    same = half[:, None] == half[None, :]
    cdm = jnp.where(same[None], per_row[:, :, None], 0.0)
    return decay, q_dec, k_dec, cdm


def _mixers_bc(zb, zc, cos, sin, sgu_w, sgu_b, ln_g, ln_b, batch, seq):
    n = batch * seq
    nc = seq // BLK
    decay, q_dec, k_dec, cdm = _retention_tables()
    bias = jnp.repeat(sgu_b.T, HEAD_DIM, axis=1)
    row = lambda b, c: (b * nc + c, 0)
    const2 = lambda b, c: (0, 0)
    const3 = lambda b, c: (0, 0, 0)
    return pl.pallas_call(
        _mix_kernel,
        grid=(batch, nc),
        in_specs=[pl.BlockSpec((BLK, UV_B), row),
                  pl.BlockSpec((BLK, QKVG_C), row),
                  pl.BlockSpec((BLK, LANES), row),
                  pl.BlockSpec((BLK, LANES), row),
                  pl.BlockSpec(sgu_w.shape, const3),
                  pl.BlockSpec(bias.shape, const2),
                  pl.BlockSpec((1, B_WIDTH), const2),
                  pl.BlockSpec((1, B_WIDTH), const2),
                  pl.BlockSpec(decay.shape, const3),
                  pl.BlockSpec(q_dec.shape, const2),
                  pl.BlockSpec(k_dec.shape, const2),
                  pl.BlockSpec(cdm.shape, const3)],
        out_specs=[pl.BlockSpec((BLK, B_WIDTH), row),
                   pl.BlockSpec((BLK, C_WIDTH), row)],
        out_shape=[jax.ShapeDtypeStruct((n, B_WIDTH), BF16),
                   jax.ShapeDtypeStruct((n, C_WIDTH), BF16)],
        scratch_shapes=[pltpu.VMEM((N_PAIRS, LANES, LANES), F32)],
        compiler_params=_params("parallel", "arbitrary"),
        name="sgu_retention",
    )(zb, zc, cos, sin, sgu_w, bias, ln_g.reshape(1, B_WIDTH), ln_b.reshape(1, B_WIDTH),
      decay, q_dec, k_dec, cdm)


def _outproj_kernel(o1_ref, l1_ref, o2_ref, l2_ref, o3_ref, l3_ref, yb_ref, yc_ref,
                    x_ref, g_ref, w_ref, out_ref):
    l1, l2, l3 = l1_ref[...], l2_ref[...], l3_ref[...]
    m = jnp.maximum(jnp.maximum(l1, l2), l3)
    e1, e2, e3 = jnp.exp(l1 - m), jnp.exp(l2 - m), jnp.exp(l3 - m)
    ya = (e1 * o1_ref[...] + e2 * o2_ref[...] + e3 * o3_ref[...]) / (e1 + e2 + e3)
    acc = _dot(ya.astype(BF16), w_ref[:A_WIDTH, :])
    acc += _dot(yb_ref[...], w_ref[A_WIDTH:A_WIDTH + B_WIDTH, :])
    acc += _dot(yc_ref[...], w_ref[A_WIDTH + B_WIDTH:, :])
    out_ref[...] = x_ref[...] + _rms(acc, g_ref[1:2, :])


def _outproj(branches, yb, yc, x, g, w):
    n = x.shape[0]
    tm = ROW_TILE
    row = lambda i: (i, 0)
    const = lambda i: (0, 0)
    flat = [t for pair in branches for t in pair]
    return pl.pallas_call(
        _outproj_kernel,
        grid=(n // tm,),
        in_specs=[pl.BlockSpec((tm, A_WIDTH), row)] * 6
                 + [pl.BlockSpec((tm, B_WIDTH), row),
                    pl.BlockSpec((tm, C_WIDTH), row),
                    pl.BlockSpec((tm, D_MODEL), row),
                    pl.BlockSpec(g.shape, const),
                    pl.BlockSpec(w.shape, const)],
        out_specs=pl.BlockSpec((tm, D_MODEL), row),
        out_shape=jax.ShapeDtypeStruct((n, D_MODEL), F32),
        compiler_params=_params("parallel"),
        name="outproj",
    )(*flat, yb, yc, x, g, w)


def _kv_kernel(m_ref, g_ref, w_ref, kv_ref):
    h = _rms(m_ref[...], g_ref[6:7, :]).astype(BF16)
    kv_ref[...] = _dot(h, w_ref[...]).astype(BF16)


def _memory_kv(mem, g, w):
    n = mem.shape[0]
    return pl.pallas_call(
        _kv_kernel,
        grid=(n // MEM_LEN,),
        in_specs=[pl.BlockSpec((MEM_LEN, D_MODEL), lambda i: (i, 0)),
                  pl.BlockSpec(g.shape, lambda i: (0, 0)),
                  pl.BlockSpec(w.shape, lambda i: (0, 0))],
        out_specs=pl.BlockSpec((MEM_LEN, 2 * D_MODEL), lambda i: (i, 0)),
        out_shape=jax.ShapeDtypeStruct((n, 2 * D_MODEL), BF16),
        compiler_params=_params("parallel"),
        name="memory_kv",
    )(mem, g, w)


def _cross_kernel(x_ref, kv_ref, g_ref, wq_ref, wo_ref, out_ref):
    x = x_ref[...]
    h = _rms(x, g_ref[2:3, :]).astype(BF16)
    q = _dot(h, wq_ref[...]).astype(BF16)
    scale = X_HEAD_DIM ** -0.5
    acc = jnp.zeros(x.shape, F32)
    for hd in range(X_HEADS):
        cols = slice(hd * X_HEAD_DIM, (hd + 1) * X_HEAD_DIM)
        k = kv_ref[:, hd * X_HEAD_DIM:(hd + 1) * X_HEAD_DIM]
        v = kv_ref[:, D_MODEL + hd * X_HEAD_DIM:D_MODEL + (hd + 1) * X_HEAD_DIM]
        s = _dot_nt(q[:, cols], k) * scale
        m = jnp.max(s, axis=-1, keepdims=True)
        p = jnp.exp(s - m)
        p = p / jnp.sum(p, axis=-1, keepdims=True)
        o = _dot(p.astype(BF16), v)
        acc += _dot(o.astype(BF16), wo_ref[cols, :])
    out_ref[...] = x + _rms(acc, g_ref[3:4, :])


def _cross(x, kv, g, wq, wo, batch, seq):
    n = x.shape[0]
    tm = ROW_TILE
    per_batch = seq // tm
    row = lambda i: (i, 0)
    const = lambda i: (0, 0)
    return pl.pallas_call(
        _cross_kernel,
        grid=(n // tm,),
        in_specs=[pl.BlockSpec((tm, D_MODEL), row),
                  pl.BlockSpec((MEM_LEN, 2 * D_MODEL), lambda i: (i // per_batch, 0)),
                  pl.BlockSpec(g.shape, const),
                  pl.BlockSpec(wq.shape, const),
                  pl.BlockSpec(wo.shape, const)],
        out_specs=pl.BlockSpec((tm, D_MODEL), row),
        out_shape=jax.ShapeDtypeStruct((n, D_MODEL), F32),
        compiler_params=_params("parallel"),
        name="cross_attention",
    )(x, kv, g, wq, wo)


def _mlp_kernel(x_ref, g_ref, wu_ref, wd_ref, out_ref):
    x = x_ref[...]
    h = _rms(x, g_ref[4:5, :]).astype(BF16)
    acc = jnp.zeros(x.shape, F32)
    for c in range(D_FF // FF_CHUNK):
        cols = slice(c * FF_CHUNK, (c + 1) * FF_CHUNK)
        f = jnp.square(jnp.maximum(_dot(h, wu_ref[:, cols]), 0.0))
        acc += _dot(f.astype(BF16), wd_ref[cols, :])
    out_ref[...] = x + _rms(acc, g_ref[5:6, :])


def _mlp(x, g, w_up, w_down):
    n = x.shape[0]
    tm = ROW_TILE
    row = lambda i: (i, 0)
    const = lambda i: (0, 0)
    return pl.pallas_call(
        _mlp_kernel,
        grid=(n // tm,),
        in_specs=[pl.BlockSpec((tm, D_MODEL), row),
                  pl.BlockSpec(g.shape, const),
                  pl.BlockSpec(w_up.shape, const, pipeline_mode=pl.Buffered(1)),
                  pl.BlockSpec(w_down.shape, const, pipeline_mode=pl.Buffered(1))],
        out_specs=pl.BlockSpec((tm, D_MODEL), row),
        out_shape=jax.ShapeDtypeStruct((n, D_MODEL), F32),
        compiler_params=_params("parallel"),
        name="mlp",
    )(x, g, w_up, w_down)


def kernel(x, mem, positions, norm_g, w_in, sgu_w, sgu_b, sgu_ln_g, sgu_ln_b,
           w_out, x_wq, x_wkv, x_wo, w_up, w_down):
    batch, seq, _ = x.shape
    assert x.shape == (batch, seq, D_MODEL) and seq % (BLK * DIL_PATTERNS[-1][1]) == 0
    assert (batch * seq) % ROW_TILE == 0 and seq % ROW_TILE == 0
    assert mem.shape == (batch, MEM_LEN, D_MODEL) and w_in.shape == (DEPTH, D_MODEL, IN_WIDTH)
    xf = x.reshape(batch * seq, D_MODEL)
    memf = mem.reshape(batch * MEM_LEN, D_MODEL)
    cos, sin = _rope_tables(positions)
    for l in range(DEPTH):
        g = norm_g[l]
        za, zb, zc = _inproj(xf, g, w_in[l].astype(BF16))
        branches = [_dilated_branch(za, batch, seq, d) for _, d in DIL_PATTERNS]
        yb, yc = _mixers_bc(zb, zc, cos, sin, sgu_w[l], sgu_b[l], sgu_ln_g[l], sgu_ln_b[l],
                            batch, seq)
        xf = _outproj(branches, yb, yc, xf, g, w_out[l].astype(BF16))
        kv = _memory_kv(memf, g, x_wkv[l].astype(BF16))
        xf = _cross(xf, kv, g, x_wq[l].astype(BF16), x_wo[l].astype(BF16), batch, seq)
        xf = _mlp(xf, g, w_up[l].astype(BF16), w_down[l].astype(BF16))
    return xf.reshape(batch, seq, D_MODEL)
```

```python
import functools

import jax
import jax.numpy as jnp
from jax import lax
from jax.experimental import pallas as pl
from jax.experimental.pallas import tpu as pltpu

F32 = jnp.float32
BF16 = jnp.bfloat16

D_MODEL = 1024
DEPTH = 2
MEM_LEN = 256
HEAD_DIM = 64
DIL_HEADS = 6
DIL_PATTERNS = ((128, 1), (512, 4), (2048, 16))
BLK = 128
SGU_GROUPS = 4
RET_HEADS = 6
ROPE_BASE = 10000.0
A_WIDTH = DIL_HEADS * HEAD_DIM
B_WIDTH = SGU_GROUPS * HEAD_DIM
C_WIDTH = RET_HEADS * HEAD_DIM
QKV_A = 3 * A_WIDTH
UV_B = 2 * B_WIDTH
QKVG_C = 4 * C_WIDTH
IN_WIDTH = QKV_A + UV_B + QKVG_C
X_HEADS = 4
X_HEAD_DIM = D_MODEL // X_HEADS
D_FF = 4 * D_MODEL
EPS = 1e-6

LANES = 128
N_PAIRS = A_WIDTH // LANES
VMEM_LIMIT = 56 * 1024 * 1024

SUPER = BLK * DIL_PATTERNS[-1][1]

ROW_TILE = 512
FF_CHUNK = 1024


def _params(*sem):
    return pltpu.CompilerParams(dimension_semantics=sem, vmem_limit_bytes=VMEM_LIMIT)


def _rms(x, g):
    ms = jnp.mean(x * x, axis=-1, keepdims=True)
    return x * lax.rsqrt(ms + EPS) * g


def _dot(a, b):
    return jnp.dot(a, b, preferred_element_type=F32)


def _dot_nt(a, b):
    return lax.dot_general(a, b, (((1,), (1,)), ((), ())), preferred_element_type=F32)


def _dot_tn(a, b):
    return lax.dot_general(a, b, (((0,), (0,)), ((), ())), preferred_element_type=F32)


def _lane_lo(shape):
    lane = lax.broadcasted_iota(jnp.int32, shape, len(shape) - 1)
    return (lane & HEAD_DIM) == 0


def _rope_kernel(pos_ref, inv_ref, cos_ref, sin_ref):
    ang = pos_ref[...].astype(F32) * inv_ref[...]
    lane = lax.broadcasted_iota(jnp.int32, ang.shape, 1)
    sign = jnp.where((lane & (HEAD_DIM // 2)) == 0, -1.0, 1.0)
    cos_ref[...] = jnp.cos(ang)
    sin_ref[...] = jnp.sin(ang) * sign


def _rope_tables(positions):
    n = positions.size
    half = HEAD_DIM // 2
    inv = 1.0 / (ROPE_BASE ** jnp.linspace(0.0, 1.0, half, dtype=F32))
    inv = jnp.tile(inv, LANES // half).reshape(1, LANES)
    tm = 1024
    return pl.pallas_call(
        _rope_kernel,
        grid=(n // tm,),
        in_specs=[pl.BlockSpec((tm, 1), lambda i: (i, 0)),
                  pl.BlockSpec((1, LANES), lambda i: (0, 0))],
        out_specs=[pl.BlockSpec((tm, LANES), lambda i: (i, 0))] * 2,
        out_shape=[jax.ShapeDtypeStruct((n, LANES), F32)] * 2,
        compiler_params=_params("parallel"),
        name="rope_tables",
    )(positions.reshape(n, 1), inv)


def _inproj_kernel(x_ref, g_ref, w_ref, a_ref, b_ref, c_ref):
    h = _rms(x_ref[...], g_ref[0:1, :]).astype(BF16)
    a_ref[...] = _dot(h, w_ref[:, :QKV_A]).astype(BF16)
    b_ref[...] = _dot(h, w_ref[:, QKV_A:QKV_A + UV_B])
    c_ref[...] = _dot(h, w_ref[:, QKV_A + UV_B:])


def _inproj(x, g, w):
    n = x.shape[0]
    tm = ROW_TILE
    return pl.pallas_call(
        _inproj_kernel,
        grid=(n // tm,),
        in_specs=[pl.BlockSpec((tm, D_MODEL), lambda i: (i, 0)),
                  pl.BlockSpec(g.shape, lambda i: (0, 0)),
                  pl.BlockSpec(w.shape, lambda i: (0, 0))],
        out_specs=[pl.BlockSpec((tm, QKV_A), lambda i: (i, 0)),
                   pl.BlockSpec((tm, UV_B), lambda i: (i, 0)),
                   pl.BlockSpec((tm, QKVG_C), lambda i: (i, 0))],
        out_shape=[jax.ShapeDtypeStruct((n, QKV_A), BF16),
                   jax.ShapeDtypeStruct((n, UV_B), F32),
                   jax.ShapeDtypeStruct((n, QKVG_C), F32)],
        compiler_params=_params("parallel"),
        name="inproj",
    )(x, g, w)


def _rows(start, stride):
    return pl.ds(start, BLK) if stride == 1 else pl.ds(start, BLK, stride=stride)


def _dil_block(qf, kring, vring, acc_o, acc_m, acc_l, q0, kc0, kp0, stride, has_prev, first):
    qi = lax.broadcasted_iota(jnp.int32, (BLK, BLK), 0)
    ki = lax.broadcasted_iota(jnp.int32, (BLK, BLK), 1)
    valid_prev = (ki >= qi) & has_prev
    valid_cur = ki <= qi
    lo = _lane_lo((BLK, LANES))
    scale = HEAD_DIM ** -0.5
    qrows = _rows(q0, stride)
    for p in range(N_PAIRS):
        cols = slice(p * LANES, (p + 1) * LANES)
        q = qf[p, qrows, :].astype(BF16)
        k = jnp.concatenate([kring[p, _rows(kp0, stride), :], kring[p, _rows(kc0, stride), :]],
                            axis=0).astype(BF16)
        v = jnp.concatenate([vring[p, _rows(kp0, stride), :], vring[p, _rows(kc0, stride), :]],
                            axis=0).astype(BF16)
        outs, alphas = [], []
        for e in range(2):
            h = 2 * p + e
            qm = jnp.where(lo if e == 0 else ~lo, q, jnp.zeros_like(q))
            s = _dot_nt(qm, k) * scale
            sp = jnp.where(valid_prev, s[:, :BLK], -jnp.inf)
            sc = jnp.where(valid_cur, s[:, BLK:], -jnp.inf)
            m_new = jnp.broadcast_to(
                jnp.maximum(jnp.max(sp, axis=-1, keepdims=True),
                            jnp.max(sc, axis=-1, keepdims=True)), (BLK, LANES))
            if not first:
                m_old = acc_m[h, qrows, :]
                m_new = jnp.maximum(m_old, m_new)
            pp = jnp.exp(sp - m_new)
            pc = jnp.exp(sc - m_new)
            l_new = jnp.broadcast_to(jnp.sum(pp, axis=-1, keepdims=True)
                                     + jnp.sum(pc, axis=-1, keepdims=True), (BLK, LANES))
            outs.append(_dot(jnp.concatenate([pp, pc], axis=1).astype(BF16), v))
            if not first:
                alpha = jnp.exp(m_old - m_new)
                alphas.append(alpha)
                l_new = acc_l[h, qrows, :] * alpha + l_new
            acc_m[h, qrows, :] = m_new
            acc_l[h, qrows, :] = l_new
        o = jnp.where(lo, outs[0], outs[1])
        if not first:
            o = acc_o[p, qrows, :] * jnp.where(lo, alphas[0], alphas[1]) + o
        acc_o[p, qrows, :] = o


def _dil_kernel(q_ref, k_ref, v_ref, o_ref, qf, kring, vring, acc_o, acc_m, acc_l):
    t = pl.program_id(1)
    cur = pl.multiple_of((t % 2) * SUPER, SUPER)

    @pl.when(t == 0)
    def _():
        kring[:, SUPER:, :] = jnp.zeros((N_PAIRS, SUPER, LANES), F32)
        vring[:, SUPER:, :] = jnp.zeros((N_PAIRS, SUPER, LANES), F32)

    for p in range(N_PAIRS):
        cols = slice(p * LANES, (p + 1) * LANES)
        qf[p] = q_ref[:, cols].astype(F32)
        kring[p, pl.ds(cur, SUPER), :] = k_ref[:, cols].astype(F32)
        vring[p, pl.ds(cur, SUPER), :] = v_ref[:, cols].astype(F32)
    ring_mask = 2 * SUPER - 1
    args = (qf, kring, vring, acc_o, acc_m, acc_l)

    for (window, d) in DIL_PATTERNS:
        nblk = SUPER // (BLK * d)
        span = BLK * d

        def body(j, carry, d=d, nblk=nblk, span=span):
            r = j // nblk
            nb = j % nblk
            base = pl.multiple_of(nb * span, BLK)
            prev = pl.multiple_of((cur + base - span) & ring_mask, BLK)
            if d == 1:
                q0, kc0, kp0 = base, cur + base, prev
            else:
                q0, kc0, kp0 = base + r, cur + base + r, prev + r
            has_prev = (t > 0) | (nb > 0)
            _dil_block(*args, q0, kc0, kp0, d, has_prev, first=(d == 1))
            return carry

        lax.fori_loop(0, SUPER // BLK, body, 0)

    lo = _lane_lo((BLK, LANES))

    def finish(i, carry):
        rows = pl.ds(pl.multiple_of(i * BLK, BLK), BLK)
        for p in range(N_PAIRS):
            cols = slice(p * LANES, (p + 1) * LANES)
            den = jnp.where(lo, acc_l[2 * p, rows, :], acc_l[2 * p + 1, rows, :])
            o_ref[rows, cols] = (acc_o[p, rows, :] / den).astype(BF16)
        return carry

    lax.fori_loop(0, SUPER // BLK, finish, 0)


def _dilated_attention(qkv, batch, seq):
    n = batch * seq
    tiles = seq // SUPER
    spec = lambda which: pl.BlockSpec((SUPER, A_WIDTH), lambda b, t: (b * tiles + t, which))
    return pl.pallas_call(
        _dil_kernel,
        grid=(batch, tiles),
        in_specs=[spec(0), spec(1), spec(2)],
        out_specs=pl.BlockSpec((SUPER, A_WIDTH), lambda b, t: (b * tiles + t, 0)),
        out_shape=jax.ShapeDtypeStruct((n, A_WIDTH), BF16),
        scratch_shapes=[pltpu.VMEM((N_PAIRS, SUPER, LANES), F32),
                        pltpu.VMEM((N_PAIRS, 2 * SUPER, LANES), F32),
                        pltpu.VMEM((N_PAIRS, 2 * SUPER, LANES), F32),
                        pltpu.VMEM((N_PAIRS, SUPER, LANES), F32),
                        pltpu.VMEM((DIL_HEADS, SUPER, LANES), F32),
                        pltpu.VMEM((DIL_HEADS, SUPER, LANES), F32)],
        compiler_params=_params("parallel", "arbitrary"),
        name="dilated_attention",
    )(qkv, qkv, qkv)


def _rotate(t, cos, sin_signed, lo_half):
    partner = jnp.where(lo_half, pltpu.roll(t, LANES - HEAD_DIM // 2, 1),
                        pltpu.roll(t, HEAD_DIM // 2, 1))
    return t * cos + partner * sin_signed


def _mix_kernel(b_ref, c_ref, cos_ref, sin_ref, sw_ref, sb_ref, lg_ref, lb_ref,
                dec_ref, qdec_ref, kdec_ref, cdm_ref, yb_ref, yc_ref, state_ref):
    @pl.when(pl.program_id(1) == 0)
    def _():
        state_ref[...] = jnp.zeros_like(state_ref)

    lo = _lane_lo((BLK, LANES))
    row = lax.broadcasted_iota(jnp.int32, (BLK, BLK), 0)
    col = lax.broadcasted_iota(jnp.int32, (BLK, BLK), 1)
    tril = row >= col
    same_head = (row >= HEAD_DIM) == (col >= HEAD_DIM)

    u = jax.nn.gelu(b_ref[:, :B_WIDTH])
    v = jax.nn.gelu(b_ref[:, B_WIDTH:])
    mu = jnp.mean(v, axis=-1, keepdims=True)
    var = jnp.mean(jnp.square(v - mu), axis=-1, keepdims=True)
    v = (v - mu) * lax.rsqrt(var + EPS) * lg_ref[...] + lb_ref[...]
    for gp in range(B_WIDTH // LANES):
        cols = slice(gp * LANES, (gp + 1) * LANES)
        vp = v[:, cols].astype(BF16)
        s = [_dot(jnp.where(tril, sw_ref[2 * gp + e], 0.0).astype(BF16), vp) for e in range(2)]
        gate = jnp.where(lo, s[0], s[1]) + sb_ref[:, cols]
        yb_ref[:, cols] = (u[:, cols] * gate).astype(BF16)

    cos = cos_ref[...]
    sin = sin_ref[...]
    lo_half = (lax.broadcasted_iota(jnp.int32, (BLK, LANES), 1) & (HEAD_DIM // 2)) == 0
    for p in range(N_PAIRS):
        cols = slice(p * LANES, (p + 1) * LANES)
        q = _rotate(c_ref[:, p * LANES:(p + 1) * LANES], cos, sin, lo_half)
        k = _rotate(c_ref[:, C_WIDTH + p * LANES:C_WIDTH + (p + 1) * LANES], cos, sin, lo_half)
        k = k * (HEAD_DIM ** -0.5)
        vb = c_ref[:, 2 * C_WIDTH + p * LANES:2 * C_WIDTH + (p + 1) * LANES].astype(BF16)
        gate = c_ref[:, 3 * C_WIDTH + p * LANES:3 * C_WIDTH + (p + 1) * LANES]
        qb = q.astype(BF16)
        kb = k.astype(BF16)
        ys = []
        for e in range(2):
            qm = jnp.where(lo if e == 0 else ~lo, qb, jnp.zeros_like(qb))
            inner = _dot_nt(qm, kb) * dec_ref[2 * p + e]
            ys.append(_dot(inner.astype(BF16), vb))
        y_in = jnp.where(lo, ys[0], ys[1])
        state = state_ref[p]
        y_x = _dot((q * qdec_ref[:, cols]).astype(BF16), state.astype(BF16))
        kv = _dot_tn((k * kdec_ref[:, cols]).astype(BF16), vb)
        state_ref[p] = state * cdm_ref[p] + jnp.where(same_head, kv, 0.0)
        y = y_in + y_x
        inv_n = 1.0 / HEAD_DIM
        mu_lo = jnp.sum(jnp.where(lo, y, 0.0), axis=-1, keepdims=True) * inv_n
        mu_hi = jnp.sum(jnp.where(lo, 0.0, y), axis=-1, keepdims=True) * inv_n
        yc_ = y - jnp.where(lo, mu_lo, mu_hi)
        sq = yc_ * yc_
        var_lo = jnp.sum(jnp.where(lo, sq, 0.0), axis=-1, keepdims=True) * inv_n
        var_hi = jnp.sum(jnp.where(lo, 0.0, sq), axis=-1, keepdims=True) * inv_n
        yn = yc_ * lax.rsqrt(jnp.where(lo, var_lo, var_hi) + EPS)
        yc_ref[:, cols] = (jax.nn.silu(gate) * yn).astype(BF16)


def _retention_tables():
    heads = jnp.arange(RET_HEADS, dtype=F32)
    log_g = jnp.log1p(-jnp.power(2.0, -5.0 - heads))
    idx = jnp.arange(BLK, dtype=F32)
    rel = idx[:, None] - idx[None, :]
    decay = jnp.where(rel[None] >= 0,
                      jnp.exp(jnp.maximum(rel, 0.0)[None] * log_g[:, None, None]), 0.0)
    k_dec = jnp.exp((BLK - 1 - idx)[:, None] * log_g[None, :])
    q_dec = jnp.exp((idx + 1.0)[:, None] * log_g[None, :])
    k_dec = jnp.repeat(k_dec, HEAD_DIM, axis=1)
    q_dec = jnp.repeat(q_dec, HEAD_DIM, axis=1)
    chunk_decay = jnp.exp(BLK * log_g)
    per_row = jnp.repeat(chunk_decay, HEAD_DIM).reshape(N_PAIRS, LANES)
    half = jnp.arange(LANES) >= HEAD_DIM
    same = half[:, None] == half[None, :]
    cdm = jnp.where(same[None], per_row[:, :, None], 0.0)
    return decay, q_dec, k_dec, cdm


def _mixers_bc(zb, zc, cos, sin, sgu_w, sgu_b, ln_g, ln_b, batch, seq):
    n = batch * seq
    nc = seq // BLK
    decay, q_dec, k_dec, cdm = _retention_tables()
    bias = jnp.repeat(sgu_b.T, HEAD_DIM, axis=1)
    row = lambda b, c: (b * nc + c, 0)
    const2 = lambda b, c: (0, 0)
    const3 = lambda b, c: (0, 0, 0)
    return pl.pallas_call(
        _mix_kernel,
        grid=(batch, nc),
        in_specs=[pl.BlockSpec((BLK, UV_B), row),
                  pl.BlockSpec((BLK, QKVG_C), row),
                  pl.BlockSpec((BLK, LANES), row),
                  pl.BlockSpec((BLK, LANES), row),
                  pl.BlockSpec(sgu_w.shape, const3),
                  pl.BlockSpec(bias.shape, const2),
                  pl.BlockSpec((1, B_WIDTH), const2),
                  pl.BlockSpec((1, B_WIDTH), const2),
                  pl.BlockSpec(decay.shape, const3),
                  pl.BlockSpec(q_dec.shape, const2),
                  pl.BlockSpec(k_dec.shape, const2),
                  pl.BlockSpec(cdm.shape, const3)],
        out_specs=[pl.BlockSpec((BLK, B_WIDTH), row),
                   pl.BlockSpec((BLK, C_WIDTH), row)],
        out_shape=[jax.ShapeDtypeStruct((n, B_WIDTH), BF16),
                   jax.ShapeDtypeStruct((n, C_WIDTH), BF16)],
        scratch_shapes=[pltpu.VMEM((N_PAIRS, LANES, LANES), F32)],
        compiler_params=_params("parallel", "arbitrary"),
        name="sgu_retention",
    )(zb, zc, cos, sin, sgu_w, bias, ln_g.reshape(1, B_WIDTH), ln_b.reshape(1, B_WIDTH),
      decay, q_dec, k_dec, cdm)


def _outproj_kernel(ya_ref, yb_ref, yc_ref, x_ref, g_ref, w_ref, out_ref):
    acc = _dot(ya_ref[...], w_ref[:A_WIDTH, :])
    acc += _dot(yb_ref[...], w_ref[A_WIDTH:A_WIDTH + B_WIDTH, :])
    acc += _dot(yc_ref[...], w_ref[A_WIDTH + B_WIDTH:, :])
    out_ref[...] = x_ref[...] + _rms(acc, g_ref[1:2, :])


def _outproj(ya, yb, yc, x, g, w):
    n = x.shape[0]
    tm = ROW_TILE
    row = lambda i: (i, 0)
    const = lambda i: (0, 0)
    return pl.pallas_call(
        _outproj_kernel,
        grid=(n // tm,),
        in_specs=[pl.BlockSpec((tm, A_WIDTH), row),
                  pl.BlockSpec((tm, B_WIDTH), row),
                  pl.BlockSpec((tm, C_WIDTH), row),
                  pl.BlockSpec((tm, D_MODEL), row),
                  pl.BlockSpec(g.shape, const),
                  pl.BlockSpec(w.shape, const)],
        out_specs=pl.BlockSpec((tm, D_MODEL), row),
        out_shape=jax.ShapeDtypeStruct((n, D_MODEL), F32),
        compiler_params=_params("parallel"),
        name="outproj",
    )(ya, yb, yc, x, g, w)


def _kv_kernel(m_ref, g_ref, w_ref, kv_ref):
    h = _rms(m_ref[...], g_ref[6:7, :]).astype(BF16)
    kv_ref[...] = _dot(h, w_ref[...]).astype(BF16)


def _memory_kv(mem, g, w):
    n = mem.shape[0]
    return pl.pallas_call(
        _kv_kernel,
        grid=(n // MEM_LEN,),
        in_specs=[pl.BlockSpec((MEM_LEN, D_MODEL), lambda i: (i, 0)),
                  pl.BlockSpec(g.shape, lambda i: (0, 0)),
                  pl.BlockSpec(w.shape, lambda i: (0, 0))],
        out_specs=pl.BlockSpec((MEM_LEN, 2 * D_MODEL), lambda i: (i, 0)),
        out_shape=jax.ShapeDtypeStruct((n, 2 * D_MODEL), BF16),
        compiler_params=_params("parallel"),
        name="memory_kv",
    )(mem, g, w)


def _cross_kernel(x_ref, kv_ref, g_ref, wq_ref, wo_ref, out_ref):
    x = x_ref[...]
    h = _rms(x, g_ref[2:3, :]).astype(BF16)
    q = _dot(h, wq_ref[...]).astype(BF16)
    scale = X_HEAD_DIM ** -0.5
    acc = jnp.zeros(x.shape, F32)
    for hd in range(X_HEADS):
        cols = slice(hd * X_HEAD_DIM, (hd + 1) * X_HEAD_DIM)
        k = kv_ref[:, hd * X_HEAD_DIM:(hd + 1) * X_HEAD_DIM]
        v = kv_ref[:, D_MODEL + hd * X_HEAD_DIM:D_MODEL + (hd + 1) * X_HEAD_DIM]
        s = _dot_nt(q[:, cols], k) * scale
        m = jnp.max(s, axis=-1, keepdims=True)
        p = jnp.exp(s - m)
        p = p / jnp.sum(p, axis=-1, keepdims=True)
        o = _dot(p.astype(BF16), v)
        acc += _dot(o.astype(BF16), wo_ref[cols, :])
    out_ref[...] = x + _rms(acc, g_ref[3:4, :])


def _cross(x, kv, g, wq, wo, batch, seq):
    n = x.shape[0]
    tm = ROW_TILE
    per_batch = seq // tm
    row = lambda i: (i, 0)
    const = lambda i: (0, 0)
    return pl.pallas_call(
        _cross_kernel,
        grid=(n // tm,),
        in_specs=[pl.BlockSpec((tm, D_MODEL), row),
                  pl.BlockSpec((MEM_LEN, 2 * D_MODEL), lambda i: (i // per_batch, 0)),
                  pl.BlockSpec(g.shape, const),
                  pl.BlockSpec(wq.shape, const),
                  pl.BlockSpec(wo.shape, const)],
        out_specs=pl.BlockSpec((tm, D_MODEL), row),
        out_shape=jax.ShapeDtypeStruct((n, D_MODEL), F32),
        compiler_params=_params("parallel"),
        name="cross_attention",
    )(x, kv, g, wq, wo)


def _mlp_kernel(x_ref, g_ref, wu_ref, wd_ref, out_ref):
    x = x_ref[...]
    h = _rms(x, g_ref[4:5, :]).astype(BF16)
    acc = jnp.zeros(x.shape, F32)
    for c in range(D_FF // FF_CHUNK):
        cols = slice(c * FF_CHUNK, (c + 1) * FF_CHUNK)
        f = jnp.square(jnp.maximum(_dot(h, wu_ref[:, cols]), 0.0))
        acc += _dot(f.astype(BF16), wd_ref[cols, :])
    out_ref[...] = x + _rms(acc, g_ref[5:6, :])


def _mlp(x, g, w_up, w_down):
    n = x.shape[0]
    tm = ROW_TILE
    row = lambda i: (i, 0)
    const = lambda i: (0, 0)
    return pl.pallas_call(
        _mlp_kernel,
        grid=(n // tm,),
        in_specs=[pl.BlockSpec((tm, D_MODEL), row),
                  pl.BlockSpec(g.shape, const),
                  pl.BlockSpec(w_up.shape, const, pipeline_mode=pl.Buffered(1)),
                  pl.BlockSpec(w_down.shape, const, pipeline_mode=pl.Buffered(1))],
        out_specs=pl.BlockSpec((tm, D_MODEL), row),
        out_shape=jax.ShapeDtypeStruct((n, D_MODEL), F32),
        compiler_params=_params("parallel"),
        name="mlp",
    )(x, g, w_up, w_down)


def kernel(x, mem, positions, norm_g, w_in, sgu_w, sgu_b, sgu_ln_g, sgu_ln_b,
           w_out, x_wq, x_wkv, x_wo, w_up, w_down):
    batch, seq, _ = x.shape
    assert x.shape == (batch, seq, D_MODEL) and seq % (BLK * DIL_PATTERNS[-1][1]) == 0
    assert (batch * seq) % ROW_TILE == 0 and seq % ROW_TILE == 0
    assert mem.shape == (batch, MEM_LEN, D_MODEL) and w_in.shape == (DEPTH, D_MODEL, IN_WIDTH)
    xf = x.reshape(batch * seq, D_MODEL)
    memf = mem.reshape(batch * MEM_LEN, D_MODEL)
    cos, sin = _rope_tables(positions)
    for l in range(DEPTH):
        g = norm_g[l]
        za, zb, zc = _inproj(xf, g, w_in[l].astype(BF16))
        ya = _dilated_attention(za, batch, seq)
        yb, yc = _mixers_bc(zb, zc, cos, sin, sgu_w[l], sgu_b[l], sgu_ln_g[l], sgu_ln_b[l],
                            batch, seq)
        xf = _outproj(ya, yb, yc, xf, g, w_out[l].astype(BF16))
        kv = _memory_kv(memf, g, x_wkv[l].astype(BF16))
        xf = _cross(xf, kv, g, x_wq[l].astype(BF16), x_wo[l].astype(BF16), batch, seq)
        xf = _mlp(xf, g, w_up[l].astype(BF16), w_down[l].astype(BF16))
    return xf.reshape(batch, seq, D_MODEL)
```

```python
import functools

import jax
import jax.numpy as jnp
from jax import lax
from jax.experimental import pallas as pl
from jax.experimental.pallas import tpu as pltpu

F32 = jnp.float32
BF16 = jnp.bfloat16

D_MODEL = 1024
DEPTH = 2
MEM_LEN = 256
HEAD_DIM = 64
DIL_HEADS = 6
DIL_PATTERNS = ((128, 1), (512, 4), (2048, 16))
BLK = 128
SGU_GROUPS = 4
RET_HEADS = 6
ROPE_BASE = 10000.0
A_WIDTH = DIL_HEADS * HEAD_DIM
B_WIDTH = SGU_GROUPS * HEAD_DIM
C_WIDTH = RET_HEADS * HEAD_DIM
QKV_A = 3 * A_WIDTH
UV_B = 2 * B_WIDTH
QKVG_C = 4 * C_WIDTH
IN_WIDTH = QKV_A + UV_B + QKVG_C
X_HEADS = 4
X_HEAD_DIM = D_MODEL // X_HEADS
D_FF = 4 * D_MODEL
EPS = 1e-6

LANES = 128
N_PAIRS = A_WIDTH // LANES
VMEM_LIMIT = 56 * 1024 * 1024

SUPER = BLK * DIL_PATTERNS[-1][1]

ROW_TILE = 512
FF_CHUNK = 1024


def _params(*sem):
    return pltpu.CompilerParams(dimension_semantics=sem, vmem_limit_bytes=VMEM_LIMIT)


def _rms(x, g):
    ms = jnp.mean(x * x, axis=-1, keepdims=True)
    return x * lax.rsqrt(ms + EPS) * g


def _dot(a, b):
    return jnp.dot(a, b, preferred_element_type=F32)


def _dot_nt(a, b):
    return lax.dot_general(a, b, (((1,), (1,)), ((), ())), preferred_element_type=F32)


def _dot_tn(a, b):
    return lax.dot_general(a, b, (((0,), (0,)), ((), ())), preferred_element_type=F32)


def _lane_lo(shape):
    lane = lax.broadcasted_iota(jnp.int32, shape, len(shape) - 1)
    return (lane & HEAD_DIM) == 0


def _rope_kernel(pos_ref, inv_ref, cos_ref, sin_ref):
    ang = pos_ref[...].astype(F32) * inv_ref[...]
    lane = lax.broadcasted_iota(jnp.int32, ang.shape, 1)
    sign = jnp.where((lane & (HEAD_DIM // 2)) == 0, -1.0, 1.0)
    cos_ref[...] = jnp.cos(ang)
    sin_ref[...] = jnp.sin(ang) * sign


def _rope_tables(positions):
    n = positions.size
    half = HEAD_DIM // 2
    inv = 1.0 / (ROPE_BASE ** jnp.linspace(0.0, 1.0, half, dtype=F32))
    inv = jnp.tile(inv, LANES // half).reshape(1, LANES)
    tm = 1024
    return pl.pallas_call(
        _rope_kernel,
        grid=(n // tm,),
        in_specs=[pl.BlockSpec((tm, 1), lambda i: (i, 0)),
                  pl.BlockSpec((1, LANES), lambda i: (0, 0))],
        out_specs=[pl.BlockSpec((tm, LANES), lambda i: (i, 0))] * 2,
        out_shape=[jax.ShapeDtypeStruct((n, LANES), F32)] * 2,
        compiler_params=_params("parallel"),
        name="rope_tables",
    )(positions.reshape(n, 1), inv)


def _inproj_kernel(x_ref, g_ref, w_ref, a_ref, b_ref, c_ref):
    h = _rms(x_ref[...], g_ref[0:1, :]).astype(BF16)
    a_ref[...] = _dot(h, w_ref[:, :QKV_A]).astype(BF16)
    b_ref[...] = _dot(h, w_ref[:, QKV_A:QKV_A + UV_B])
    c_ref[...] = _dot(h, w_ref[:, QKV_A + UV_B:])


def _inproj(x, g, w):
    n = x.shape[0]
    tm = ROW_TILE
    return pl.pallas_call(
        _inproj_kernel,
        grid=(n // tm,),
        in_specs=[pl.BlockSpec((tm, D_MODEL), lambda i: (i, 0)),
                  pl.BlockSpec(g.shape, lambda i: (0, 0)),
                  pl.BlockSpec(w.shape, lambda i: (0, 0))],
        out_specs=[pl.BlockSpec((tm, QKV_A), lambda i: (i, 0)),
                   pl.BlockSpec((tm, UV_B), lambda i: (i, 0)),
                   pl.BlockSpec((tm, QKVG_C), lambda i: (i, 0))],
        out_shape=[jax.ShapeDtypeStruct((n, QKV_A), BF16),
                   jax.ShapeDtypeStruct((n, UV_B), F32),
                   jax.ShapeDtypeStruct((n, QKVG_C), F32)],
        compiler_params=_params("parallel"),
        name="inproj",
    )(x, g, w)


MAX_DIL = DIL_PATTERNS[-1][1]


def _block_load(ref, p, base, d):
    run = 8 * d
    parts = [ref[p, pl.ds(pl.multiple_of(base + c * d * BLK, 8), run), :] for c in range(MAX_DIL // d)]
    return parts[0] if len(parts) == 1 else jnp.concatenate(parts, axis=0)


def _block_store(ref, p, base, d, val):
    run = 8 * d
    for c in range(MAX_DIL // d):
        ref[p, pl.ds(pl.multiple_of(base + c * d * BLK, 8), run), :] = val[c * run:(c + 1) * run]


def _dil_block(bias_ref, qp, kring, vring, acc_o, acc_m, acc_l, branch, d, qbase, kcbase, kpbase,
               has_prev, first):
    bias_prev = jnp.where(has_prev, bias_ref[branch, :, :BLK], -jnp.inf)
    bias_cur = bias_ref[branch, :, BLK:]
    lo = _lane_lo((BLK, LANES))
    for p in range(N_PAIRS):
        q = _block_load(qp, p, qbase, d).astype(BF16)
        k = jnp.concatenate([_block_load(kring, p, kpbase, d), _block_load(kring, p, kcbase, d)],
                            axis=0).astype(BF16)
        v = jnp.concatenate([_block_load(vring, p, kpbase, d), _block_load(vring, p, kcbase, d)],
                            axis=0).astype(BF16)
        o_h, m_h, l_h = [], [], []
        for e in range(2):
            qm = jnp.where(lo if e == 0 else ~lo, q, jnp.zeros_like(q))
            s = _dot_nt(qm, k)
            sp = s[:, :BLK] + bias_prev
            sc = s[:, BLK:] + bias_cur
            m = jnp.max(jnp.maximum(sp, sc), axis=-1, keepdims=True)
            pp = jnp.exp(sp - m)
            pc = jnp.exp(sc - m)
            l_h.append(jnp.sum(pp + pc, axis=-1, keepdims=True))
            m_h.append(m)
            o_h.append(_dot(jnp.concatenate([pp, pc], axis=1).astype(BF16), v))
        o_b = jnp.where(lo, o_h[0], o_h[1])
        m_b = jnp.where(lo, m_h[0], m_h[1])
        l_b = jnp.where(lo, l_h[0], l_h[1])
        if first:
            _block_store(acc_o, p, qbase, d, o_b)
            _block_store(acc_m, p, qbase, d, m_b)
            _block_store(acc_l, p, qbase, d, l_b)
        else:
            m_old = _block_load(acc_m, p, qbase, d)
            m_new = jnp.maximum(m_old, m_b)
            a_old = jnp.exp(m_old - m_new)
            a_b = jnp.exp(m_b - m_new)
            _block_store(acc_o, p, qbase, d, _block_load(acc_o, p, qbase, d) * a_old + o_b * a_b)
            _block_store(acc_l, p, qbase, d, _block_load(acc_l, p, qbase, d) * a_old + l_b * a_b)
            _block_store(acc_m, p, qbase, d, m_new)


def _dil_kernel(q_ref, k_ref, v_ref, bias_ref, o_ref, stage, qp, kring, vring, acc_o, acc_m, acc_l):
    t = pl.program_id(1)
    cur = pl.multiple_of((t % 2) * SUPER, SUPER)
    other = pl.multiple_of(SUPER - cur, SUPER)

    @pl.when(t == 0)
    def _():
        kring[:, SUPER:, :] = jnp.zeros((N_PAIRS, SUPER, LANES), F32)
        vring[:, SUPER:, :] = jnp.zeros((N_PAIRS, SUPER, LANES), F32)

    def permute_in(src_ref, dst_ref, dst_off, scale):
        for p in range(N_PAIRS):
            x = src_ref[:, p * LANES:(p + 1) * LANES].astype(F32)
            stage[p] = x if scale is None else x * scale

        def gather(r, carry):
            rows = pl.ds(pl.multiple_of(dst_off + r * BLK, BLK), BLK)
            for p in range(N_PAIRS):
                dst_ref[p, rows, :] = stage[p, pl.ds(r, BLK, stride=MAX_DIL), :]
            return carry

        lax.fori_loop(0, MAX_DIL, gather, 0)

    permute_in(q_ref, qp, 0, HEAD_DIM ** -0.5)
    permute_in(k_ref, kring, cur, None)
    permute_in(v_ref, vring, cur, None)
    args = (bias_ref, qp, kring, vring, acc_o, acc_m, acc_l)

    for branch, (window, d) in enumerate(DIL_PATTERNS):
        nblk = MAX_DIL // d

        def body(j, carry, branch=branch, d=d, nblk=nblk):
            if nblk == 1:
                qbase = pl.multiple_of(j * BLK, BLK)
                kpbase = other + qbase
                has_prev = t > 0
            else:
                r = j // nblk
                nb = j % nblk
                qbase = pl.multiple_of(r * BLK + nb * 8 * d, 8)
                kpbase = pl.multiple_of(
                    jnp.where(nb == 0, other + r * BLK + (nblk - 1) * 8 * d, cur + qbase - 8 * d), 8)
                has_prev = (t > 0) | (nb > 0)
            _dil_block(*args, branch, d, qbase, cur + qbase, kpbase, has_prev, first=(branch == 0))
            return carry

        lax.fori_loop(0, SUPER // BLK, body, 0, unroll=2)

    def scatter(r, carry):
        rows = pl.ds(pl.multiple_of(r * BLK, BLK), BLK)
        for p in range(N_PAIRS):
            stage[p, pl.ds(r, BLK, stride=MAX_DIL), :] = acc_o[p, rows, :] / acc_l[p, rows, :]
        return carry

    lax.fori_loop(0, MAX_DIL, scatter, 0)
    for p in range(N_PAIRS):
        o_ref[:, p * LANES:(p + 1) * LANES] = stage[p].astype(BF16)


def _dilated_bias():
    out = []
    i = jnp.arange(BLK)
    for _, d in DIL_PATTERNS:
        run = 8 * d
        pos = (MAX_DIL // d) * (i % run) + i // run
        prev_ok = pos[None, :] >= pos[:, None]
        cur_ok = pos[None, :] <= pos[:, None]
        out.append(jnp.where(jnp.concatenate([prev_ok, cur_ok], axis=1), 0.0, -jnp.inf))
    return jnp.stack(out).astype(F32)


def _dilated_attention(qkv, batch, seq):
    n = batch * seq
    tiles = seq // SUPER
    bias = _dilated_bias()
    spec = lambda which: pl.BlockSpec((SUPER, A_WIDTH), lambda b, t: (b * tiles + t, which))
    tile_f32 = pltpu.VMEM((N_PAIRS, SUPER, LANES), F32)
    ring_f32 = pltpu.VMEM((N_PAIRS, 2 * SUPER, LANES), F32)
    return pl.pallas_call(
        _dil_kernel,
        grid=(batch, tiles),
        in_specs=[spec(0), spec(1), spec(2), pl.BlockSpec(bias.shape, lambda b, t: (0, 0, 0))],
        out_specs=pl.BlockSpec((SUPER, A_WIDTH), lambda b, t: (b * tiles + t, 0)),
        out_shape=jax.ShapeDtypeStruct((n, A_WIDTH), BF16),
        scratch_shapes=[tile_f32, tile_f32, ring_f32, ring_f32, tile_f32, tile_f32, tile_f32],
        compiler_params=_params("parallel", "arbitrary"),
        name="dilated_attention",
    )(qkv, qkv, qkv, bias)


def _rotate(t, cos, sin_signed, lo_half):
    partner = jnp.where(lo_half, pltpu.roll(t, LANES - HEAD_DIM // 2, 1),
                        pltpu.roll(t, HEAD_DIM // 2, 1))
    return t * cos + partner * sin_signed


def _mix_kernel(b_ref, c_ref, cos_ref, sin_ref, sw_ref, sb_ref, lg_ref, lb_ref,
                dec_ref, qdec_ref, kdec_ref, cdm_ref, yb_ref, yc_ref, state_ref):
    @pl.when(pl.program_id(1) == 0)
    def _():
        state_ref[...] = jnp.zeros_like(state_ref)

    lo = _lane_lo((BLK, LANES))
    row = lax.broadcasted_iota(jnp.int32, (BLK, BLK), 0)
    col = lax.broadcasted_iota(jnp.int32, (BLK, BLK), 1)
    tril = row >= col
    same_head = (row >= HEAD_DIM) == (col >= HEAD_DIM)

    u = jax.nn.gelu(b_ref[:, :B_WIDTH])
    v = jax.nn.gelu(b_ref[:, B_WIDTH:])
    mu = jnp.mean(v, axis=-1, keepdims=True)
    var = jnp.mean(jnp.square(v - mu), axis=-1, keepdims=True)
    v = (v - mu) * lax.rsqrt(var + EPS) * lg_ref[...] + lb_ref[...]
    for gp in range(B_WIDTH // LANES):
        cols = slice(gp * LANES, (gp + 1) * LANES)
        vp = v[:, cols].astype(BF16)
        s = [_dot(jnp.where(tril, sw_ref[2 * gp + e], 0.0).astype(BF16), vp) for e in range(2)]
        gate = jnp.where(lo, s[0], s[1]) + sb_ref[:, cols]
        yb_ref[:, cols] = (u[:, cols] * gate).astype(BF16)

    cos = cos_ref[...]
    sin = sin_ref[...]
    lo_half = (lax.broadcasted_iota(jnp.int32, (BLK, LANES), 1) & (HEAD_DIM // 2)) == 0
    for p in range(N_PAIRS):
        cols = slice(p * LANES, (p + 1) * LANES)
        q = _rotate(c_ref[:, p * LANES:(p + 1) * LANES], cos, sin, lo_half)
        k = _rotate(c_ref[:, C_WIDTH + p * LANES:C_WIDTH + (p + 1) * LANES], cos, sin, lo_half)
        k = k * (HEAD_DIM ** -0.5)
        vb = c_ref[:, 2 * C_WIDTH + p * LANES:2 * C_WIDTH + (p + 1) * LANES].astype(BF16)
        gate = c_ref[:, 3 * C_WIDTH + p * LANES:3 * C_WIDTH + (p + 1) * LANES]
        qb = q.astype(BF16)
        kb = k.astype(BF16)
        ys = []
        for e in range(2):
            qm = jnp.where(lo if e == 0 else ~lo, qb, jnp.zeros_like(qb))
            inner = _dot_nt(qm, kb) * dec_ref[2 * p + e]
            ys.append(_dot(inner.astype(BF16), vb))
        y_in = jnp.where(lo, ys[0], ys[1])
        state = state_ref[p]
        y_x = _dot((q * qdec_ref[:, cols]).astype(BF16), state.astype(BF16))
        kv = _dot_tn((k * kdec_ref[:, cols]).astype(BF16), vb)
        state_ref[p] = state * cdm_ref[p] + jnp.where(same_head, kv, 0.0)
        y = y_in + y_x
        inv_n = 1.0 / HEAD_DIM
        mu_lo = jnp.sum(jnp.where(lo, y, 0.0), axis=-1, keepdims=True) * inv_n
        mu_hi = jnp.sum(jnp.where(lo, 0.0, y), axis=-1, keepdims=True) * inv_n
        yc_ = y - jnp.where(lo, mu_lo, mu_hi)
        sq = yc_ * yc_
        var_lo = jnp.sum(jnp.where(lo, sq, 0.0), axis=-1, keepdims=True) * inv_n
        var_hi = jnp.sum(jnp.where(lo, 0.0, sq), axis=-1, keepdims=True) * inv_n
        yn = yc_ * lax.rsqrt(jnp.where(lo, var_lo, var_hi) + EPS)
        yc_ref[:, cols] = (jax.nn.silu(gate) * yn).astype(BF16)


def _retention_tables():
    heads = jnp.arange(RET_HEADS, dtype=F32)
    log_g = jnp.log1p(-jnp.power(2.0, -5.0 - heads))
    idx = jnp.arange(BLK, dtype=F32)
    rel = idx[:, None] - idx[None, :]
    decay = jnp.where(rel[None] >= 0,
                      jnp.exp(jnp.maximum(rel, 0.0)[None] * log_g[:, None, None]), 0.0)
    k_dec = jnp.exp((BLK - 1 - idx)[:, None] * log_g[None, :])
    q_dec = jnp.exp((idx + 1.0)[:, None] * log_g[None, :])
    k_dec = jnp.repeat(k_dec, HEAD_DIM, axis=1)
    q_dec = jnp.repeat(q_dec, HEAD_DIM, axis=1)
    chunk_decay = jnp.exp(BLK * log_g)
    per_row = jnp.repeat(chunk_decay, HEAD_DIM).reshape(N_PAIRS, LANES)
    half = jnp.arange(LANES) >= HEAD_DIM
    same = half[:, None] == half[None, :]
    cdm = jnp.where(same[None], per_row[:, :, None], 0.0)
    return decay, q_dec, k_dec, cdm


def _mixers_bc(zb, zc, cos, sin, sgu_w, sgu_b, ln_g, ln_b, batch, seq):
    n = batch * seq
    nc = seq // BLK
    decay, q_dec, k_dec, cdm = _retention_tables()
    bias = jnp.repeat(sgu_b.T, HEAD_DIM, axis=1)
    row = lambda b, c: (b * nc + c, 0)
    const2 = lambda b, c: (0, 0)
    const3 = lambda b, c: (0, 0, 0)
    return pl.pallas_call(
        _mix_kernel,
        grid=(batch, nc),
        in_specs=[pl.BlockSpec((BLK, UV_B), row),
                  pl.BlockSpec((BLK, QKVG_C), row),
                  pl.BlockSpec((BLK, LANES), row),
                  pl.BlockSpec((BLK, LANES), row),
                  pl.BlockSpec(sgu_w.shape, const3),
                  pl.BlockSpec(bias.shape, const2),
                  pl.BlockSpec((1, B_WIDTH), const2),
                  pl.BlockSpec((1, B_WIDTH), const2),
                  pl.BlockSpec(decay.shape, const3),
                  pl.BlockSpec(q_dec.shape, const2),
                  pl.BlockSpec(k_dec.shape, const2),
                  pl.BlockSpec(cdm.shape, const3)],
        out_specs=[pl.BlockSpec((BLK, B_WIDTH), row),
                   pl.BlockSpec((BLK, C_WIDTH), row)],
        out_shape=[jax.ShapeDtypeStruct((n, B_WIDTH), BF16),
                   jax.ShapeDtypeStruct((n, C_WIDTH), BF16)],
        scratch_shapes=[pltpu.VMEM((N_PAIRS, LANES, LANES), F32)],
        compiler_params=_params("parallel", "arbitrary"),
        name="sgu_retention",
    )(zb, zc, cos, sin, sgu_w, bias, ln_g.reshape(1, B_WIDTH), ln_b.reshape(1, B_WIDTH),
      decay, q_dec, k_dec, cdm)


def _outproj_kernel(ya_ref, yb_ref, yc_ref, x_ref, g_ref, w_ref, out_ref):
    acc = _dot(ya_ref[...], w_ref[:A_WIDTH, :])
    acc += _dot(yb_ref[...], w_ref[A_WIDTH:A_WIDTH + B_WIDTH, :])
    acc += _dot(yc_ref[...], w_ref[A_WIDTH + B_WIDTH:, :])
    out_ref[...] = x_ref[...] + _rms(acc, g_ref[1:2, :])


def _outproj(ya, yb, yc, x, g, w):
    n = x.shape[0]
    tm = ROW_TILE
    row = lambda i: (i, 0)
    const = lambda i: (0, 0)
    return pl.pallas_call(
        _outproj_kernel,
        grid=(n // tm,),
        in_specs=[pl.BlockSpec((tm, A_WIDTH), row),
                  pl.BlockSpec((tm, B_WIDTH), row),
                  pl.BlockSpec((tm, C_WIDTH), row),
                  pl.BlockSpec((tm, D_MODEL), row),
                  pl.BlockSpec(g.shape, const),
                  pl.BlockSpec(w.shape, const)],
        out_specs=pl.BlockSpec((tm, D_MODEL), row),
        out_shape=jax.ShapeDtypeStruct((n, D_MODEL), F32),
        compiler_params=_params("parallel"),
        name="outproj",
    )(ya, yb, yc, x, g, w)


def _kv_kernel(m_ref, g_ref, w_ref, kv_ref):
    h = _rms(m_ref[...], g_ref[6:7, :]).astype(BF16)
    kv_ref[...] = _dot(h, w_ref[...]).astype(BF16)


def _memory_kv(mem, g, w):
    n = mem.shape[0]
    return pl.pallas_call(
        _kv_kernel,
        grid=(n // MEM_LEN,),
        in_specs=[pl.BlockSpec((MEM_LEN, D_MODEL), lambda i: (i, 0)),
                  pl.BlockSpec(g.shape, lambda i: (0, 0)),
                  pl.BlockSpec(w.shape, lambda i: (0, 0))],
        out_specs=pl.BlockSpec((MEM_LEN, 2 * D_MODEL), lambda i: (i, 0)),
        out_shape=jax.ShapeDtypeStruct((n, 2 * D_MODEL), BF16),
        compiler_params=_params("parallel"),
        name="memory_kv",
    )(mem, g, w)


def _cross_kernel(x_ref, kv_ref, g_ref, wq_ref, wo_ref, out_ref):
    x = x_ref[...]
    h = _rms(x, g_ref[2:3, :]).astype(BF16)
    q = _dot(h, wq_ref[...]).astype(BF16)
    scale = X_HEAD_DIM ** -0.5
    acc = jnp.zeros(x.shape, F32)
    for hd in range(X_HEADS):
        cols = slice(hd * X_HEAD_DIM, (hd + 1) * X_HEAD_DIM)
        k = kv_ref[:, hd * X_HEAD_DIM:(hd + 1) * X_HEAD_DIM]
        v = kv_ref[:, D_MODEL + hd * X_HEAD_DIM:D_MODEL + (hd + 1) * X_HEAD_DIM]
        s = _dot_nt(q[:, cols], k) * scale
        m = jnp.max(s, axis=-1, keepdims=True)
        p = jnp.exp(s - m)
        p = p / jnp.sum(p, axis=-1, keepdims=True)
        o = _dot(p.astype(BF16), v)
        acc += _dot(o.astype(BF16), wo_ref[cols, :])
    out_ref[...] = x + _rms(acc, g_ref[3:4, :])


def _cross(x, kv, g, wq, wo, batch, seq):
    n = x.shape[0]
    tm = ROW_TILE
    per_batch = seq // tm
    row = lambda i: (i, 0)
    const = lambda i: (0, 0)
    return pl.pallas_call(
        _cross_kernel,
        grid=(n // tm,),
        in_specs=[pl.BlockSpec((tm, D_MODEL), row),
                  pl.BlockSpec((MEM_LEN, 2 * D_MODEL), lambda i: (i // per_batch, 0)),
                  pl.BlockSpec(g.shape, const),
                  pl.BlockSpec(wq.shape, const),
                  pl.BlockSpec(wo.shape, const)],
        out_specs=pl.BlockSpec((tm, D_MODEL), row),
        out_shape=jax.ShapeDtypeStruct((n, D_MODEL), F32),
        compiler_params=_params("parallel"),
        name="cross_attention",
    )(x, kv, g, wq, wo)


def _mlp_kernel(x_ref, g_ref, wu_ref, wd_ref, out_ref):
    x = x_ref[...]
    h = _rms(x, g_ref[4:5, :]).astype(BF16)
    acc = jnp.zeros(x.shape, F32)
    for c in range(D_FF // FF_CHUNK):
        cols = slice(c * FF_CHUNK, (c + 1) * FF_CHUNK)
        f = jnp.square(jnp.maximum(_dot(h, wu_ref[:, cols]), 0.0))
        acc += _dot(f.astype(BF16), wd_ref[cols, :])
    out_ref[...] = x + _rms(acc, g_ref[5:6, :])


def _mlp(x, g, w_up, w_down):
    n = x.shape[0]
    tm = ROW_TILE
    row = lambda i: (i, 0)
    const = lambda i: (0, 0)
    return pl.pallas_call(
        _mlp_kernel,
        grid=(n // tm,),
        in_specs=[pl.BlockSpec((tm, D_MODEL), row),
                  pl.BlockSpec(g.shape, const),
                  pl.BlockSpec(w_up.shape, const, pipeline_mode=pl.Buffered(1)),
                  pl.BlockSpec(w_down.shape, const, pipeline_mode=pl.Buffered(1))],
        out_specs=pl.BlockSpec((tm, D_MODEL), row),
        out_shape=jax.ShapeDtypeStruct((n, D_MODEL), F32),
        compiler_params=_params("parallel"),
        name="mlp",
    )(x, g, w_up, w_down)


def kernel(x, mem, positions, norm_g, w_in, sgu_w, sgu_b, sgu_ln_g, sgu_ln_b,
           w_out, x_wq, x_wkv, x_wo, w_up, w_down):
    batch, seq, _ = x.shape
    assert x.shape == (batch, seq, D_MODEL) and seq % (BLK * DIL_PATTERNS[-1][1]) == 0
    assert (batch * seq) % ROW_TILE == 0 and seq % ROW_TILE == 0
    assert mem.shape == (batch, MEM_LEN, D_MODEL) and w_in.shape == (DEPTH, D_MODEL, IN_WIDTH)
    xf = x.reshape(batch * seq, D_MODEL)
    memf = mem.reshape(batch * MEM_LEN, D_MODEL)
    cos, sin = _rope_tables(positions)
    for l in range(DEPTH):
        g = norm_g[l]
        za, zb, zc = _inproj(xf, g, w_in[l].astype(BF16))
        ya = _dilated_attention(za, batch, seq)
        yb, yc = _mixers_bc(zb, zc, cos, sin, sgu_w[l], sgu_b[l], sgu_ln_g[l], sgu_ln_b[l],
                            batch, seq)
        xf = _outproj(ya, yb, yc, xf, g, w_out[l].astype(BF16))
        kv = _memory_kv(memf, g, x_wkv[l].astype(BF16))
        xf = _cross(xf, kv, g, x_wq[l].astype(BF16), x_wo[l].astype(BF16), batch, seq)
        xf = _mlp(xf, g, w_up[l].astype(BF16), w_down[l].astype(BF16))
    return xf.reshape(batch, seq, D_MODEL)
```

```python
import functools

import jax
import jax.numpy as jnp
from jax import lax
from jax.experimental import pallas as pl
from jax.experimental.pallas import tpu as pltpu

F32 = jnp.float32
BF16 = jnp.bfloat16

D_MODEL = 1024
DEPTH = 2
MEM_LEN = 256
HEAD_DIM = 64
DIL_HEADS = 6
DIL_PATTERNS = ((128, 1), (512, 4), (2048, 16))
BLK = 128
SGU_GROUPS = 4
RET_HEADS = 6
ROPE_BASE = 10000.0
A_WIDTH = DIL_HEADS * HEAD_DIM
B_WIDTH = SGU_GROUPS * HEAD_DIM
C_WIDTH = RET_HEADS * HEAD_DIM
QKV_A = 3 * A_WIDTH
UV_B = 2 * B_WIDTH
QKVG_C = 4 * C_WIDTH
IN_WIDTH = QKV_A + UV_B + QKVG_C
X_HEADS = 4
X_HEAD_DIM = D_MODEL // X_HEADS
D_FF = 4 * D_MODEL
EPS = 1e-6

LANES = 128
N_PAIRS = A_WIDTH // LANES
VMEM_LIMIT = 56 * 1024 * 1024

SUPER = BLK * DIL_PATTERNS[-1][1]

ROW_TILE = 512
MIX_CHUNKS = 8
FF_CHUNK = 1024


def _params(*sem):
    return pltpu.CompilerParams(dimension_semantics=sem, vmem_limit_bytes=VMEM_LIMIT)


def _rms(x, g):
    ms = jnp.mean(x * x, axis=-1, keepdims=True)
    return x * lax.rsqrt(ms + EPS) * g


def _dot(a, b):
    return jnp.dot(a, b, preferred_element_type=F32)


def _dot_nt(a, b):
    return lax.dot_general(a, b, (((1,), (1,)), ((), ())), preferred_element_type=F32)


def _dot_tn(a, b):
    return lax.dot_general(a, b, (((0,), (0,)), ((), ())), preferred_element_type=F32)


def _lane_lo(shape):
    lane = lax.broadcasted_iota(jnp.int32, shape, len(shape) - 1)
    return (lane & HEAD_DIM) == 0


def _rope_kernel(pos_ref, inv_ref, cos_ref, sin_ref):
    ang = pos_ref[...].astype(F32) * inv_ref[...]
    lane = lax.broadcasted_iota(jnp.int32, ang.shape, 1)
    sign = jnp.where((lane & (HEAD_DIM // 2)) == 0, -1.0, 1.0)
    cos_ref[...] = jnp.cos(ang)
    sin_ref[...] = jnp.sin(ang) * sign


def _rope_tables(positions):
    n = positions.size
    half = HEAD_DIM // 2
    inv = 1.0 / (ROPE_BASE ** jnp.linspace(0.0, 1.0, half, dtype=F32))
    inv = jnp.tile(inv, LANES // half).reshape(1, LANES)
    tm = 1024
    return pl.pallas_call(
        _rope_kernel,
        grid=(n // tm,),
        in_specs=[pl.BlockSpec((tm, 1), lambda i: (i, 0)),
                  pl.BlockSpec((1, LANES), lambda i: (0, 0))],
        out_specs=[pl.BlockSpec((tm, LANES), lambda i: (i, 0))] * 2,
        out_shape=[jax.ShapeDtypeStruct((n, LANES), F32)] * 2,
        compiler_params=_params("parallel"),
        name="rope_tables",
    )(positions.reshape(n, 1), inv)


def _inproj_kernel(x_ref, g_ref, w_ref, a_ref, b_ref, c_ref):
    h = _rms(x_ref[...], g_ref[0:1, :]).astype(BF16)
    a_ref[...] = _dot(h, w_ref[:, :QKV_A]).astype(BF16)
    b_ref[...] = _dot(h, w_ref[:, QKV_A:QKV_A + UV_B])
    c_ref[...] = _dot(h, w_ref[:, QKV_A + UV_B:])


def _inproj(x, g, w):
    n = x.shape[0]
    tm = ROW_TILE
    return pl.pallas_call(
        _inproj_kernel,
        grid=(n // tm,),
        in_specs=[pl.BlockSpec((tm, D_MODEL), lambda i: (i, 0)),
                  pl.BlockSpec(g.shape, lambda i: (0, 0)),
                  pl.BlockSpec(w.shape, lambda i: (0, 0))],
        out_specs=[pl.BlockSpec((tm, QKV_A), lambda i: (i, 0)),
                   pl.BlockSpec((tm, UV_B), lambda i: (i, 0)),
                   pl.BlockSpec((tm, QKVG_C), lambda i: (i, 0))],
        out_shape=[jax.ShapeDtypeStruct((n, QKV_A), BF16),
                   jax.ShapeDtypeStruct((n, UV_B), F32),
                   jax.ShapeDtypeStruct((n, QKVG_C), F32)],
        compiler_params=_params("parallel"),
        name="inproj",
    )(x, g, w)


MAX_DIL = DIL_PATTERNS[-1][1]


def _block_load(ref, p, base, d):
    run = 8 * d
    parts = [ref[p, pl.ds(pl.multiple_of(base + c * d * BLK, 8), run), :] for c in range(MAX_DIL // d)]
    return parts[0] if len(parts) == 1 else jnp.concatenate(parts, axis=0)


def _block_store(ref, p, base, d, val):
    run = 8 * d
    for c in range(MAX_DIL // d):
        ref[p, pl.ds(pl.multiple_of(base + c * d * BLK, 8), run), :] = val[c * run:(c + 1) * run]


def _dil_block(bias_ref, qp, kring, vring, acc_o, acc_m, acc_l, branch, d, qbase, kcbase, kpbase,
               has_prev, first):
    bias_prev = jnp.where(has_prev, bias_ref[branch, :, :BLK], -jnp.inf)
    bias_cur = bias_ref[branch, :, BLK:]
    lo = _lane_lo((BLK, LANES))
    for p in range(N_PAIRS):
        q = _block_load(qp, p, qbase, d).astype(BF16)
        k = jnp.concatenate([_block_load(kring, p, kpbase, d), _block_load(kring, p, kcbase, d)],
                            axis=0).astype(BF16)
        v = jnp.concatenate([_block_load(vring, p, kpbase, d), _block_load(vring, p, kcbase, d)],
                            axis=0).astype(BF16)
        o_h, m_h, l_h = [], [], []
        for e in range(2):
            qm = jnp.where(lo if e == 0 else ~lo, q, jnp.zeros_like(q))
            s = _dot_nt(qm, k)
            sp = s[:, :BLK] + bias_prev
            sc = s[:, BLK:] + bias_cur
            m = jnp.max(jnp.maximum(sp, sc), axis=-1, keepdims=True)
            pp = jnp.exp(sp - m)
            pc = jnp.exp(sc - m)
            l_h.append(jnp.sum(pp + pc, axis=-1, keepdims=True))
            m_h.append(m)
            o_h.append(_dot(jnp.concatenate([pp, pc], axis=1).astype(BF16), v))
        o_b = jnp.where(lo, o_h[0], o_h[1])
        m_b = jnp.where(lo, m_h[0], m_h[1])
        l_b = jnp.where(lo, l_h[0], l_h[1])
        if first:
            _block_store(acc_o, p, qbase, d, o_b)
            _block_store(acc_m, p, qbase, d, m_b)
            _block_store(acc_l, p, qbase, d, l_b)
        else:
            m_old = _block_load(acc_m, p, qbase, d)
            m_new = jnp.maximum(m_old, m_b)
            a_old = jnp.exp(m_old - m_new)
            a_b = jnp.exp(m_b - m_new)
            _block_store(acc_o, p, qbase, d, _block_load(acc_o, p, qbase, d) * a_old + o_b * a_b)
            _block_store(acc_l, p, qbase, d, _block_load(acc_l, p, qbase, d) * a_old + l_b * a_b)
            _block_store(acc_m, p, qbase, d, m_new)


def _dil_kernel(q_ref, k_ref, v_ref, bias_ref, o_ref, stage, qp, kring, vring, acc_o, acc_m, acc_l):
    t = pl.program_id(1)
    cur = pl.multiple_of((t % 2) * SUPER, SUPER)
    other = pl.multiple_of(SUPER - cur, SUPER)

    @pl.when(t == 0)
    def _():
        kring[:, SUPER:, :] = jnp.zeros((N_PAIRS, SUPER, LANES), F32)
        vring[:, SUPER:, :] = jnp.zeros((N_PAIRS, SUPER, LANES), F32)

    def permute_in(src_ref, dst_ref, dst_off, scale):
        for p in range(N_PAIRS):
            x = src_ref[:, p * LANES:(p + 1) * LANES].astype(F32)
            stage[p] = x if scale is None else x * scale

        def gather(r, carry):
            rows = pl.ds(pl.multiple_of(dst_off + r * BLK, BLK), BLK)
            for p in range(N_PAIRS):
                dst_ref[p, rows, :] = stage[p, pl.ds(r, BLK, stride=MAX_DIL), :]
            return carry

        lax.fori_loop(0, MAX_DIL, gather, 0)

    permute_in(q_ref, qp, 0, HEAD_DIM ** -0.5)
    permute_in(k_ref, kring, cur, None)
    permute_in(v_ref, vring, cur, None)
    args = (bias_ref, qp, kring, vring, acc_o, acc_m, acc_l)

    for branch, (window, d) in enumerate(DIL_PATTERNS):
        nblk = MAX_DIL // d

        def body(j, carry, branch=branch, d=d, nblk=nblk):
            if nblk == 1:
                qbase = pl.multiple_of(j * BLK, BLK)
                kpbase = other + qbase
                has_prev = t > 0
            else:
                r = j // nblk
                nb = j % nblk
                qbase = pl.multiple_of(r * BLK + nb * 8 * d, 8)
                kpbase = pl.multiple_of(
                    jnp.where(nb == 0, other + r * BLK + (nblk - 1) * 8 * d, cur + qbase - 8 * d), 8)
                has_prev = (t > 0) | (nb > 0)
            _dil_block(*args, branch, d, qbase, cur + qbase, kpbase, has_prev, first=(branch == 0))
            return carry

        lax.fori_loop(0, SUPER // BLK, body, 0, unroll=2)

    def scatter(r, carry):
        rows = pl.ds(pl.multiple_of(r * BLK, BLK), BLK)
        for p in range(N_PAIRS):
            stage[p, pl.ds(r, BLK, stride=MAX_DIL), :] = acc_o[p, rows, :] / acc_l[p, rows, :]
        return carry

    lax.fori_loop(0, MAX_DIL, scatter, 0)
    for p in range(N_PAIRS):
        o_ref[:, p * LANES:(p + 1) * LANES] = stage[p].astype(BF16)


def _dilated_bias():
    out = []
    i = jnp.arange(BLK)
    for _, d in DIL_PATTERNS:
        run = 8 * d
        pos = (MAX_DIL // d) * (i % run) + i // run
        prev_ok = pos[None, :] >= pos[:, None]
        cur_ok = pos[None, :] <= pos[:, None]
        out.append(jnp.where(jnp.concatenate([prev_ok, cur_ok], axis=1), 0.0, -jnp.inf))
    return jnp.stack(out).astype(F32)


def _dilated_attention(qkv, batch, seq):
    n = batch * seq
    tiles = seq // SUPER
    bias = _dilated_bias()
    spec = lambda which: pl.BlockSpec((SUPER, A_WIDTH), lambda b, t: (b * tiles + t, which))
    tile_f32 = pltpu.VMEM((N_PAIRS, SUPER, LANES), F32)
    ring_f32 = pltpu.VMEM((N_PAIRS, 2 * SUPER, LANES), F32)
    return pl.pallas_call(
        _dil_kernel,
        grid=(batch, tiles),
        in_specs=[spec(0), spec(1), spec(2), pl.BlockSpec(bias.shape, lambda b, t: (0, 0, 0))],
        out_specs=pl.BlockSpec((SUPER, A_WIDTH), lambda b, t: (b * tiles + t, 0)),
        out_shape=jax.ShapeDtypeStruct((n, A_WIDTH), BF16),
        scratch_shapes=[tile_f32, tile_f32, ring_f32, ring_f32, tile_f32, tile_f32, tile_f32],
        compiler_params=_params("parallel", "arbitrary"),
        name="dilated_attention",
    )(qkv, qkv, qkv, bias)


def _rotate(t, cos, sin_signed, lo_half):
    partner = jnp.where(lo_half, pltpu.roll(t, LANES - HEAD_DIM // 2, 1),
                        pltpu.roll(t, HEAD_DIM // 2, 1))
    return t * cos + partner * sin_signed


def _mix_kernel(b_ref, c_ref, cos_ref, sin_ref, sw_ref, sb_ref, lg_ref, lb_ref,
                dec_ref, qdec_ref, kdec_ref, cdm_ref, yb_ref, yc_ref, state_ref):
    @pl.when(pl.program_id(1) == 0)
    def _():
        state_ref[...] = jnp.zeros_like(state_ref)

    lo = _lane_lo((BLK, LANES))
    row = lax.broadcasted_iota(jnp.int32, (BLK, BLK), 0)
    col = lax.broadcasted_iota(jnp.int32, (BLK, BLK), 1)
    tril = row >= col
    same_head = (row >= HEAD_DIM) == (col >= HEAD_DIM)
    lo_half = (lax.broadcasted_iota(jnp.int32, (BLK, LANES), 1) & (HEAD_DIM // 2)) == 0
    w_tril = [jnp.where(tril, sw_ref[g], 0.0).astype(BF16) for g in range(SGU_GROUPS)]
    states = [state_ref[p] for p in range(N_PAIRS)]
    inv_n = 1.0 / HEAD_DIM

    for c in range(MIX_CHUNKS):
        rows = slice(c * BLK, (c + 1) * BLK)
        u = jax.nn.gelu(b_ref[rows, :B_WIDTH])
        v = jax.nn.gelu(b_ref[rows, B_WIDTH:])
        mu = jnp.mean(v, axis=-1, keepdims=True)
        var = jnp.mean(jnp.square(v - mu), axis=-1, keepdims=True)
        v = (v - mu) * lax.rsqrt(var + EPS) * lg_ref[...] + lb_ref[...]
        for gp in range(B_WIDTH // LANES):
            cols = slice(gp * LANES, (gp + 1) * LANES)
            vp = v[:, cols].astype(BF16)
            s = [_dot(w_tril[2 * gp + e], vp) for e in range(2)]
            gate = jnp.where(lo, s[0], s[1]) + sb_ref[:, cols]
            yb_ref[rows, cols] = (u[:, cols] * gate).astype(BF16)

        cos = cos_ref[rows, :]
        sin = sin_ref[rows, :]
        for p in range(N_PAIRS):
            cols = slice(p * LANES, (p + 1) * LANES)
            q = _rotate(c_ref[rows, p * LANES:(p + 1) * LANES], cos, sin, lo_half)
            k = _rotate(c_ref[rows, C_WIDTH + p * LANES:C_WIDTH + (p + 1) * LANES], cos, sin, lo_half)
            k = k * (HEAD_DIM ** -0.5)
            vb = c_ref[rows, 2 * C_WIDTH + p * LANES:2 * C_WIDTH + (p + 1) * LANES].astype(BF16)
            gate = c_ref[rows, 3 * C_WIDTH + p * LANES:3 * C_WIDTH + (p + 1) * LANES]
            qb = q.astype(BF16)
            kb = k.astype(BF16)
            ys = []
            for e in range(2):
                qm = jnp.where(lo if e == 0 else ~lo, qb, jnp.zeros_like(qb))
                inner = _dot_nt(qm, kb) * dec_ref[2 * p + e]
                ys.append(_dot(inner.astype(BF16), vb))
            y_in = jnp.where(lo, ys[0], ys[1])
            y_x = _dot((q * qdec_ref[:, cols]).astype(BF16), states[p].astype(BF16))
            kv = _dot_tn((k * kdec_ref[:, cols]).astype(BF16), vb)
            states[p] = states[p] * cdm_ref[p] + jnp.where(same_head, kv, 0.0)
            y = y_in + y_x
            mu_lo = jnp.sum(jnp.where(lo, y, 0.0), axis=-1, keepdims=True) * inv_n
            mu_hi = jnp.sum(jnp.where(lo, 0.0, y), axis=-1, keepdims=True) * inv_n
            yc_ = y - jnp.where(lo, mu_lo, mu_hi)
            sq = yc_ * yc_
            var_lo = jnp.sum(jnp.where(lo, sq, 0.0), axis=-1, keepdims=True) * inv_n
            var_hi = jnp.sum(jnp.where(lo, 0.0, sq), axis=-1, keepdims=True) * inv_n
            yn = yc_ * lax.rsqrt(jnp.where(lo, var_lo, var_hi) + EPS)
            yc_ref[rows, cols] = (jax.nn.silu(gate) * yn).astype(BF16)

    for p in range(N_PAIRS):
        state_ref[p] = states[p]


def _retention_tables():
    heads = jnp.arange(RET_HEADS, dtype=F32)
    log_g = jnp.log1p(-jnp.power(2.0, -5.0 - heads))
    idx = jnp.arange(BLK, dtype=F32)
    rel = idx[:, None] - idx[None, :]
    decay = jnp.where(rel[None] >= 0,
                      jnp.exp(jnp.maximum(rel, 0.0)[None] * log_g[:, None, None]), 0.0)
    k_dec = jnp.exp((BLK - 1 - idx)[:, None] * log_g[None, :])
    q_dec = jnp.exp((idx + 1.0)[:, None] * log_g[None, :])
    k_dec = jnp.repeat(k_dec, HEAD_DIM, axis=1)
    q_dec = jnp.repeat(q_dec, HEAD_DIM, axis=1)
    chunk_decay = jnp.exp(BLK * log_g)
    per_row = jnp.repeat(chunk_decay, HEAD_DIM).reshape(N_PAIRS, LANES)
    half = jnp.arange(LANES) >= HEAD_DIM
    same = half[:, None] == half[None, :]
    cdm = jnp.where(same[None], per_row[:, :, None], 0.0)
    return decay, q_dec, k_dec, cdm


def _mixers_bc(zb, zc, cos, sin, sgu_w, sgu_b, ln_g, ln_b, batch, seq):
    n = batch * seq
    tm = MIX_CHUNKS * BLK
    nc = seq // tm
    decay, q_dec, k_dec, cdm = _retention_tables()
    bias = jnp.repeat(sgu_b.T, HEAD_DIM, axis=1)
    row = lambda b, c: (b * nc + c, 0)
    const2 = lambda b, c: (0, 0)
    const3 = lambda b, c: (0, 0, 0)
    return pl.pallas_call(
        _mix_kernel,
        grid=(batch, nc),
        in_specs=[pl.BlockSpec((tm, UV_B), row),
                  pl.BlockSpec((tm, QKVG_C), row),
                  pl.BlockSpec((tm, LANES), row),
                  pl.BlockSpec((tm, LANES), row),
                  pl.BlockSpec(sgu_w.shape, const3),
                  pl.BlockSpec(bias.shape, const2),
                  pl.BlockSpec((1, B_WIDTH), const2),
                  pl.BlockSpec((1, B_WIDTH), const2),
                  pl.BlockSpec(decay.shape, const3),
                  pl.BlockSpec(q_dec.shape, const2),
                  pl.BlockSpec(k_dec.shape, const2),
                  pl.BlockSpec(cdm.shape, const3)],
        out_specs=[pl.BlockSpec((tm, B_WIDTH), row),
                   pl.BlockSpec((tm, C_WIDTH), row)],
        out_shape=[jax.ShapeDtypeStruct((n, B_WIDTH), BF16),
                   jax.ShapeDtypeStruct((n, C_WIDTH), BF16)],
        scratch_shapes=[pltpu.VMEM((N_PAIRS, LANES, LANES), F32)],
        compiler_params=_params("parallel", "arbitrary"),
        name="sgu_retention",
    )(zb, zc, cos, sin, sgu_w, bias, ln_g.reshape(1, B_WIDTH), ln_b.reshape(1, B_WIDTH),
      decay, q_dec, k_dec, cdm)


def _outproj_kernel(ya_ref, yb_ref, yc_ref, x_ref, g_ref, w_ref, out_ref):
    acc = _dot(ya_ref[...], w_ref[:A_WIDTH, :])
    acc += _dot(yb_ref[...], w_ref[A_WIDTH:A_WIDTH + B_WIDTH, :])
    acc += _dot(yc_ref[...], w_ref[A_WIDTH + B_WIDTH:, :])
    out_ref[...] = x_ref[...] + _rms(acc, g_ref[1:2, :])


def _outproj(ya, yb, yc, x, g, w):
    n = x.shape[0]
    tm = ROW_TILE
    row = lambda i: (i, 0)
    const = lambda i: (0, 0)
    return pl.pallas_call(
        _outproj_kernel,
        grid=(n // tm,),
        in_specs=[pl.BlockSpec((tm, A_WIDTH), row),
                  pl.BlockSpec((tm, B_WIDTH), row),
                  pl.BlockSpec((tm, C_WIDTH), row),
                  pl.BlockSpec((tm, D_MODEL), row),
                  pl.BlockSpec(g.shape, const),
                  pl.BlockSpec(w.shape, const)],
        out_specs=pl.BlockSpec((tm, D_MODEL), row),
        out_shape=jax.ShapeDtypeStruct((n, D_MODEL), F32),
        compiler_params=_params("parallel"),
        name="outproj",
    )(ya, yb, yc, x, g, w)


def _kv_kernel(m_ref, g_ref, w_ref, kv_ref):
    h = _rms(m_ref[...], g_ref[6:7, :]).astype(BF16)
    kv_ref[...] = _dot(h, w_ref[...]).astype(BF16)


def _memory_kv(mem, g, w):
    n = mem.shape[0]
    return pl.pallas_call(
        _kv_kernel,
        grid=(n // MEM_LEN,),
        in_specs=[pl.BlockSpec((MEM_LEN, D_MODEL), lambda i: (i, 0)),
                  pl.BlockSpec(g.shape, lambda i: (0, 0)),
                  pl.BlockSpec(w.shape, lambda i: (0, 0))],
        out_specs=pl.BlockSpec((MEM_LEN, 2 * D_MODEL), lambda i: (i, 0)),
        out_shape=jax.ShapeDtypeStruct((n, 2 * D_MODEL), BF16),
        compiler_params=_params("parallel"),
        name="memory_kv",
    )(mem, g, w)


def _cross_kernel(x_ref, kv_ref, g_ref, wq_ref, wo_ref, out_ref):
    x = x_ref[...]
    h = _rms(x, g_ref[2:3, :]).astype(BF16)
    q = _dot(h, wq_ref[...]).astype(BF16)
    scale = X_HEAD_DIM ** -0.5
    acc = jnp.zeros(x.shape, F32)
    for hd in range(X_HEADS):
        cols = slice(hd * X_HEAD_DIM, (hd + 1) * X_HEAD_DIM)
        k = kv_ref[:, hd * X_HEAD_DIM:(hd + 1) * X_HEAD_DIM]
        v = kv_ref[:, D_MODEL + hd * X_HEAD_DIM:D_MODEL + (hd + 1) * X_HEAD_DIM]
        s = _dot_nt(q[:, cols], k) * scale
        m = jnp.max(s, axis=-1, keepdims=True)
        p = jnp.exp(s - m)
        p = p / jnp.sum(p, axis=-1, keepdims=True)
        o = _dot(p.astype(BF16), v)
        acc += _dot(o.astype(BF16), wo_ref[cols, :])
    out_ref[...] = x + _rms(acc, g_ref[3:4, :])


def _cross(x, kv, g, wq, wo, batch, seq):
    n = x.shape[0]
    tm = ROW_TILE
    per_batch = seq // tm
    row = lambda i: (i, 0)
    const = lambda i: (0, 0)
    return pl.pallas_call(
        _cross_kernel,
        grid=(n // tm,),
        in_specs=[pl.BlockSpec((tm, D_MODEL), row),
                  pl.BlockSpec((MEM_LEN, 2 * D_MODEL), lambda i: (i // per_batch, 0)),
                  pl.BlockSpec(g.shape, const),
                  pl.BlockSpec(wq.shape, const),
                  pl.BlockSpec(wo.shape, const)],
        out_specs=pl.BlockSpec((tm, D_MODEL), row),
        out_shape=jax.ShapeDtypeStruct((n, D_MODEL), F32),
        compiler_params=_params("parallel"),
        name="cross_attention",
    )(x, kv, g, wq, wo)


def _mlp_kernel(x_ref, g_ref, wu_ref, wd_ref, out_ref):
    x = x_ref[...]
    h = _rms(x, g_ref[4:5, :]).astype(BF16)
    acc = jnp.zeros(x.shape, F32)
    for c in range(D_FF // FF_CHUNK):
        cols = slice(c * FF_CHUNK, (c + 1) * FF_CHUNK)
        f = jnp.square(jnp.maximum(_dot(h, wu_ref[:, cols]), 0.0))
        acc += _dot(f.astype(BF16), wd_ref[cols, :])
    out_ref[...] = x + _rms(acc, g_ref[5:6, :])


def _mlp(x, g, w_up, w_down):
    n = x.shape[0]
    tm = ROW_TILE
    row = lambda i: (i, 0)
    const = lambda i: (0, 0)
    return pl.pallas_call(
        _mlp_kernel,
        grid=(n // tm,),
        in_specs=[pl.BlockSpec((tm, D_MODEL), row),
                  pl.BlockSpec(g.shape, const),
                  pl.BlockSpec(w_up.shape, const, pipeline_mode=pl.Buffered(1)),
                  pl.BlockSpec(w_down.shape, const, pipeline_mode=pl.Buffered(1))],
        out_specs=pl.BlockSpec((tm, D_MODEL), row),
        out_shape=jax.ShapeDtypeStruct((n, D_MODEL), F32),
        compiler_params=_params("parallel"),
        name="mlp",
    )(x, g, w_up, w_down)


def kernel(x, mem, positions, norm_g, w_in, sgu_w, sgu_b, sgu_ln_g, sgu_ln_b,
           w_out, x_wq, x_wkv, x_wo, w_up, w_down):
    batch, seq, _ = x.shape
    assert x.shape == (batch, seq, D_MODEL) and seq % (BLK * DIL_PATTERNS[-1][1]) == 0
    assert (batch * seq) % ROW_TILE == 0 and seq % ROW_TILE == 0
    assert mem.shape == (batch, MEM_LEN, D_MODEL) and w_in.shape == (DEPTH, D_MODEL, IN_WIDTH)
    xf = x.reshape(batch * seq, D_MODEL)
    memf = mem.reshape(batch * MEM_LEN, D_MODEL)
    cos, sin = _rope_tables(positions)
    for l in range(DEPTH):
        g = norm_g[l]
        za, zb, zc = _inproj(xf, g, w_in[l].astype(BF16))
        ya = _dilated_attention(za, batch, seq)
        yb, yc = _mixers_bc(zb, zc, cos, sin, sgu_w[l], sgu_b[l], sgu_ln_g[l], sgu_ln_b[l],
                            batch, seq)
        xf = _outproj(ya, yb, yc, xf, g, w_out[l].astype(BF16))
        kv = _memory_kv(memf, g, x_wkv[l].astype(BF16))
        xf = _cross(xf, kv, g, x_wq[l].astype(BF16), x_wo[l].astype(BF16), batch, seq)
        xf = _mlp(xf, g, w_up[l].astype(BF16), w_down[l].astype(BF16))
    return xf.reshape(batch, seq, D_MODEL)
```

```python
import functools

import jax
import jax.numpy as jnp
from jax import lax
from jax.experimental import pallas as pl
from jax.experimental.pallas import tpu as pltpu

F32 = jnp.float32
BF16 = jnp.bfloat16

D_MODEL = 1024
DEPTH = 2
MEM_LEN = 256
HEAD_DIM = 64
DIL_HEADS = 6
DIL_PATTERNS = ((128, 1), (512, 4), (2048, 16))
BLK = 128
SGU_GROUPS = 4
RET_HEADS = 6
ROPE_BASE = 10000.0
A_WIDTH = DIL_HEADS * HEAD_DIM
B_WIDTH = SGU_GROUPS * HEAD_DIM
C_WIDTH = RET_HEADS * HEAD_DIM
QKV_A = 3 * A_WIDTH
UV_B = 2 * B_WIDTH
QKVG_C = 4 * C_WIDTH
IN_WIDTH = QKV_A + UV_B + QKVG_C
X_HEADS = 4
X_HEAD_DIM = D_MODEL // X_HEADS
D_FF = 4 * D_MODEL
EPS = 1e-6

LANES = 128
N_PAIRS = A_WIDTH // LANES
VMEM_LIMIT = 56 * 1024 * 1024

SUPER = BLK * DIL_PATTERNS[-1][1]

ROW_TILE = 512
MIX_CHUNKS = 8
FF_CHUNK = 1024


def _params(*sem):
    return pltpu.CompilerParams(dimension_semantics=sem, vmem_limit_bytes=VMEM_LIMIT)


def _rms(x, g):
    ms = jnp.mean(x * x, axis=-1, keepdims=True)
    return x * lax.rsqrt(ms + EPS) * g


def _dot(a, b):
    return jnp.dot(a, b, preferred_element_type=F32)


def _dot_nt(a, b):
    return lax.dot_general(a, b, (((1,), (1,)), ((), ())), preferred_element_type=F32)


def _dot_tn(a, b):
    return lax.dot_general(a, b, (((0,), (0,)), ((), ())), preferred_element_type=F32)


def _lane_lo(shape):
    lane = lax.broadcasted_iota(jnp.int32, shape, len(shape) - 1)
    return (lane & HEAD_DIM) == 0


def _rope_kernel(pos_ref, inv_ref, cos_ref, sin_ref):
    ang = pos_ref[...].astype(F32) * inv_ref[...]
    lane = lax.broadcasted_iota(jnp.int32, ang.shape, 1)
    sign = jnp.where((lane & (HEAD_DIM // 2)) == 0, -1.0, 1.0)
    cos_ref[...] = jnp.cos(ang)
    sin_ref[...] = jnp.sin(ang) * sign


def _rope_tables(positions):
    n = positions.size
    half = HEAD_DIM // 2
    inv = 1.0 / (ROPE_BASE ** jnp.linspace(0.0, 1.0, half, dtype=F32))
    inv = jnp.tile(inv, LANES // half).reshape(1, LANES)
    tm = 1024
    return pl.pallas_call(
        _rope_kernel,
        grid=(n // tm,),
        in_specs=[pl.BlockSpec((tm, 1), lambda i: (i, 0)),
                  pl.BlockSpec((1, LANES), lambda i: (0, 0))],
        out_specs=[pl.BlockSpec((tm, LANES), lambda i: (i, 0))] * 2,
        out_shape=[jax.ShapeDtypeStruct((n, LANES), F32)] * 2,
        compiler_params=_params("parallel"),
        name="rope_tables",
    )(positions.reshape(n, 1), inv)


def _inproj_kernel(x_ref, g_ref, w_ref, perm_ref, a_ref, b_ref, c_ref):
    h = _rms(x_ref[...], g_ref[0:1, :]).astype(BF16)
    hp = jnp.concatenate([_dot(perm_ref[...], h[g * BLK:(g + 1) * BLK]).astype(BF16)
                          for g in range(h.shape[0] // BLK)], axis=0)
    a_ref[...] = _dot(hp, w_ref[:, :QKV_A]).astype(BF16)
    b_ref[...] = _dot(h, w_ref[:, QKV_A:QKV_A + UV_B])
    c_ref[...] = _dot(h, w_ref[:, QKV_A + UV_B:])


def _inproj(x, g, w, perm):
    n = x.shape[0]
    tm = ROW_TILE
    return pl.pallas_call(
        _inproj_kernel,
        grid=(n // tm,),
        in_specs=[pl.BlockSpec((tm, D_MODEL), lambda i: (i, 0)),
                  pl.BlockSpec(g.shape, lambda i: (0, 0)),
                  pl.BlockSpec(w.shape, lambda i: (0, 0)),
                  pl.BlockSpec(perm.shape, lambda i: (0, 0))],
        out_specs=[pl.BlockSpec((tm, QKV_A), lambda i: (i, 0)),
                   pl.BlockSpec((tm, UV_B), lambda i: (i, 0)),
                   pl.BlockSpec((tm, QKVG_C), lambda i: (i, 0))],
        out_shape=[jax.ShapeDtypeStruct((n, QKV_A), BF16),
                   jax.ShapeDtypeStruct((n, UV_B), F32),
                   jax.ShapeDtypeStruct((n, QKVG_C), F32)],
        compiler_params=_params("parallel"),
        name="inproj",
    )(x, g, w, perm)


MAX_DIL = DIL_PATTERNS[-1][1]
SUB = 8


def _group_permutation():
    i = jnp.arange(BLK)
    src = MAX_DIL * (i % SUB) + i // SUB
    return (src[:, None] == jnp.arange(BLK)[None, :]).astype(BF16)


def _run_offsets(d):
    return [BLK * g + SUB * d * c for g in range(d) for c in range(MAX_DIL // d)]


def _block_load(ref, p, base, d):
    if d == 1:
        return ref[p, pl.ds(pl.multiple_of(base, BLK), BLK), :]
    return jnp.concatenate(
        [ref[p, pl.ds(pl.multiple_of(base + off, SUB), SUB), :] for off in _run_offsets(d)], axis=0)


def _block_store(ref, p, base, d, val):
    if d == 1:
        ref[p, pl.ds(pl.multiple_of(base, BLK), BLK), :] = val
        return
    for i, off in enumerate(_run_offsets(d)):
        ref[p, pl.ds(pl.multiple_of(base + off, SUB), SUB), :] = val[i * SUB:(i + 1) * SUB]


def _dil_block(bias_ref, qp, kring, vring, acc_o, acc_m, acc_l, branch, d, qbase, kcbase, kpbase,
               has_prev, first):
    bias_prev = jnp.where(has_prev, bias_ref[branch, :, :BLK], -jnp.inf)
    bias_cur = bias_ref[branch, :, BLK:]
    lo = _lane_lo((BLK, LANES))
    for p in range(N_PAIRS):
        q = _block_load(qp, p, qbase, d).astype(BF16)
        k = jnp.concatenate([_block_load(kring, p, kpbase, d), _block_load(kring, p, kcbase, d)],
                            axis=0).astype(BF16)
        v = jnp.concatenate([_block_load(vring, p, kpbase, d), _block_load(vring, p, kcbase, d)],
                            axis=0).astype(BF16)
        o_h, m_h, l_h = [], [], []
        for e in range(2):
            qm = jnp.where(lo if e == 0 else ~lo, q, jnp.zeros_like(q))
            s = _dot_nt(qm, k)
            sp = s[:, :BLK] + bias_prev
            sc = s[:, BLK:] + bias_cur
            m = jnp.max(jnp.maximum(sp, sc), axis=-1, keepdims=True)
            pp = jnp.exp(sp - m)
            pc = jnp.exp(sc - m)
            l_h.append(jnp.sum(pp + pc, axis=-1, keepdims=True))
            m_h.append(m)
            o_h.append(_dot(jnp.concatenate([pp, pc], axis=1).astype(BF16), v))
        o_b = jnp.where(lo, o_h[0], o_h[1])
        m_b = jnp.where(lo, m_h[0], m_h[1])
        l_b = jnp.where(lo, l_h[0], l_h[1])
        if first:
            _block_store(acc_o, p, qbase, d, o_b)
            _block_store(acc_m, p, qbase, d, m_b)
            _block_store(acc_l, p, qbase, d, l_b)
        else:
            m_old = _block_load(acc_m, p, qbase, d)
            m_new = jnp.maximum(m_old, m_b)
            a_old = jnp.exp(m_old - m_new)
            a_b = jnp.exp(m_b - m_new)
            _block_store(acc_o, p, qbase, d, _block_load(acc_o, p, qbase, d) * a_old + o_b * a_b)
            _block_store(acc_l, p, qbase, d, _block_load(acc_l, p, qbase, d) * a_old + l_b * a_b)
            _block_store(acc_m, p, qbase, d, m_new)


def _dil_kernel(q_ref, k_ref, v_ref, bias_ref, o_ref, qp, kring, vring, acc_o, acc_m, acc_l):
    t = pl.program_id(1)
    cur = pl.multiple_of((t % 2) * SUPER, SUPER)
    other = pl.multiple_of(SUPER - cur, SUPER)

    @pl.when(t == 0)
    def _():
        kring[:, SUPER:, :] = jnp.zeros((N_PAIRS, SUPER, LANES), F32)
        vring[:, SUPER:, :] = jnp.zeros((N_PAIRS, SUPER, LANES), F32)

    for p in range(N_PAIRS):
        cols = slice(p * LANES, (p + 1) * LANES)
        qp[p] = q_ref[:, cols].astype(F32) * (HEAD_DIM ** -0.5)
        kring[p, pl.ds(cur, SUPER), :] = k_ref[:, cols].astype(F32)
        vring[p, pl.ds(cur, SUPER), :] = v_ref[:, cols].astype(F32)
    args = (bias_ref, qp, kring, vring, acc_o, acc_m, acc_l)

    for branch, (window, d) in enumerate(DIL_PATTERNS):
        nblk = MAX_DIL // d
        span = BLK * d

        def body(j, carry, branch=branch, d=d, nblk=nblk, span=span):
            if nblk == 1:
                qbase = pl.multiple_of(j * SUB, SUB)
                kpbase = other + qbase
                has_prev = t > 0
            else:
                r = j // nblk
                nb = j % nblk
                qbase = pl.multiple_of(nb * span + r * SUB, SUB)
                kpbase = pl.multiple_of(
                    jnp.where(nb == 0, other + (nblk - 1) * span + r * SUB, cur + qbase - span), SUB)
                has_prev = (t > 0) | (nb > 0)
            _dil_block(*args, branch, d, qbase, cur + qbase, kpbase, has_prev, first=(branch == 0))
            return carry

        lax.fori_loop(0, SUPER // BLK, body, 0, unroll=8 if branch == 0 else 4)

    def finish(i, carry):
        rows = pl.ds(pl.multiple_of(i * BLK, BLK), BLK)
        for p in range(N_PAIRS):
            o_ref[rows, p * LANES:(p + 1) * LANES] = (acc_o[p, rows, :] / acc_l[p, rows, :]).astype(BF16)
        return carry

    lax.fori_loop(0, SUPER // BLK, finish, 0)


def _dilated_bias():
    out = []
    i = jnp.arange(BLK)
    for _, d in DIL_PATTERNS:
        n_c = MAX_DIL // d
        pos = (BLK // d) * (i // (SUB * n_c)) + n_c * (i % SUB) + (i // SUB) % n_c
        prev_ok = pos[None, :] >= pos[:, None]
        cur_ok = pos[None, :] <= pos[:, None]
        out.append(jnp.where(jnp.concatenate([prev_ok, cur_ok], axis=1), 0.0, -jnp.inf))
    return jnp.stack(out).astype(F32)


def _dilated_attention(qkv, batch, seq):
    n = batch * seq
    tiles = seq // SUPER
    bias = _dilated_bias()
    spec = lambda which: pl.BlockSpec((SUPER, A_WIDTH), lambda b, t: (b * tiles + t, which))
    tile_f32 = pltpu.VMEM((N_PAIRS, SUPER, LANES), F32)
    ring_f32 = pltpu.VMEM((N_PAIRS, 2 * SUPER, LANES), F32)
    return pl.pallas_call(
        _dil_kernel,
        grid=(batch, tiles),
        in_specs=[spec(0), spec(1), spec(2), pl.BlockSpec(bias.shape, lambda b, t: (0, 0, 0))],
        out_specs=pl.BlockSpec((SUPER, A_WIDTH), lambda b, t: (b * tiles + t, 0)),
        out_shape=jax.ShapeDtypeStruct((n, A_WIDTH), BF16),
        scratch_shapes=[tile_f32, ring_f32, ring_f32, tile_f32, tile_f32, tile_f32],
        compiler_params=_params("parallel", "arbitrary"),
        name="dilated_attention",
    )(qkv, qkv, qkv, bias)


def _rotate(t, cos, sin_signed, lo_half):
    partner = jnp.where(lo_half, pltpu.roll(t, LANES - HEAD_DIM // 2, 1),
                        pltpu.roll(t, HEAD_DIM // 2, 1))
    return t * cos + partner * sin_signed


def _mix_kernel(b_ref, c_ref, cos_ref, sin_ref, sw_ref, sb_ref, lg_ref, lb_ref,
                dec_ref, qdec_ref, kdec_ref, cdm_ref, yb_ref, yc_ref, state_ref):
    @pl.when(pl.program_id(1) == 0)
    def _():
        state_ref[...] = jnp.zeros_like(state_ref)

    lo = _lane_lo((BLK, LANES))
    row = lax.broadcasted_iota(jnp.int32, (BLK, BLK), 0)
    col = lax.broadcasted_iota(jnp.int32, (BLK, BLK), 1)
    tril = row >= col
    same_head = (row >= HEAD_DIM) == (col >= HEAD_DIM)
    lo_half = (lax.broadcasted_iota(jnp.int32, (BLK, LANES), 1) & (HEAD_DIM // 2)) == 0
    w_tril = [jnp.where(tril, sw_ref[g], 0.0).astype(BF16) for g in range(SGU_GROUPS)]
    states = [state_ref[p] for p in range(N_PAIRS)]
    inv_n = 1.0 / HEAD_DIM

    for c in range(MIX_CHUNKS):
        rows = slice(c * BLK, (c + 1) * BLK)
        u = jax.nn.gelu(b_ref[rows, :B_WIDTH])
        v = jax.nn.gelu(b_ref[rows, B_WIDTH:])
        mu = jnp.mean(v, axis=-1, keepdims=True)
        var = jnp.mean(jnp.square(v - mu), axis=-1, keepdims=True)
        v = (v - mu) * lax.rsqrt(var + EPS) * lg_ref[...] + lb_ref[...]
        for gp in range(B_WIDTH // LANES):
            cols = slice(gp * LANES, (gp + 1) * LANES)
            vp = v[:, cols].astype(BF16)
            s = [_dot(w_tril[2 * gp + e], vp) for e in range(2)]
            gate = jnp.where(lo, s[0], s[1]) + sb_ref[:, cols]
            yb_ref[rows, cols] = (u[:, cols] * gate).astype(BF16)

        cos = cos_ref[rows, :]
        sin = sin_ref[rows, :]
        for p in range(N_PAIRS):
            cols = slice(p * LANES, (p + 1) * LANES)
            q = _rotate(c_ref[rows, p * LANES:(p + 1) * LANES], cos, sin, lo_half)
            k = _rotate(c_ref[rows, C_WIDTH + p * LANES:C_WIDTH + (p + 1) * LANES], cos, sin, lo_half)
            k = k * (HEAD_DIM ** -0.5)
            vb = c_ref[rows, 2 * C_WIDTH + p * LANES:2 * C_WIDTH + (p + 1) * LANES].astype(BF16)
            gate = c_ref[rows, 3 * C_WIDTH + p * LANES:3 * C_WIDTH + (p + 1) * LANES]
            qb = q.astype(BF16)
            kb = k.astype(BF16)
            ys = []
            for e in range(2):
                qm = jnp.where(lo if e == 0 else ~lo, qb, jnp.zeros_like(qb))
                inner = _dot_nt(qm, kb) * dec_ref[2 * p + e]
                ys.append(_dot(inner.astype(BF16), vb))
            y_in = jnp.where(lo, ys[0], ys[1])
            y_x = _dot((q * qdec_ref[:, cols]).astype(BF16), states[p].astype(BF16))
            kv = _dot_tn((k * kdec_ref[:, cols]).astype(BF16), vb)
            states[p] = states[p] * cdm_ref[p] + jnp.where(same_head, kv, 0.0)
            y = y_in + y_x
            mu_lo = jnp.sum(jnp.where(lo, y, 0.0), axis=-1, keepdims=True) * inv_n
            mu_hi = jnp.sum(jnp.where(lo, 0.0, y), axis=-1, keepdims=True) * inv_n
            yc_ = y - jnp.where(lo, mu_lo, mu_hi)
            sq = yc_ * yc_
            var_lo = jnp.sum(jnp.where(lo, sq, 0.0), axis=-1, keepdims=True) * inv_n
            var_hi = jnp.sum(jnp.where(lo, 0.0, sq), axis=-1, keepdims=True) * inv_n
            yn = yc_ * lax.rsqrt(jnp.where(lo, var_lo, var_hi) + EPS)
            yc_ref[rows, cols] = (jax.nn.silu(gate) * yn).astype(BF16)

    for p in range(N_PAIRS):
        state_ref[p] = states[p]


def _retention_tables():
    heads = jnp.arange(RET_HEADS, dtype=F32)
    log_g = jnp.log1p(-jnp.power(2.0, -5.0 - heads))
    idx = jnp.arange(BLK, dtype=F32)
    rel = idx[:, None] - idx[None, :]
    decay = jnp.where(rel[None] >= 0,
                      jnp.exp(jnp.maximum(rel, 0.0)[None] * log_g[:, None, None]), 0.0)
    k_dec = jnp.exp((BLK - 1 - idx)[:, None] * log_g[None, :])
    q_dec = jnp.exp((idx + 1.0)[:, None] * log_g[None, :])
    k_dec = jnp.repeat(k_dec, HEAD_DIM, axis=1)
    q_dec = jnp.repeat(q_dec, HEAD_DIM, axis=1)
    chunk_decay = jnp.exp(BLK * log_g)
    per_row = jnp.repeat(chunk_decay, HEAD_DIM).reshape(N_PAIRS, LANES)
    half = jnp.arange(LANES) >= HEAD_DIM
    same = half[:, None] == half[None, :]
    cdm = jnp.where(same[None], per_row[:, :, None], 0.0)
    return decay, q_dec, k_dec, cdm


def _mixers_bc(zb, zc, cos, sin, sgu_w, sgu_b, ln_g, ln_b, batch, seq):
    n = batch * seq
    tm = MIX_CHUNKS * BLK
    nc = seq // tm
    decay, q_dec, k_dec, cdm = _retention_tables()
    bias = jnp.repeat(sgu_b.T, HEAD_DIM, axis=1)
    row = lambda b, c: (b * nc + c, 0)
    const2 = lambda b, c: (0, 0)
    const3 = lambda b, c: (0, 0, 0)
    return pl.pallas_call(
        _mix_kernel,
        grid=(batch, nc),
        in_specs=[pl.BlockSpec((tm, UV_B), row),
                  pl.BlockSpec((tm, QKVG_C), row),
                  pl.BlockSpec((tm, LANES), row),
                  pl.BlockSpec((tm, LANES), row),
                  pl.BlockSpec(sgu_w.shape, const3),
                  pl.BlockSpec(bias.shape, const2),
                  pl.BlockSpec((1, B_WIDTH), const2),
                  pl.BlockSpec((1, B_WIDTH), const2),
                  pl.BlockSpec(decay.shape, const3),
                  pl.BlockSpec(q_dec.shape, const2),
                  pl.BlockSpec(k_dec.shape, const2),
                  pl.BlockSpec(cdm.shape, const3)],
        out_specs=[pl.BlockSpec((tm, B_WIDTH), row),
                   pl.BlockSpec((tm, C_WIDTH), row)],
        out_shape=[jax.ShapeDtypeStruct((n, B_WIDTH), BF16),
                   jax.ShapeDtypeStruct((n, C_WIDTH), BF16)],
        scratch_shapes=[pltpu.VMEM((N_PAIRS, LANES, LANES), F32)],
        compiler_params=_params("parallel", "arbitrary"),
        name="sgu_retention",
    )(zb, zc, cos, sin, sgu_w, bias, ln_g.reshape(1, B_WIDTH), ln_b.reshape(1, B_WIDTH),
      decay, q_dec, k_dec, cdm)


def _outproj_kernel(ya_ref, yb_ref, yc_ref, x_ref, g_ref, w_ref, unperm_ref, out_ref):
    tm = ya_ref.shape[0]
    ya = jnp.concatenate([_dot(unperm_ref[...], ya_ref[g * BLK:(g + 1) * BLK, :]).astype(BF16)
                          for g in range(tm // BLK)], axis=0)
    acc = _dot(ya, w_ref[:A_WIDTH, :])
    acc += _dot(yb_ref[...], w_ref[A_WIDTH:A_WIDTH + B_WIDTH, :])
    acc += _dot(yc_ref[...], w_ref[A_WIDTH + B_WIDTH:, :])
    out_ref[...] = x_ref[...] + _rms(acc, g_ref[1:2, :])


def _outproj(ya, yb, yc, x, g, w, unperm):
    n = x.shape[0]
    tm = ROW_TILE
    row = lambda i: (i, 0)
    const = lambda i: (0, 0)
    return pl.pallas_call(
        _outproj_kernel,
        grid=(n // tm,),
        in_specs=[pl.BlockSpec((tm, A_WIDTH), row),
                  pl.BlockSpec((tm, B_WIDTH), row),
                  pl.BlockSpec((tm, C_WIDTH), row),
                  pl.BlockSpec((tm, D_MODEL), row),
                  pl.BlockSpec(g.shape, const),
                  pl.BlockSpec(w.shape, const),
                  pl.BlockSpec(unperm.shape, const)],
        out_specs=pl.BlockSpec((tm, D_MODEL), row),
        out_shape=jax.ShapeDtypeStruct((n, D_MODEL), F32),
        compiler_params=_params("parallel"),
        name="outproj",
    )(ya, yb, yc, x, g, w, unperm)


def _kv_kernel(m_ref, g_ref, w_ref, kv_ref):
    h = _rms(m_ref[...], g_ref[6:7, :]).astype(BF16)
    kv_ref[...] = _dot(h, w_ref[...]).astype(BF16)


def _memory_kv(mem, g, w):
    n = mem.shape[0]
    return pl.pallas_call(
        _kv_kernel,
        grid=(n // MEM_LEN,),
        in_specs=[pl.BlockSpec((MEM_LEN, D_MODEL), lambda i: (i, 0)),
                  pl.BlockSpec(g.shape, lambda i: (0, 0)),
                  pl.BlockSpec(w.shape, lambda i: (0, 0))],
        out_specs=pl.BlockSpec((MEM_LEN, 2 * D_MODEL), lambda i: (i, 0)),
        out_shape=jax.ShapeDtypeStruct((n, 2 * D_MODEL), BF16),
        compiler_params=_params("parallel"),
        name="memory_kv",
    )(mem, g, w)


def _cross_kernel(x_ref, kv_ref, g_ref, wq_ref, wo_ref, out_ref):
    x = x_ref[...]
    h = _rms(x, g_ref[2:3, :]).astype(BF16)
    q = _dot(h, wq_ref[...]).astype(BF16)
    scale = X_HEAD_DIM ** -0.5
    acc = jnp.zeros(x.shape, F32)
    for hd in range(X_HEADS):
        cols = slice(hd * X_HEAD_DIM, (hd + 1) * X_HEAD_DIM)
        k = kv_ref[:, hd * X_HEAD_DIM:(hd + 1) * X_HEAD_DIM]
        v = kv_ref[:, D_MODEL + hd * X_HEAD_DIM:D_MODEL + (hd + 1) * X_HEAD_DIM]
        s = _dot_nt(q[:, cols], k) * scale
        m = jnp.max(s, axis=-1, keepdims=True)
        p = jnp.exp(s - m)
        p = p / jnp.sum(p, axis=-1, keepdims=True)
        o = _dot(p.astype(BF16), v)
        acc += _dot(o.astype(BF16), wo_ref[cols, :])
    out_ref[...] = x + _rms(acc, g_ref[3:4, :])


def _cross(x, kv, g, wq, wo, batch, seq):
    n = x.shape[0]
    tm = ROW_TILE
    per_batch = seq // tm
    row = lambda i: (i, 0)
    const = lambda i: (0, 0)
    return pl.pallas_call(
        _cross_kernel,
        grid=(n // tm,),
        in_specs=[pl.BlockSpec((tm, D_MODEL), row),
                  pl.BlockSpec((MEM_LEN, 2 * D_MODEL), lambda i: (i // per_batch, 0)),
                  pl.BlockSpec(g.shape, const),
                  pl.BlockSpec(wq.shape, const),
                  pl.BlockSpec(wo.shape, const)],
        out_specs=pl.BlockSpec((tm, D_MODEL), row),
        out_shape=jax.ShapeDtypeStruct((n, D_MODEL), F32),
        compiler_params=_params("parallel"),
        name="cross_attention",
    )(x, kv, g, wq, wo)


def _mlp_kernel(x_ref, g_ref, wu_ref, wd_ref, out_ref):
    x = x_ref[...]
    h = _rms(x, g_ref[4:5, :]).astype(BF16)
    acc = jnp.zeros(x.shape, F32)
    for c in range(D_FF // FF_CHUNK):
        cols = slice(c * FF_CHUNK, (c + 1) * FF_CHUNK)
        f = jnp.square(jnp.maximum(_dot(h, wu_ref[:, cols]), 0.0))
        acc += _dot(f.astype(BF16), wd_ref[cols, :])
    out_ref[...] = x + _rms(acc, g_ref[5:6, :])


def _mlp(x, g, w_up, w_down):
    n = x.shape[0]
    tm = ROW_TILE
    row = lambda i: (i, 0)
    const = lambda i: (0, 0)
    return pl.pallas_call(
        _mlp_kernel,
        grid=(n // tm,),
        in_specs=[pl.BlockSpec((tm, D_MODEL), row),
                  pl.BlockSpec(g.shape, const),
                  pl.BlockSpec(w_up.shape, const, pipeline_mode=pl.Buffered(1)),
                  pl.BlockSpec(w_down.shape, const, pipeline_mode=pl.Buffered(1))],
        out_specs=pl.BlockSpec((tm, D_MODEL), row),
        out_shape=jax.ShapeDtypeStruct((n, D_MODEL), F32),
        compiler_params=_params("parallel"),
        name="mlp",
    )(x, g, w_up, w_down)


def kernel(x, mem, positions, norm_g, w_in, sgu_w, sgu_b, sgu_ln_g, sgu_ln_b,
           w_out, x_wq, x_wkv, x_wo, w_up, w_down):
    batch, seq, _ = x.shape
    assert x.shape == (batch, seq, D_MODEL) and seq % (BLK * DIL_PATTERNS[-1][1]) == 0
    assert (batch * seq) % ROW_TILE == 0 and seq % ROW_TILE == 0
    assert mem.shape == (batch, MEM_LEN, D_MODEL) and w_in.shape == (DEPTH, D_MODEL, IN_WIDTH)
    xf = x.reshape(batch * seq, D_MODEL)
    memf = mem.reshape(batch * MEM_LEN, D_MODEL)
    cos, sin = _rope_tables(positions)
    perm = _group_permutation()
    for l in range(DEPTH):
        g = norm_g[l]
        za, zb, zc = _inproj(xf, g, w_in[l].astype(BF16), perm)
        ya = _dilated_attention(za, batch, seq)
        yb, yc = _mixers_bc(zb, zc, cos, sin, sgu_w[l], sgu_b[l], sgu_ln_g[l], sgu_ln_b[l],
                            batch, seq)
        xf = _outproj(ya, yb, yc, xf, g, w_out[l].astype(BF16), perm.T)
        kv = _memory_kv(memf, g, x_wkv[l].astype(BF16))
        xf = _cross(xf, kv, g, x_wq[l].astype(BF16), x_wo[l].astype(BF16), batch, seq)
        xf = _mlp(xf, g, w_up[l].astype(BF16), w_down[l].astype(BF16))
    return xf.reshape(batch, seq, D_MODEL)
```

```python
import functools

import jax
import jax.numpy as jnp
from jax import lax
from jax.experimental import pallas as pl
from jax.experimental.pallas import tpu as pltpu

F32 = jnp.float32
BF16 = jnp.bfloat16

D_MODEL = 1024
DEPTH = 2
MEM_LEN = 256
HEAD_DIM = 64
DIL_HEADS = 6
DIL_PATTERNS = ((128, 1), (512, 4), (2048, 16))
BLK = 128
SGU_GROUPS = 4
RET_HEADS = 6
ROPE_BASE = 10000.0
A_WIDTH = DIL_HEADS * HEAD_DIM
B_WIDTH = SGU_GROUPS * HEAD_DIM
C_WIDTH = RET_HEADS * HEAD_DIM
QKV_A = 3 * A_WIDTH
UV_B = 2 * B_WIDTH
QKVG_C = 4 * C_WIDTH
IN_WIDTH = QKV_A + UV_B + QKVG_C
X_HEADS = 4
X_HEAD_DIM = D_MODEL // X_HEADS
D_FF = 4 * D_MODEL
EPS = 1e-6

LANES = 128
N_PAIRS = A_WIDTH // LANES
VMEM_LIMIT = 56 * 1024 * 1024

SUPER = BLK * DIL_PATTERNS[-1][1]

ROW_TILE = 512
DIL_UNROLL = (4, 4, 4)
DIL_LOOKAHEAD = (12, 3, 3)
MIX_UNROLL = 2
MIX_CHUNKS = 8
FF_CHUNK = 1024


def _params(*sem):
    return pltpu.CompilerParams(dimension_semantics=sem, vmem_limit_bytes=VMEM_LIMIT)


def _rms(x, g):
    ms = jnp.mean(x * x, axis=-1, keepdims=True)
    return x * lax.rsqrt(ms + EPS) * g


def _dot(a, b):
    return jnp.dot(a, b, preferred_element_type=F32)


def _dot_nt(a, b):
    return lax.dot_general(a, b, (((1,), (1,)), ((), ())), preferred_element_type=F32)


def _dot_tn(a, b):
    return lax.dot_general(a, b, (((0,), (0,)), ((), ())), preferred_element_type=F32)


def _lane_lo(shape):
    lane = lax.broadcasted_iota(jnp.int32, shape, len(shape) - 1)
    return (lane & HEAD_DIM) == 0


def _rope_kernel(pos_ref, inv_ref, cos_ref, sin_ref):
    ang = pos_ref[...].astype(F32) * inv_ref[...]
    lane = lax.broadcasted_iota(jnp.int32, ang.shape, 1)
    sign = jnp.where(lane < HEAD_DIM, -1.0, 1.0)
    cos_ref[...] = jnp.cos(ang)
    sin_ref[...] = jnp.sin(ang) * sign


def _rope_tables(positions):
    n = positions.size
    half = HEAD_DIM // 2
    inv = 1.0 / (ROPE_BASE ** jnp.linspace(0.0, 1.0, half, dtype=F32))
    inv = jnp.tile(inv, LANES // half).reshape(1, LANES)
    tm = 1024
    return pl.pallas_call(
        _rope_kernel,
        grid=(n // tm,),
        in_specs=[pl.BlockSpec((tm, 1), lambda i: (i, 0)),
                  pl.BlockSpec((1, LANES), lambda i: (0, 0))],
        out_specs=[pl.BlockSpec((tm, LANES), lambda i: (i, 0))] * 2,
        out_shape=[jax.ShapeDtypeStruct((n, LANES), F32)] * 2,
        compiler_params=_params("parallel"),
        name="rope_tables",
    )(positions.reshape(n, 1), inv)


def _inproj_kernel(x_ref, g_ref, w_ref, perm_ref, a_ref, b_ref, c_ref):
    h = _rms(x_ref[...], g_ref[0:1, :]).astype(BF16)
    hp = jnp.concatenate([_dot(perm_ref[...], h[g * BLK:(g + 1) * BLK]).astype(BF16)
                          for g in range(h.shape[0] // BLK)], axis=0)
    a_ref[...] = _dot(hp, w_ref[:, :QKV_A]).astype(BF16)
    b_ref[...] = _dot(h, w_ref[:, QKV_A:QKV_A + UV_B])
    c_ref[...] = _dot(h, w_ref[:, QKV_A + UV_B:])


def _inproj(x, g, w, perm):
    n = x.shape[0]
    tm = ROW_TILE
    return pl.pallas_call(
        _inproj_kernel,
        grid=(n // tm,),
        in_specs=[pl.BlockSpec((tm, D_MODEL), lambda i: (i, 0)),
                  pl.BlockSpec(g.shape, lambda i: (0, 0)),
                  pl.BlockSpec(w.shape, lambda i: (0, 0)),
                  pl.BlockSpec(perm.shape, lambda i: (0, 0))],
        out_specs=[pl.BlockSpec((tm, QKV_A), lambda i: (i, 0)),
                   pl.BlockSpec((tm, UV_B), lambda i: (i, 0)),
                   pl.BlockSpec((tm, QKVG_C), lambda i: (i, 0))],
        out_shape=[jax.ShapeDtypeStruct((n, QKV_A), BF16),
                   jax.ShapeDtypeStruct((n, UV_B), F32),
                   jax.ShapeDtypeStruct((n, QKVG_C), F32)],
        compiler_params=_params("parallel"),
        name="inproj",
    )(x, g, w, perm)


MAX_DIL = DIL_PATTERNS[-1][1]
SUB = 8


def _group_permutation():
    i = jnp.arange(BLK)
    src = MAX_DIL * (i % SUB) + i // SUB
    return (src[:, None] == jnp.arange(BLK)[None, :]).astype(BF16)


def _run_offsets(d):
    return [BLK * g + SUB * d * c for g in range(d) for c in range(MAX_DIL // d)]


def _block_load(ref, p, base, d):
    if d == 1:
        return ref[p, pl.ds(pl.multiple_of(base, BLK), BLK), :]
    return jnp.concatenate(
        [ref[p, pl.ds(pl.multiple_of(base + off, SUB), SUB), :] for off in _run_offsets(d)], axis=0)


def _block_store(ref, p, base, d, val):
    if d == 1:
        ref[p, pl.ds(pl.multiple_of(base, BLK), BLK), :] = val
        return
    for i, off in enumerate(_run_offsets(d)):
        ref[p, pl.ds(pl.multiple_of(base + off, SUB), SUB), :] = val[i * SUB:(i + 1) * SUB]


def _dil_blocks(refs, branch, d, blocks, first, lookahead):
    bias_ref, qp, kring, vring, acc_o, acc_m, acc_l = refs
    lo = _lane_lo((BLK, LANES))
    units = [(blk, p) for blk in blocks for p in range(N_PAIRS)]

    def scores(unit):
        (qbase, kcbase, kpbase, _), p = unit
        q = _block_load(qp, p, qbase, d).astype(BF16)
        k = jnp.concatenate([_block_load(kring, p, kpbase, d), _block_load(kring, p, kcbase, d)],
                            axis=0).astype(BF16)
        return [_dot_nt(jnp.where(lo if e == 0 else ~lo, q, jnp.zeros_like(q)), k) for e in range(2)]

    def softmax_pv(unit, s_pair):
        (qbase, kcbase, kpbase, has_prev), p = unit
        bias_prev = jnp.where(has_prev, bias_ref[branch, :, :BLK], -jnp.inf)
        bias_cur = bias_ref[branch, :, BLK:]
        v = jnp.concatenate([_block_load(vring, p, kpbase, d), _block_load(vring, p, kcbase, d)],
                            axis=0).astype(BF16)
        o_h, m_h, l_h = [], [], []
        for s in s_pair:
            sp = s[:, :BLK] + bias_prev
            sc = s[:, BLK:] + bias_cur
            m = jnp.max(jnp.maximum(sp, sc), axis=-1, keepdims=True)
            pp = jnp.exp(sp - m)
            pc = jnp.exp(sc - m)
            l_h.append(jnp.sum(pp + pc, axis=-1, keepdims=True))
            m_h.append(m)
            o_h.append(_dot(jnp.concatenate([pp, pc], axis=1).astype(BF16), v))
        return o_h, m_h, l_h

    def merge(unit, stats):
        (qbase, _, _, _), p = unit
        o_h, m_h, l_h = stats
        o_b = jnp.where(lo, o_h[0], o_h[1])
        m_b = jnp.where(lo, m_h[0], m_h[1])
        l_b = jnp.where(lo, l_h[0], l_h[1])
        if first:
            _block_store(acc_o, p, qbase, d, o_b)
            _block_store(acc_m, p, qbase, d, m_b)
            _block_store(acc_l, p, qbase, d, l_b)
        else:
            m_old = _block_load(acc_m, p, qbase, d)
            m_new = jnp.maximum(m_old, m_b)
            a_old = jnp.exp(m_old - m_new)
            a_b = jnp.exp(m_b - m_new)
            _block_store(acc_o, p, qbase, d, _block_load(acc_o, p, qbase, d) * a_old + o_b * a_b)
            _block_store(acc_l, p, qbase, d, _block_load(acc_l, p, qbase, d) * a_old + l_b * a_b)
            _block_store(acc_m, p, qbase, d, m_new)

    ahead = min(lookahead, len(units))
    pending = [scores(u) for u in units[:ahead]]
    stats_prev = None
    for i, unit in enumerate(units):
        if i + ahead < len(units):
            pending.append(scores(units[i + ahead]))
        stats = softmax_pv(unit, pending[i])
        if stats_prev is not None:
            merge(units[i - 1], stats_prev)
        stats_prev = stats
    merge(units[-1], stats_prev)


def _dil_kernel(q_ref, k_ref, v_ref, bias_ref, o_ref, qp, kring, vring, acc_o, acc_m, acc_l):
    t = pl.program_id(1)
    cur = pl.multiple_of((t % 2) * SUPER, SUPER)
    other = pl.multiple_of(SUPER - cur, SUPER)

    @pl.when(t == 0)
    def _():
        kring[:, SUPER:, :] = jnp.zeros((N_PAIRS, SUPER, LANES), F32)
        vring[:, SUPER:, :] = jnp.zeros((N_PAIRS, SUPER, LANES), F32)

    for p in range(N_PAIRS):
        cols = slice(p * LANES, (p + 1) * LANES)
        qp[p] = q_ref[:, cols].astype(F32) * (HEAD_DIM ** -0.5)
        kring[p, pl.ds(cur, SUPER), :] = k_ref[:, cols].astype(F32)
        vring[p, pl.ds(cur, SUPER), :] = v_ref[:, cols].astype(F32)
    refs = (bias_ref, qp, kring, vring, acc_o, acc_m, acc_l)

    for branch, (window, d) in enumerate(DIL_PATTERNS):
        nblk = MAX_DIL // d
        span = BLK * d

        def block_rows(j, d=d, nblk=nblk, span=span):
            if nblk == 1:
                qbase = pl.multiple_of(j * SUB, SUB)
                return qbase, cur + qbase, other + qbase, t > 0
            r = j // nblk
            nb = j % nblk
            qbase = pl.multiple_of(nb * span + r * SUB, SUB)
            kpbase = pl.multiple_of(
                jnp.where(nb == 0, other + (nblk - 1) * span + r * SUB, cur + qbase - span), SUB)
            return qbase, cur + qbase, kpbase, (t > 0) | (nb > 0)

        unroll, lookahead = DIL_UNROLL[branch], DIL_LOOKAHEAD[branch]

        def body(i, carry, branch=branch, d=d, block_rows=block_rows, unroll=unroll, lookahead=lookahead):
            blocks = [block_rows(i * unroll + u) for u in range(unroll)]
            _dil_blocks(refs, branch, d, blocks, first=(branch == 0), lookahead=lookahead)
            return carry

        lax.fori_loop(0, SUPER // BLK // unroll, body, 0)

    def finish(i, carry):
        rows = pl.ds(pl.multiple_of(i * BLK, BLK), BLK)
        for p in range(N_PAIRS):
            o_ref[rows, p * LANES:(p + 1) * LANES] = (acc_o[p, rows, :] / acc_l[p, rows, :]).astype(BF16)
        return carry

    lax.fori_loop(0, SUPER // BLK, finish, 0)


def _dilated_bias():
    out = []
    i = jnp.arange(BLK)
    for _, d in DIL_PATTERNS:
        n_c = MAX_DIL // d
        pos = (BLK // d) * (i // (SUB * n_c)) + n_c * (i % SUB) + (i // SUB) % n_c
        prev_ok = pos[None, :] >= pos[:, None]
        cur_ok = pos[None, :] <= pos[:, None]
        out.append(jnp.where(jnp.concatenate([prev_ok, cur_ok], axis=1), 0.0, -jnp.inf))
    return jnp.stack(out).astype(F32)


def _dilated_attention(qkv, batch, seq):
    n = batch * seq
    tiles = seq // SUPER
    bias = _dilated_bias()
    spec = lambda which: pl.BlockSpec((SUPER, A_WIDTH), lambda b, t: (b * tiles + t, which))
    tile_f32 = pltpu.VMEM((N_PAIRS, SUPER, LANES), F32)
    ring_f32 = pltpu.VMEM((N_PAIRS, 2 * SUPER, LANES), F32)
    return pl.pallas_call(
        _dil_kernel,
        grid=(batch, tiles),
        in_specs=[spec(0), spec(1), spec(2), pl.BlockSpec(bias.shape, lambda b, t: (0, 0, 0))],
        out_specs=pl.BlockSpec((SUPER, A_WIDTH), lambda b, t: (b * tiles + t, 0)),
        out_shape=jax.ShapeDtypeStruct((n, A_WIDTH), BF16),
        scratch_shapes=[tile_f32, ring_f32, ring_f32, tile_f32, tile_f32, tile_f32],
        compiler_params=_params("parallel", "arbitrary"),
        name="dilated_attention",
    )(qkv, qkv, qkv, bias)


def _rotate(t, cos, sin_signed):
    return t * cos + pltpu.roll(t, HEAD_DIM, 1) * sin_signed


def _mix_kernel(b_ref, c_ref, cos_ref, sin_ref, sw_ref, sb_ref, lg_ref, lb_ref,
                dec_ref, qdec_ref, kdec_ref, cdm_ref, yb_ref, yc_ref, state_ref, wt_ref):
    @pl.when(pl.program_id(1) == 0)
    def _():
        state_ref[...] = jnp.zeros_like(state_ref)

    lo = _lane_lo((BLK, LANES))
    row = lax.broadcasted_iota(jnp.int32, (BLK, BLK), 0)
    col = lax.broadcasted_iota(jnp.int32, (BLK, BLK), 1)
    tril = row >= col
    same_head = ((row >> 5) & 1) == (col >> 6)
    lane = lax.broadcasted_iota(jnp.int32, (BLK, LANES), 1)
    qk_head0 = (lane & (HEAD_DIM // 2)) == 0
    for g in range(SGU_GROUPS):
        wt_ref[g] = jnp.where(tril, sw_ref[g], 0.0).astype(BF16)
    inv_n = 1.0 / HEAD_DIM

    def sgu(rows):
        u = jax.nn.gelu(b_ref[rows, :B_WIDTH])
        v = jax.nn.gelu(b_ref[rows, B_WIDTH:])
        mu = jnp.mean(v, axis=-1, keepdims=True)
        var = jnp.mean(jnp.square(v - mu), axis=-1, keepdims=True)
        v = (v - mu) * lax.rsqrt(var + EPS) * lg_ref[...] + lb_ref[...]
        for gp in range(B_WIDTH // LANES):
            cols = slice(gp * LANES, (gp + 1) * LANES)
            vp = v[:, cols].astype(BF16)
            s = [_dot(wt_ref[2 * gp + e], vp) for e in range(2)]
            gate = jnp.where(lo, s[0], s[1]) + sb_ref[:, cols]
            yb_ref[rows, cols] = (u[:, cols] * gate).astype(BF16)

    def retention(row_list):
        work = [(rows, p) for rows in row_list for p in range(N_PAIRS)]
        stage1 = []
        for rows, p in work:
            cols = slice(p * LANES, (p + 1) * LANES)
            cos, sin = cos_ref[rows, :], sin_ref[rows, :]
            q = _rotate(c_ref[rows, p * LANES:(p + 1) * LANES], cos, sin)
            k = _rotate(c_ref[rows, C_WIDTH + p * LANES:C_WIDTH + (p + 1) * LANES], cos, sin)
            vb = c_ref[rows, 2 * C_WIDTH + p * LANES:2 * C_WIDTH + (p + 1) * LANES].astype(BF16)
            qb, kb = q.astype(BF16), k.astype(BF16)
            scores = [_dot_nt(jnp.where(qk_head0 if e == 0 else ~qk_head0, qb, jnp.zeros_like(qb)), kb)
                      for e in range(2)]
            kv = _dot_tn((k * kdec_ref[:, cols]).astype(BF16), vb)
            stage1.append((scores, kv, (q * qdec_ref[:, cols]).astype(BF16), vb))
        stage2 = []
        for (rows, p), (scores, kv, qd, vb) in zip(work, stage1):
            state = state_ref[p]
            y_x = _dot(qd, state.astype(BF16))
            state_ref[p] = state * cdm_ref[p] + jnp.where(same_head, kv, 0.0)
            ys = [_dot((scores[e] * dec_ref[2 * p + e]).astype(BF16), vb)
                  for e in range(2)]
            stage2.append((ys, y_x))
        for (rows, p), (ys, y_x) in zip(work, stage2):
            cols = slice(p * LANES, (p + 1) * LANES)
            y = jnp.where(lo, ys[0], ys[1]) + y_x
            mu_lo = jnp.sum(jnp.where(lo, y, 0.0), axis=-1, keepdims=True) * inv_n
            mu_hi = jnp.sum(jnp.where(lo, 0.0, y), axis=-1, keepdims=True) * inv_n
            yc_ = y - jnp.where(lo, mu_lo, mu_hi)
            sq = yc_ * yc_
            var_lo = jnp.sum(jnp.where(lo, sq, 0.0), axis=-1, keepdims=True) * inv_n
            var_hi = jnp.sum(jnp.where(lo, 0.0, sq), axis=-1, keepdims=True) * inv_n
            yn = yc_ * lax.rsqrt(jnp.where(lo, var_lo, var_hi) + EPS)
            gate = c_ref[rows, 3 * C_WIDTH + p * LANES:3 * C_WIDTH + (p + 1) * LANES]
            yc_ref[rows, cols] = (jax.nn.silu(gate) * yn).astype(BF16)

    def group(i, carry):
        row_list = [pl.ds(pl.multiple_of((i * MIX_UNROLL + cc) * BLK, BLK), BLK)
                    for cc in range(MIX_UNROLL)]
        for rows in row_list:
            sgu(rows)
        retention(row_list)
        return carry

    lax.fori_loop(0, MIX_CHUNKS // MIX_UNROLL, group, 0)


def _qk_lane_head():
    lane = jnp.arange(C_WIDTH)
    return 2 * (lane // LANES) + ((lane >> 5) & 1)


def _qk_reorder(w):
    lead = w.shape[:-1]
    w = w.reshape(lead + (N_PAIRS, 2, 2, HEAD_DIM // 2))
    return jnp.swapaxes(w, -3, -2).reshape(lead + (C_WIDTH,))


def _inproj_weight(w):
    q0 = QKV_A + UV_B
    w = w.astype(BF16)
    return jnp.concatenate([w[:, :q0], _qk_reorder(w[:, q0:q0 + C_WIDTH]),
                            _qk_reorder(w[:, q0 + C_WIDTH:q0 + 2 * C_WIDTH]), w[:, q0 + 2 * C_WIDTH:]],
                           axis=1)


def _retention_tables():
    heads = jnp.arange(RET_HEADS, dtype=F32)
    log_g = jnp.log1p(-jnp.power(2.0, -5.0 - heads))
    idx = jnp.arange(BLK, dtype=F32)
    rel = idx[:, None] - idx[None, :]
    scale = HEAD_DIM ** -0.5
    decay = jnp.where(rel[None] >= 0,
                      jnp.exp(jnp.maximum(rel, 0.0)[None] * log_g[:, None, None]), 0.0) * scale
    k_dec = jnp.exp((BLK - 1 - idx)[:, None] * log_g[None, :]) * scale
    q_dec = jnp.exp((idx + 1.0)[:, None] * log_g[None, :])
    lane_head = _qk_lane_head()
    k_dec = k_dec[:, lane_head]
    q_dec = q_dec[:, lane_head]
    chunk_decay = jnp.exp(BLK * log_g)
    per_row = chunk_decay[lane_head].reshape(N_PAIRS, LANES)
    row_head = (jnp.arange(LANES) >> 5) & 1
    col_head = jnp.arange(LANES) >> 6
    same = row_head[:, None] == col_head[None, :]
    cdm = jnp.where(same[None], per_row[:, :, None], 0.0)
    return decay, q_dec, k_dec, cdm


def _mixers_bc(zb, zc, cos, sin, sgu_w, sgu_b, ln_g, ln_b, batch, seq):
    n = batch * seq
    tm = MIX_CHUNKS * BLK
    nc = seq // tm
    decay, q_dec, k_dec, cdm = _retention_tables()
    bias = jnp.repeat(sgu_b.T, HEAD_DIM, axis=1)
    row = lambda b, c: (b * nc + c, 0)
    const2 = lambda b, c: (0, 0)
    const3 = lambda b, c: (0, 0, 0)
    return pl.pallas_call(
        _mix_kernel,
        grid=(batch, nc),
        in_specs=[pl.BlockSpec((tm, UV_B), row),
                  pl.BlockSpec((tm, QKVG_C), row),
                  pl.BlockSpec((tm, LANES), row),
                  pl.BlockSpec((tm, LANES), row),
                  pl.BlockSpec(sgu_w.shape, const3),
                  pl.BlockSpec(bias.shape, const2),
                  pl.BlockSpec((1, B_WIDTH), const2),
                  pl.BlockSpec((1, B_WIDTH), const2),
                  pl.BlockSpec(decay.shape, const3),
                  pl.BlockSpec(q_dec.shape, const2),
                  pl.BlockSpec(k_dec.shape, const2),
                  pl.BlockSpec(cdm.shape, const3)],
        out_specs=[pl.BlockSpec((tm, B_WIDTH), row),
                   pl.BlockSpec((tm, C_WIDTH), row)],
        out_shape=[jax.ShapeDtypeStruct((n, B_WIDTH), BF16),
                   jax.ShapeDtypeStruct((n, C_WIDTH), BF16)],
        scratch_shapes=[pltpu.VMEM((N_PAIRS, LANES, LANES), F32),
                        pltpu.VMEM((SGU_GROUPS, BLK, BLK), BF16)],
        compiler_params=_params("parallel", "arbitrary"),
        name="sgu_retention",
    )(zb, zc, cos, sin, sgu_w, bias, ln_g.reshape(1, B_WIDTH), ln_b.reshape(1, B_WIDTH),
      decay, q_dec, k_dec, cdm)


def _outproj_kernel(ya_ref, yb_ref, yc_ref, x_ref, g_ref, w_ref, unperm_ref, out_ref):
    tm = ya_ref.shape[0]
    ya = jnp.concatenate([_dot(unperm_ref[...], ya_ref[g * BLK:(g + 1) * BLK, :]).astype(BF16)
                          for g in range(tm // BLK)], axis=0)
    mix = jnp.concatenate([ya, yb_ref[...], yc_ref[...]], axis=1)
    out_ref[...] = x_ref[...] + _rms(_dot(mix, w_ref[...]), g_ref[1:2, :])


def _outproj(ya, yb, yc, x, g, w, unperm):
    n = x.shape[0]
    tm = ROW_TILE
    row = lambda i: (i, 0)
    const = lambda i: (0, 0)
    return pl.pallas_call(
        _outproj_kernel,
        grid=(n // tm,),
        in_specs=[pl.BlockSpec((tm, A_WIDTH), row),
                  pl.BlockSpec((tm, B_WIDTH), row),
                  pl.BlockSpec((tm, C_WIDTH), row),
                  pl.BlockSpec((tm, D_MODEL), row),
                  pl.BlockSpec(g.shape, const),
                  pl.BlockSpec(w.shape, const),
                  pl.BlockSpec(unperm.shape, const)],
        out_specs=pl.BlockSpec((tm, D_MODEL), row),
        out_shape=jax.ShapeDtypeStruct((n, D_MODEL), F32),
        compiler_params=_params("parallel"),
        name="outproj",
    )(ya, yb, yc, x, g, w, unperm)


def _kv_kernel(m_ref, g_ref, w_ref, kv_ref):
    h = _rms(m_ref[...], g_ref[6:7, :]).astype(BF16)
    kv_ref[...] = _dot(h, w_ref[...]).astype(BF16)


def _memory_kv(mem, g, w):
    n = mem.shape[0]
    return pl.pallas_call(
        _kv_kernel,
        grid=(n // MEM_LEN,),
        in_specs=[pl.BlockSpec((MEM_LEN, D_MODEL), lambda i: (i, 0)),
                  pl.BlockSpec(g.shape, lambda i: (0, 0)),
                  pl.BlockSpec(w.shape, lambda i: (0, 0))],
        out_specs=pl.BlockSpec((MEM_LEN, 2 * D_MODEL), lambda i: (i, 0)),
        out_shape=jax.ShapeDtypeStruct((n, 2 * D_MODEL), BF16),
        compiler_params=_params("parallel"),
        name="memory_kv",
    )(mem, g, w)


def _cross_kernel(x_ref, kv_ref, g_ref, wq_ref, wo_ref, out_ref):
    x = x_ref[...]
    h = _rms(x, g_ref[2:3, :]).astype(BF16)
    q = _dot(h, wq_ref[...]).astype(BF16)
    scale = X_HEAD_DIM ** -0.5
    heads = [slice(hd * X_HEAD_DIM, (hd + 1) * X_HEAD_DIM) for hd in range(X_HEADS)]
    scores = [_dot_nt(q[:, cols], kv_ref[:, cols]) for cols in heads]
    outs = []
    for hd, s in enumerate(scores):
        s = s * scale
        m = jnp.max(s, axis=-1, keepdims=True)
        p = jnp.exp(s - m)
        p = p / jnp.sum(p, axis=-1, keepdims=True)
        v = kv_ref[:, D_MODEL + hd * X_HEAD_DIM:D_MODEL + (hd + 1) * X_HEAD_DIM]
        outs.append(_dot(p.astype(BF16), v).astype(BF16))
    acc = _dot(jnp.concatenate(outs, axis=1), wo_ref[...])
    out_ref[...] = x + _rms(acc, g_ref[3:4, :])


def _cross(x, kv, g, wq, wo, batch, seq):
    n = x.shape[0]
    tm = ROW_TILE
    per_batch = seq // tm
    row = lambda i: (i, 0)
    const = lambda i: (0, 0)
    return pl.pallas_call(
        _cross_kernel,
        grid=(n // tm,),
        in_specs=[pl.BlockSpec((tm, D_MODEL), row),
                  pl.BlockSpec((MEM_LEN, 2 * D_MODEL), lambda i: (i // per_batch, 0)),
                  pl.BlockSpec(g.shape, const),
                  pl.BlockSpec(wq.shape, const),
                  pl.BlockSpec(wo.shape, const)],
        out_specs=pl.BlockSpec((tm, D_MODEL), row),
        out_shape=jax.ShapeDtypeStruct((n, D_MODEL), F32),
        compiler_params=_params("parallel"),
        name="cross_attention",
    )(x, kv, g, wq, wo)


def _mlp_kernel(x_ref, g_ref, wu_ref, wd_ref, out_ref):
    x = x_ref[...]
    h = _rms(x, g_ref[4:5, :]).astype(BF16)
    acc = jnp.zeros(x.shape, F32)
    for c in range(D_FF // FF_CHUNK):
        cols = slice(c * FF_CHUNK, (c + 1) * FF_CHUNK)
        f = jnp.square(jnp.maximum(_dot(h, wu_ref[:, cols]), 0.0))
        acc += _dot(f.astype(BF16), wd_ref[cols, :])
    out_ref[...] = x + _rms(acc, g_ref[5:6, :])


def _mlp(x, g, w_up, w_down):
    n = x.shape[0]
    tm = ROW_TILE
    row = lambda i: (i, 0)
    const = lambda i: (0, 0)
    return pl.pallas_call(
        _mlp_kernel,
        grid=(n // tm,),
        in_specs=[pl.BlockSpec((tm, D_MODEL), row),
                  pl.BlockSpec(g.shape, const),
                  pl.BlockSpec(w_up.shape, const, pipeline_mode=pl.Buffered(1)),
                  pl.BlockSpec(w_down.shape, const, pipeline_mode=pl.Buffered(1))],
        out_specs=pl.BlockSpec((tm, D_MODEL), row),
        out_shape=jax.ShapeDtypeStruct((n, D_MODEL), F32),
        compiler_params=_params("parallel"),
        name="mlp",
    )(x, g, w_up, w_down)


def kernel(x, mem, positions, norm_g, w_in, sgu_w, sgu_b, sgu_ln_g, sgu_ln_b,
           w_out, x_wq, x_wkv, x_wo, w_up, w_down):
    batch, seq, _ = x.shape
    assert x.shape == (batch, seq, D_MODEL) and seq % (BLK * DIL_PATTERNS[-1][1]) == 0
    assert (batch * seq) % ROW_TILE == 0 and seq % ROW_TILE == 0
    assert mem.shape == (batch, MEM_LEN, D_MODEL) and w_in.shape == (DEPTH, D_MODEL, IN_WIDTH)
    xf = x.reshape(batch * seq, D_MODEL)
    memf = mem.reshape(batch * MEM_LEN, D_MODEL)
    cos, sin = _rope_tables(positions)
    perm = _group_permutation()
    for l in range(DEPTH):
        g = norm_g[l]
        za, zb, zc = _inproj(xf, g, _inproj_weight(w_in[l]), perm)
        ya = _dilated_attention(za, batch, seq)
        yb, yc = _mixers_bc(zb, zc, cos, sin, sgu_w[l], sgu_b[l], sgu_ln_g[l], sgu_ln_b[l],
                            batch, seq)
        xf = _outproj(ya, yb, yc, xf, g, w_out[l].astype(BF16), perm.T)
        kv = _memory_kv(memf, g, x_wkv[l].astype(BF16))
        xf = _cross(xf, kv, g, x_wq[l].astype(BF16), x_wo[l].astype(BF16), batch, seq)
        xf = _mlp(xf, g, w_up[l].astype(BF16), w_down[l].astype(BF16))
    return xf.reshape(batch, seq, D_MODEL)
```

```python
import functools

import jax
import jax.numpy as jnp
from jax import lax
from jax.experimental import pallas as pl
from jax.experimental.pallas import tpu as pltpu

F32 = jnp.float32
BF16 = jnp.bfloat16

D_MODEL = 1024
DEPTH = 2
MEM_LEN = 256
HEAD_DIM = 64
DIL_HEADS = 6
DIL_PATTERNS = ((128, 1), (512, 4), (2048, 16))
BLK = 128
SGU_GROUPS = 4
RET_HEADS = 6
ROPE_BASE = 10000.0
A_WIDTH = DIL_HEADS * HEAD_DIM
B_WIDTH = SGU_GROUPS * HEAD_DIM
C_WIDTH = RET_HEADS * HEAD_DIM
QKV_A = 3 * A_WIDTH
UV_B = 2 * B_WIDTH
QKVG_C = 4 * C_WIDTH
IN_WIDTH = QKV_A + UV_B + QKVG_C
X_HEADS = 4
X_HEAD_DIM = D_MODEL // X_HEADS
D_FF = 4 * D_MODEL
EPS = 1e-6

LANES = 128
N_PAIRS = A_WIDTH // LANES
VMEM_LIMIT = 56 * 1024 * 1024

SUPER = BLK * DIL_PATTERNS[-1][1]

ROW_TILE = 512
DIL_UNROLL = (4, 4, 4)
DIL_LOOKAHEAD = (12, 3, 3)
MIX_UNROLL = 2
MIX_CHUNKS = 8
FF_CHUNK = 1024


def _params(*sem):
    return pltpu.CompilerParams(dimension_semantics=sem, vmem_limit_bytes=VMEM_LIMIT)


def _layer(arr, l, **kw):
    zeros = (0,) * (arr.ndim - 1)
    return pl.BlockSpec((None,) + arr.shape[1:], lambda *_: (l,) + zeros, **kw)


def _rms(x, g):
    ms = jnp.mean(x * x, axis=-1, keepdims=True)
    return x * lax.rsqrt(ms + EPS) * g


def _dot(a, b):
    return jnp.dot(a, b, preferred_element_type=F32)


def _dot_nt(a, b):
    return lax.dot_general(a, b, (((1,), (1,)), ((), ())), preferred_element_type=F32)


def _dot_tn(a, b):
    return lax.dot_general(a, b, (((0,), (0,)), ((), ())), preferred_element_type=F32)


def _lane_lo(shape):
    lane = lax.broadcasted_iota(jnp.int32, shape, len(shape) - 1)
    return (lane & HEAD_DIM) == 0


def _rope_kernel(pos_ref, inv_ref, cos_ref, sin_ref):
    ang = pos_ref[...].astype(F32) * inv_ref[...]
    lane = lax.broadcasted_iota(jnp.int32, ang.shape, 1)
    sign = jnp.where(lane < HEAD_DIM, -1.0, 1.0)
    cos_ref[...] = jnp.cos(ang)
    sin_ref[...] = jnp.sin(ang) * sign


def _rope_tables(positions):
    n = positions.size
    half = HEAD_DIM // 2
    inv = 1.0 / (ROPE_BASE ** jnp.linspace(0.0, 1.0, half, dtype=F32))
    inv = jnp.tile(inv, LANES // half).reshape(1, LANES)
    tm = 1024
    return pl.pallas_call(
        _rope_kernel,
        grid=(n // tm,),
        in_specs=[pl.BlockSpec((tm, 1), lambda i: (i, 0)),
                  pl.BlockSpec((1, LANES), lambda i: (0, 0))],
        out_specs=[pl.BlockSpec((tm, LANES), lambda i: (i, 0))] * 2,
        out_shape=[jax.ShapeDtypeStruct((n, LANES), F32)] * 2,
        compiler_params=_params("parallel"),
        name="rope_tables",
    )(positions.reshape(n, 1), inv)


def _inproj_kernel(x_ref, g_ref, w_ref, perm_ref, a_ref, b_ref, c_ref):
    h = _rms(x_ref[...], g_ref[0:1, :]).astype(BF16)
    hp = jnp.concatenate([_dot(perm_ref[...], h[g * BLK:(g + 1) * BLK]).astype(BF16)
                          for g in range(h.shape[0] // BLK)], axis=0)
    a_ref[...] = _dot(hp, w_ref[:, :QKV_A]).astype(BF16)
    b_ref[...] = _dot(h, w_ref[:, QKV_A:QKV_A + UV_B])
    c_ref[...] = _dot(h, w_ref[:, QKV_A + UV_B:])


def _inproj(x, g, w, perm, l):
    n = x.shape[0]
    tm = ROW_TILE
    return pl.pallas_call(
        _inproj_kernel,
        grid=(n // tm,),
        in_specs=[pl.BlockSpec((tm, D_MODEL), lambda i: (i, 0)),
                  _layer(g, l), _layer(w, l),
                  pl.BlockSpec(perm.shape, lambda i: (0, 0))],
        out_specs=[pl.BlockSpec((tm, QKV_A), lambda i: (i, 0)),
                   pl.BlockSpec((tm, UV_B), lambda i: (i, 0)),
                   pl.BlockSpec((tm, QKVG_C), lambda i: (i, 0))],
        out_shape=[jax.ShapeDtypeStruct((n, QKV_A), BF16),
                   jax.ShapeDtypeStruct((n, UV_B), F32),
                   jax.ShapeDtypeStruct((n, QKVG_C), F32)],
        compiler_params=_params("parallel"),
        name="inproj",
    )(x, g, w, perm)


MAX_DIL = DIL_PATTERNS[-1][1]
SUB = 8


def _group_permutation():
    i = jnp.arange(BLK)
    src = MAX_DIL * (i % SUB) + i // SUB
    return (src[:, None] == jnp.arange(BLK)[None, :]).astype(BF16)


def _run_offsets(d):
    return [BLK * g + SUB * d * c for g in range(d) for c in range(MAX_DIL // d)]


def _block_load(ref, p, base, d):
    if d == 1:
        return ref[p, pl.ds(pl.multiple_of(base, BLK), BLK), :]
    return jnp.concatenate(
        [ref[p, pl.ds(pl.multiple_of(base + off, SUB), SUB), :] for off in _run_offsets(d)], axis=0)


def _block_store(ref, p, base, d, val):
    if d == 1:
        ref[p, pl.ds(pl.multiple_of(base, BLK), BLK), :] = val
        return
    for i, off in enumerate(_run_offsets(d)):
        ref[p, pl.ds(pl.multiple_of(base + off, SUB), SUB), :] = val[i * SUB:(i + 1) * SUB]


def _dil_blocks(refs, branch, d, blocks, first, lookahead):
    bias_ref, qp, kring, vring, acc_o, acc_m, acc_l = refs
    lo = _lane_lo((BLK, LANES))
    units = [(blk, p) for blk in blocks for p in range(N_PAIRS)]

    def scores(unit):
        (qbase, kcbase, kpbase, _), p = unit
        q = _block_load(qp, p, qbase, d).astype(BF16)
        k = jnp.concatenate([_block_load(kring, p, kpbase, d), _block_load(kring, p, kcbase, d)],
                            axis=0).astype(BF16)
        return [_dot_nt(jnp.where(lo if e == 0 else ~lo, q, jnp.zeros_like(q)), k) for e in range(2)]

    def softmax_pv(unit, s_pair):
        (qbase, kcbase, kpbase, has_prev), p = unit
        bias_prev = jnp.where(has_prev, bias_ref[branch, :, :BLK], -jnp.inf)
        bias_cur = bias_ref[branch, :, BLK:]
        v = jnp.concatenate([_block_load(vring, p, kpbase, d), _block_load(vring, p, kcbase, d)],
                            axis=0).astype(BF16)
        o_h, m_h, l_h = [], [], []
        for s in s_pair:
            sp = s[:, :BLK] + bias_prev
            sc = s[:, BLK:] + bias_cur
            m = jnp.max(jnp.maximum(sp, sc), axis=-1, keepdims=True)
            pp = jnp.exp(sp - m)
            pc = jnp.exp(sc - m)
            l_h.append(jnp.sum(pp + pc, axis=-1, keepdims=True))
            m_h.append(m)
            o_h.append(_dot(jnp.concatenate([pp, pc], axis=1).astype(BF16), v))
        return o_h, m_h, l_h

    def merge(unit, stats):
        (qbase, _, _, _), p = unit
        o_h, m_h, l_h = stats
        o_b = jnp.where(lo, o_h[0], o_h[1])
        m_b = jnp.where(lo, m_h[0], m_h[1])
        l_b = jnp.where(lo, l_h[0], l_h[1])
        if first:
            _block_store(acc_o, p, qbase, d, o_b)
            _block_store(acc_m, p, qbase, d, m_b)
            _block_store(acc_l, p, qbase, d, l_b)
        else:
            m_old = _block_load(acc_m, p, qbase, d)
            m_new = jnp.maximum(m_old, m_b)
            a_old = jnp.exp(m_old - m_new)
            a_b = jnp.exp(m_b - m_new)
            _block_store(acc_o, p, qbase, d, _block_load(acc_o, p, qbase, d) * a_old + o_b * a_b)
            _block_store(acc_l, p, qbase, d, _block_load(acc_l, p, qbase, d) * a_old + l_b * a_b)
            _block_store(acc_m, p, qbase, d, m_new)

    ahead = min(lookahead, len(units))
    pending = [scores(u) for u in units[:ahead]]
    stats_prev = None
    for i, unit in enumerate(units):
        if i + ahead < len(units):
            pending.append(scores(units[i + ahead]))
        stats = softmax_pv(unit, pending[i])
        if stats_prev is not None:
            merge(units[i - 1], stats_prev)
        stats_prev = stats
    merge(units[-1], stats_prev)


def _dil_kernel(q_ref, k_ref, v_ref, bias_ref, o_ref, qp, kring, vring, acc_o, acc_m, acc_l):
    t = pl.program_id(1)
    cur = pl.multiple_of((t % 2) * SUPER, SUPER)
    other = pl.multiple_of(SUPER - cur, SUPER)

    @pl.when(t == 0)
    def _():
        kring[:, SUPER:, :] = jnp.zeros((N_PAIRS, SUPER, LANES), F32)
        vring[:, SUPER:, :] = jnp.zeros((N_PAIRS, SUPER, LANES), F32)

    for p in range(N_PAIRS):
        cols = slice(p * LANES, (p + 1) * LANES)
        qp[p] = q_ref[:, cols].astype(F32) * (HEAD_DIM ** -0.5)
        kring[p, pl.ds(cur, SUPER), :] = k_ref[:, cols].astype(F32)
        vring[p, pl.ds(cur, SUPER), :] = v_ref[:, cols].astype(F32)
    refs = (bias_ref, qp, kring, vring, acc_o, acc_m, acc_l)

    for branch, (window, d) in enumerate(DIL_PATTERNS):
        nblk = MAX_DIL // d
        span = BLK * d

        def block_rows(j, d=d, nblk=nblk, span=span):
            if nblk == 1:
                qbase = pl.multiple_of(j * SUB, SUB)
                return qbase, cur + qbase, other + qbase, t > 0
            r = j // nblk
            nb = j % nblk
            qbase = pl.multiple_of(nb * span + r * SUB, SUB)
            kpbase = pl.multiple_of(
                jnp.where(nb == 0, other + (nblk - 1) * span + r * SUB, cur + qbase - span), SUB)
            return qbase, cur + qbase, kpbase, (t > 0) | (nb > 0)

        unroll, lookahead = DIL_UNROLL[branch], DIL_LOOKAHEAD[branch]

        def body(i, carry, branch=branch, d=d, block_rows=block_rows, unroll=unroll, lookahead=lookahead):
            blocks = [block_rows(i * unroll + u) for u in range(unroll)]
            _dil_blocks(refs, branch, d, blocks, first=(branch == 0), lookahead=lookahead)
            return carry

        lax.fori_loop(0, SUPER // BLK // unroll, body, 0)

    def finish(i, carry):
        rows = pl.ds(pl.multiple_of(i * BLK, BLK), BLK)
        for p in range(N_PAIRS):
            o_ref[rows, p * LANES:(p + 1) * LANES] = (acc_o[p, rows, :] / acc_l[p, rows, :]).astype(BF16)
        return carry

    lax.fori_loop(0, SUPER // BLK, finish, 0)


def _dilated_bias():
    out = []
    i = jnp.arange(BLK)
    for _, d in DIL_PATTERNS:
        n_c = MAX_DIL // d
        pos = (BLK // d) * (i // (SUB * n_c)) + n_c * (i % SUB) + (i // SUB) % n_c
        prev_ok = pos[None, :] >= pos[:, None]
        cur_ok = pos[None, :] <= pos[:, None]
        out.append(jnp.where(jnp.concatenate([prev_ok, cur_ok], axis=1), 0.0, -jnp.inf))
    return jnp.stack(out).astype(F32)


def _dilated_attention(qkv, batch, seq):
    n = batch * seq
    tiles = seq // SUPER
    bias = _dilated_bias()
    spec = lambda which: pl.BlockSpec((SUPER, A_WIDTH), lambda b, t: (b * tiles + t, which))
    tile_f32 = pltpu.VMEM((N_PAIRS, SUPER, LANES), F32)
    ring_f32 = pltpu.VMEM((N_PAIRS, 2 * SUPER, LANES), F32)
    return pl.pallas_call(
        _dil_kernel,
        grid=(batch, tiles),
        in_specs=[spec(0), spec(1), spec(2), pl.BlockSpec(bias.shape, lambda b, t: (0, 0, 0))],
        out_specs=pl.BlockSpec((SUPER, A_WIDTH), lambda b, t: (b * tiles + t, 0)),
        out_shape=jax.ShapeDtypeStruct((n, A_WIDTH), BF16),
        scratch_shapes=[tile_f32, ring_f32, ring_f32, tile_f32, tile_f32, tile_f32],
        compiler_params=_params("parallel", "arbitrary"),
        name="dilated_attention",
    )(qkv, qkv, qkv, bias)


def _rotate(t, cos, sin_signed):
    return t * cos + pltpu.roll(t, HEAD_DIM, 1) * sin_signed


def _mix_kernel(b_ref, c_ref, cos_ref, sin_ref, sw_ref, sb_ref, lg_ref, lb_ref,
                dec_ref, qdec_ref, kdec_ref, cdm_ref, yb_ref, yc_ref, state_ref, wt_ref):
    @pl.when(pl.program_id(1) == 0)
    def _():
        state_ref[...] = jnp.zeros_like(state_ref)

    lo = _lane_lo((BLK, LANES))
    row = lax.broadcasted_iota(jnp.int32, (BLK, BLK), 0)
    col = lax.broadcasted_iota(jnp.int32, (BLK, BLK), 1)
    tril = row >= col
    same_head = ((row >> 5) & 1) == (col >> 6)
    lane = lax.broadcasted_iota(jnp.int32, (BLK, LANES), 1)
    qk_head0 = (lane & (HEAD_DIM // 2)) == 0
    for g in range(SGU_GROUPS):
        wt_ref[g] = jnp.where(tril, sw_ref[g], 0.0).astype(BF16)
    inv_n = 1.0 / HEAD_DIM

    def sgu(rows):
        u = jax.nn.gelu(b_ref[rows, :B_WIDTH])
        v = jax.nn.gelu(b_ref[rows, B_WIDTH:])
        mu = jnp.mean(v, axis=-1, keepdims=True)
        var = jnp.mean(jnp.square(v - mu), axis=-1, keepdims=True)
        v = (v - mu) * lax.rsqrt(var + EPS) * lg_ref[...] + lb_ref[...]
        for gp in range(B_WIDTH // LANES):
            cols = slice(gp * LANES, (gp + 1) * LANES)
            vp = v[:, cols].astype(BF16)
            s = [_dot(wt_ref[2 * gp + e], vp) for e in range(2)]
            gate = jnp.where(lo, s[0], s[1]) + sb_ref[:, cols]
            yb_ref[rows, cols] = (u[:, cols] * gate).astype(BF16)

    def retention(row_list):
        work = [(rows, p) for rows in row_list for p in range(N_PAIRS)]
        stage1 = []
        for rows, p in work:
            cols = slice(p * LANES, (p + 1) * LANES)
            cos, sin = cos_ref[rows, :], sin_ref[rows, :]
            q = _rotate(c_ref[rows, p * LANES:(p + 1) * LANES], cos, sin)
            k = _rotate(c_ref[rows, C_WIDTH + p * LANES:C_WIDTH + (p + 1) * LANES], cos, sin)
            vb = c_ref[rows, 2 * C_WIDTH + p * LANES:2 * C_WIDTH + (p + 1) * LANES].astype(BF16)
            qb, kb = q.astype(BF16), k.astype(BF16)
            scores = [_dot_nt(jnp.where(qk_head0 if e == 0 else ~qk_head0, qb, jnp.zeros_like(qb)), kb)
                      for e in range(2)]
            kv = _dot_tn((k * kdec_ref[:, cols]).astype(BF16), vb)
            stage1.append((scores, kv, (q * qdec_ref[:, cols]).astype(BF16), vb))
        stage2 = []
        for (rows, p), (scores, kv, qd, vb) in zip(work, stage1):
            state = state_ref[p]
            y_x = _dot(qd, state.astype(BF16))
            state_ref[p] = state * cdm_ref[p] + jnp.where(same_head, kv, 0.0)
            ys = [_dot((scores[e] * dec_ref[2 * p + e]).astype(BF16), vb)
                  for e in range(2)]
            stage2.append((ys, y_x))
        for (rows, p), (ys, y_x) in zip(work, stage2):
            cols = slice(p * LANES, (p + 1) * LANES)
            y = jnp.where(lo, ys[0], ys[1]) + y_x
            mu_lo = jnp.sum(jnp.where(lo, y, 0.0), axis=-1, keepdims=True) * inv_n
            mu_hi = jnp.sum(jnp.where(lo, 0.0, y), axis=-1, keepdims=True) * inv_n
            yc_ = y - jnp.where(lo, mu_lo, mu_hi)
            sq = yc_ * yc_
            var_lo = jnp.sum(jnp.where(lo, sq, 0.0), axis=-1, keepdims=True) * inv_n
            var_hi = jnp.sum(jnp.where(lo, 0.0, sq), axis=-1, keepdims=True) * inv_n
            yn = yc_ * lax.rsqrt(jnp.where(lo, var_lo, var_hi) + EPS)
            gate = c_ref[rows, 3 * C_WIDTH + p * LANES:3 * C_WIDTH + (p + 1) * LANES]
            yc_ref[rows, cols] = (jax.nn.silu(gate) * yn).astype(BF16)

    def group(i, carry):
        row_list = [pl.ds(pl.multiple_of((i * MIX_UNROLL + cc) * BLK, BLK), BLK)
                    for cc in range(MIX_UNROLL)]
        for rows in row_list:
            sgu(rows)
        retention(row_list)
        return carry

    lax.fori_loop(0, MIX_CHUNKS // MIX_UNROLL, group, 0)


def _qk_lane_head():
    lane = jnp.arange(C_WIDTH)
    return 2 * (lane // LANES) + ((lane >> 5) & 1)


def _qk_reorder(w):
    lead = w.shape[:-1]
    w = w.reshape(lead + (N_PAIRS, 2, 2, HEAD_DIM // 2))
    return jnp.swapaxes(w, -3, -2).reshape(lead + (C_WIDTH,))


def _inproj_weight(w):
    q0 = QKV_A + UV_B
    w = w.astype(BF16)
    return jnp.concatenate([w[..., :q0], _qk_reorder(w[..., q0:q0 + C_WIDTH]),
                            _qk_reorder(w[..., q0 + C_WIDTH:q0 + 2 * C_WIDTH]),
                            w[..., q0 + 2 * C_WIDTH:]], axis=-1)


def _retention_tables():
    heads = jnp.arange(RET_HEADS, dtype=F32)
    log_g = jnp.log1p(-jnp.power(2.0, -5.0 - heads))
    idx = jnp.arange(BLK, dtype=F32)
    rel = idx[:, None] - idx[None, :]
    scale = HEAD_DIM ** -0.5
    decay = jnp.where(rel[None] >= 0,
                      jnp.exp(jnp.maximum(rel, 0.0)[None] * log_g[:, None, None]), 0.0) * scale
    k_dec = jnp.exp((BLK - 1 - idx)[:, None] * log_g[None, :]) * scale
    q_dec = jnp.exp((idx + 1.0)[:, None] * log_g[None, :])
    lane_head = _qk_lane_head()
    k_dec = k_dec[:, lane_head]
    q_dec = q_dec[:, lane_head]
    chunk_decay = jnp.exp(BLK * log_g)
    per_row = chunk_decay[lane_head].reshape(N_PAIRS, LANES)
    row_head = (jnp.arange(LANES) >> 5) & 1
    col_head = jnp.arange(LANES) >> 6
    same = row_head[:, None] == col_head[None, :]
    cdm = jnp.where(same[None], per_row[:, :, None], 0.0)
    return decay, q_dec, k_dec, cdm


def _mixers_bc(zb, zc, cos, sin, sgu_w, sgu_bias, ln_g, ln_b, l, batch, seq):
    n = batch * seq
    tm = MIX_CHUNKS * BLK
    nc = seq // tm
    decay, q_dec, k_dec, cdm = _retention_tables()
    row = lambda b, c: (b * nc + c, 0)
    const2 = lambda b, c: (0, 0)
    const3 = lambda b, c: (0, 0, 0)
    return pl.pallas_call(
        _mix_kernel,
        grid=(batch, nc),
        in_specs=[pl.BlockSpec((tm, UV_B), row),
                  pl.BlockSpec((tm, QKVG_C), row),
                  pl.BlockSpec((tm, LANES), row),
                  pl.BlockSpec((tm, LANES), row),
                  _layer(sgu_w, l), _layer(sgu_bias, l), _layer(ln_g, l), _layer(ln_b, l),
                  pl.BlockSpec(decay.shape, const3),
                  pl.BlockSpec(q_dec.shape, const2),
                  pl.BlockSpec(k_dec.shape, const2),
                  pl.BlockSpec(cdm.shape, const3)],
        out_specs=[pl.BlockSpec((tm, B_WIDTH), row),
                   pl.BlockSpec((tm, C_WIDTH), row)],
        out_shape=[jax.ShapeDtypeStruct((n, B_WIDTH), BF16),
                   jax.ShapeDtypeStruct((n, C_WIDTH), BF16)],
        scratch_shapes=[pltpu.VMEM((N_PAIRS, LANES, LANES), F32),
                        pltpu.VMEM((SGU_GROUPS, BLK, BLK), BF16)],
        compiler_params=_params("parallel", "arbitrary"),
        name="sgu_retention",
    )(zb, zc, cos, sin, sgu_w, sgu_bias, ln_g, ln_b, decay, q_dec, k_dec, cdm)


def _outproj_kernel(ya_ref, yb_ref, yc_ref, x_ref, g_ref, w_ref, unperm_ref, out_ref):
    tm = ya_ref.shape[0]
    ya = jnp.concatenate([_dot(unperm_ref[...], ya_ref[g * BLK:(g + 1) * BLK, :]).astype(BF16)
                          for g in range(tm // BLK)], axis=0)
    mix = jnp.concatenate([ya, yb_ref[...], yc_ref[...]], axis=1)
    out_ref[...] = x_ref[...] + _rms(_dot(mix, w_ref[...]), g_ref[1:2, :])


def _outproj(ya, yb, yc, x, g, w, unperm, l):
    n = x.shape[0]
    tm = ROW_TILE
    row = lambda i: (i, 0)
    const = lambda i: (0, 0)
    return pl.pallas_call(
        _outproj_kernel,
        grid=(n // tm,),
        in_specs=[pl.BlockSpec((tm, A_WIDTH), row),
                  pl.BlockSpec((tm, B_WIDTH), row),
                  pl.BlockSpec((tm, C_WIDTH), row),
                  pl.BlockSpec((tm, D_MODEL), row),
                  _layer(g, l), _layer(w, l),
                  pl.BlockSpec(unperm.shape, const)],
        out_specs=pl.BlockSpec((tm, D_MODEL), row),
        out_shape=jax.ShapeDtypeStruct((n, D_MODEL), F32),
        compiler_params=_params("parallel"),
        name="outproj",
    )(ya, yb, yc, x, g, w, unperm)


def _kv_kernel(m_ref, g_ref, w_ref, kv_ref):
    h = _rms(m_ref[...], g_ref[6:7, :]).astype(BF16)
    kv_ref[...] = _dot(h, w_ref[...]).astype(BF16)


def _memory_kv(mem, g, w, l):
    n = mem.shape[0]
    return pl.pallas_call(
        _kv_kernel,
        grid=(n // MEM_LEN,),
        in_specs=[pl.BlockSpec((MEM_LEN, D_MODEL), lambda i: (i, 0)),
                  _layer(g, l), _layer(w, l)],
        out_specs=pl.BlockSpec((MEM_LEN, 2 * D_MODEL), lambda i: (i, 0)),
        out_shape=jax.ShapeDtypeStruct((n, 2 * D_MODEL), BF16),
        compiler_params=_params("parallel"),
        name="memory_kv",
    )(mem, g, w)


def _cross_kernel(x_ref, kv_ref, g_ref, wq_ref, wo_ref, out_ref):
    x = x_ref[...]
    h = _rms(x, g_ref[2:3, :]).astype(BF16)
    q = _dot(h, wq_ref[...]).astype(BF16)
    scale = X_HEAD_DIM ** -0.5
    heads = [slice(hd * X_HEAD_DIM, (hd + 1) * X_HEAD_DIM) for hd in range(X_HEADS)]
    scores = [_dot_nt(q[:, cols], kv_ref[:, cols]) for cols in heads]
    outs = []
    for hd, s in enumerate(scores):
        s = s * scale
        m = jnp.max(s, axis=-1, keepdims=True)
        p = jnp.exp(s - m)
        p = p / jnp.sum(p, axis=-1, keepdims=True)
        v = kv_ref[:, D_MODEL + hd * X_HEAD_DIM:D_MODEL + (hd + 1) * X_HEAD_DIM]
        outs.append(_dot(p.astype(BF16), v).astype(BF16))
    acc = _dot(jnp.concatenate(outs, axis=1), wo_ref[...])
    out_ref[...] = x + _rms(acc, g_ref[3:4, :])


def _cross(x, kv, g, wq, wo, l, batch, seq):
    n = x.shape[0]
    tm = ROW_TILE
    per_batch = seq // tm
    row = lambda i: (i, 0)
    const = lambda i: (0, 0)
    return pl.pallas_call(
        _cross_kernel,
        grid=(n // tm,),
        in_specs=[pl.BlockSpec((tm, D_MODEL), row),
                  pl.BlockSpec((MEM_LEN, 2 * D_MODEL), lambda i: (i // per_batch, 0)),
                  _layer(g, l), _layer(wq, l), _layer(wo, l)],
        out_specs=pl.BlockSpec((tm, D_MODEL), row),
        out_shape=jax.ShapeDtypeStruct((n, D_MODEL), F32),
        compiler_params=_params("parallel"),
        name="cross_attention",
    )(x, kv, g, wq, wo)


def _mlp_kernel(x_ref, g_ref, wu_ref, wd_ref, out_ref):
    x = x_ref[...]
    h = _rms(x, g_ref[4:5, :]).astype(BF16)
    acc = jnp.zeros(x.shape, F32)
    for c in range(D_FF // FF_CHUNK):
        cols = slice(c * FF_CHUNK, (c + 1) * FF_CHUNK)
        f = jnp.square(jnp.maximum(_dot(h, wu_ref[:, cols]), 0.0))
        acc += _dot(f.astype(BF16), wd_ref[cols, :])
    out_ref[...] = x + _rms(acc, g_ref[5:6, :])


def _mlp(x, g, w_up, w_down, l):
    n = x.shape[0]
    tm = ROW_TILE
    row = lambda i: (i, 0)
    const = lambda i: (0, 0)
    return pl.pallas_call(
        _mlp_kernel,
        grid=(n // tm,),
        in_specs=[pl.BlockSpec((tm, D_MODEL), row),
                  _layer(g, l),
                  _layer(w_up, l, pipeline_mode=pl.Buffered(1)),
                  _layer(w_down, l, pipeline_mode=pl.Buffered(1))],
        out_specs=pl.BlockSpec((tm, D_MODEL), row),
        out_shape=jax.ShapeDtypeStruct((n, D_MODEL), F32),
        compiler_params=_params("parallel"),
        name="mlp",
    )(x, g, w_up, w_down)


def kernel(x, mem, positions, norm_g, w_in, sgu_w, sgu_b, sgu_ln_g, sgu_ln_b,
           w_out, x_wq, x_wkv, x_wo, w_up, w_down):
    batch, seq, _ = x.shape
    assert x.shape == (batch, seq, D_MODEL) and seq % (BLK * DIL_PATTERNS[-1][1]) == 0
    assert (batch * seq) % ROW_TILE == 0 and seq % ROW_TILE == 0
    assert mem.shape == (batch, MEM_LEN, D_MODEL) and w_in.shape == (DEPTH, D_MODEL, IN_WIDTH)
    xf = x.reshape(batch * seq, D_MODEL)
    memf = mem.reshape(batch * MEM_LEN, D_MODEL)
    cos, sin = _rope_tables(positions)
    perm = _group_permutation()
    w_in_b, w_out_b = _inproj_weight(w_in), w_out.astype(BF16)
    wq_b, wkv_b, wo_b = x_wq.astype(BF16), x_wkv.astype(BF16), x_wo.astype(BF16)
    w_up_b, w_down_b = w_up.astype(BF16), w_down.astype(BF16)
    sgu_bias = jnp.repeat(jnp.swapaxes(sgu_b, 1, 2), HEAD_DIM, axis=2)
    ln_g = sgu_ln_g.reshape(DEPTH, 1, B_WIDTH)
    ln_b = sgu_ln_b.reshape(DEPTH, 1, B_WIDTH)
    for l in range(DEPTH):
        za, zb, zc = _inproj(xf, norm_g, w_in_b, perm, l)
        ya = _dilated_attention(za, batch, seq)
        yb, yc = _mixers_bc(zb, zc, cos, sin, sgu_w, sgu_bias, ln_g, ln_b, l, batch, seq)
        xf = _outproj(ya, yb, yc, xf, norm_g, w_out_b, perm.T, l)
        kv = _memory_kv(memf, norm_g, wkv_b, l)
        xf = _cross(xf, kv, norm_g, wq_b, wo_b, l, batch, seq)
        xf = _mlp(xf, norm_g, w_up_b, w_down_b, l)
    return xf.reshape(batch, seq, D_MODEL)
```

```python
import functools

import jax
import jax.numpy as jnp
from jax import lax
from jax.experimental import pallas as pl
from jax.experimental.pallas import tpu as pltpu

F32 = jnp.float32
BF16 = jnp.bfloat16

D_MODEL = 1024
DEPTH = 2
MEM_LEN = 256
HEAD_DIM = 64
DIL_HEADS = 6
DIL_PATTERNS = ((128, 1), (512, 4), (2048, 16))
BLK = 128
SGU_GROUPS = 4
RET_HEADS = 6
ROPE_BASE = 10000.0
A_WIDTH = DIL_HEADS * HEAD_DIM
B_WIDTH = SGU_GROUPS * HEAD_DIM
C_WIDTH = RET_HEADS * HEAD_DIM
QKV_A = 3 * A_WIDTH
UV_B = 2 * B_WIDTH
QKVG_C = 4 * C_WIDTH
IN_WIDTH = QKV_A + UV_B + QKVG_C
X_HEADS = 4
X_HEAD_DIM = D_MODEL // X_HEADS
D_FF = 4 * D_MODEL
EPS = 1e-6

LANES = 128
N_PAIRS = A_WIDTH // LANES
VMEM_LIMIT = 56 * 1024 * 1024

SUPER = BLK * DIL_PATTERNS[-1][1]

ROW_TILE = 512
DIL_UNROLL = (4, 4, 4)
DIL_LOOKAHEAD = (12, 3, 3)
MIX_UNROLL = 2
FF_CHUNK = 1024


def _params(*sem):
    return pltpu.CompilerParams(dimension_semantics=sem, vmem_limit_bytes=VMEM_LIMIT)


def _layer(arr, l, **kw):
    zeros = (0,) * (arr.ndim - 1)
    return pl.BlockSpec((None,) + arr.shape[1:], lambda *_: (l,) + zeros, **kw)


def _rms(x, g):
    ms = jnp.mean(x * x, axis=-1, keepdims=True)
    return x * lax.rsqrt(ms + EPS) * g


def _dot(a, b):
    return jnp.dot(a, b, preferred_element_type=F32)


def _dot_nt(a, b):
    return lax.dot_general(a, b, (((1,), (1,)), ((), ())), preferred_element_type=F32)


def _dot_tn(a, b):
    return lax.dot_general(a, b, (((0,), (0,)), ((), ())), preferred_element_type=F32)


def _lane_lo(shape):
    lane = lax.broadcasted_iota(jnp.int32, shape, len(shape) - 1)
    return (lane & HEAD_DIM) == 0


def _rope_kernel(pos_ref, inv_ref, cos_ref, sin_ref):
    ang = pos_ref[...].astype(F32) * inv_ref[...]
    lane = lax.broadcasted_iota(jnp.int32, ang.shape, 1)
    sign = jnp.where(lane < HEAD_DIM, -1.0, 1.0)
    cos_ref[...] = jnp.cos(ang)
    sin_ref[...] = jnp.sin(ang) * sign


def _rope_tables(positions):
    n = positions.size
    half = HEAD_DIM // 2
    inv = 1.0 / (ROPE_BASE ** jnp.linspace(0.0, 1.0, half, dtype=F32))
    inv = jnp.tile(inv, LANES // half).reshape(1, LANES)
    tm = 1024
    return pl.pallas_call(
        _rope_kernel,
        grid=(n // tm,),
        in_specs=[pl.BlockSpec((tm, 1), lambda i: (i, 0)),
                  pl.BlockSpec((1, LANES), lambda i: (0, 0))],
        out_specs=[pl.BlockSpec((tm, LANES), lambda i: (i, 0))] * 2,
        out_shape=[jax.ShapeDtypeStruct((n, LANES), F32)] * 2,
        compiler_params=_params("parallel"),
        name="rope_tables",
    )(positions.reshape(n, 1), inv)


MAX_DIL = DIL_PATTERNS[-1][1]
SUB = 8


def _group_permutation():
    i = jnp.arange(BLK)
    src = MAX_DIL * (i % SUB) + i // SUB
    return (src[:, None] == jnp.arange(BLK)[None, :]).astype(BF16)


def _run_offsets(d):
    return [BLK * g + SUB * d * c for g in range(d) for c in range(MAX_DIL // d)]


def _block_load(ref, p, base, d):
    if d == 1:
        return ref[p, pl.ds(pl.multiple_of(base, BLK), BLK), :]
    return jnp.concatenate(
        [ref[p, pl.ds(pl.multiple_of(base + off, SUB), SUB), :] for off in _run_offsets(d)], axis=0)


def _block_store(ref, p, base, d, val):
    if d == 1:
        ref[p, pl.ds(pl.multiple_of(base, BLK), BLK), :] = val
        return
    for i, off in enumerate(_run_offsets(d)):
        ref[p, pl.ds(pl.multiple_of(base + off, SUB), SUB), :] = val[i * SUB:(i + 1) * SUB]


def _dil_blocks(refs, branch, d, blocks, first, lookahead):
    bias_ref, qp, kring, vring, acc_o, acc_m, acc_l = refs
    lo = _lane_lo((BLK, LANES))
    units = [(blk, p) for blk in blocks for p in range(N_PAIRS)]

    def scores(unit):
        (qbase, kcbase, kpbase, _), p = unit
        q = _block_load(qp, p, qbase, d).astype(BF16)
        k = jnp.concatenate([_block_load(kring, p, kpbase, d), _block_load(kring, p, kcbase, d)],
                            axis=0).astype(BF16)
        return [_dot_nt(jnp.where(lo if e == 0 else ~lo, q, jnp.zeros_like(q)), k) for e in range(2)]

    def softmax_pv(unit, s_pair):
        (qbase, kcbase, kpbase, has_prev), p = unit
        bias_prev = jnp.where(has_prev, bias_ref[branch, :, :BLK], -jnp.inf)
        bias_cur = bias_ref[branch, :, BLK:]
        v = jnp.concatenate([_block_load(vring, p, kpbase, d), _block_load(vring, p, kcbase, d)],
                            axis=0).astype(BF16)
        o_h, m_h, l_h = [], [], []
        for s in s_pair:
            sp = s[:, :BLK] + bias_prev
            sc = s[:, BLK:] + bias_cur
            m = jnp.max(jnp.maximum(sp, sc), axis=-1, keepdims=True)
            pp = jnp.exp(sp - m)
            pc = jnp.exp(sc - m)
            l_h.append(jnp.sum(pp + pc, axis=-1, keepdims=True))
            m_h.append(m)
            o_h.append(_dot(jnp.concatenate([pp, pc], axis=1).astype(BF16), v))
        return o_h, m_h, l_h

    def merge(unit, stats):
        (qbase, _, _, _), p = unit
        o_h, m_h, l_h = stats
        o_b = jnp.where(lo, o_h[0], o_h[1])
        m_b = jnp.where(lo, m_h[0], m_h[1])
        l_b = jnp.where(lo, l_h[0], l_h[1])
        if first:
            _block_store(acc_o, p, qbase, d, o_b)
            _block_store(acc_m, p, qbase, d, m_b)
            _block_store(acc_l, p, qbase, d, l_b)
        else:
            m_old = _block_load(acc_m, p, qbase, d)
            m_new = jnp.maximum(m_old, m_b)
            a_old = jnp.exp(m_old - m_new)
            a_b = jnp.exp(m_b - m_new)
            _block_store(acc_o, p, qbase, d, _block_load(acc_o, p, qbase, d) * a_old + o_b * a_b)
            _block_store(acc_l, p, qbase, d, _block_load(acc_l, p, qbase, d) * a_old + l_b * a_b)
            _block_store(acc_m, p, qbase, d, m_new)

    ahead = min(lookahead, len(units))
    pending = [scores(u) for u in units[:ahead]]
    stats_prev = None
    for i, unit in enumerate(units):
        if i + ahead < len(units):
            pending.append(scores(units[i + ahead]))
        stats = softmax_pv(unit, pending[i])
        if stats_prev is not None:
            merge(units[i - 1], stats_prev)
        stats_prev = stats
    merge(units[-1], stats_prev)


def _dil_kernel(q_ref, k_ref, v_ref, bias_ref, o_ref, qp, kring, vring, acc_o, acc_m, acc_l):
    t = pl.program_id(1)
    cur = pl.multiple_of((t % 2) * SUPER, SUPER)
    other = pl.multiple_of(SUPER - cur, SUPER)

    @pl.when(t == 0)
    def _():
        kring[:, SUPER:, :] = jnp.zeros((N_PAIRS, SUPER, LANES), F32)
        vring[:, SUPER:, :] = jnp.zeros((N_PAIRS, SUPER, LANES), F32)

    for p in range(N_PAIRS):
        cols = slice(p * LANES, (p + 1) * LANES)
        qp[p] = q_ref[:, cols].astype(F32) * (HEAD_DIM ** -0.5)
        kring[p, pl.ds(cur, SUPER), :] = k_ref[:, cols].astype(F32)
        vring[p, pl.ds(cur, SUPER), :] = v_ref[:, cols].astype(F32)
    refs = (bias_ref, qp, kring, vring, acc_o, acc_m, acc_l)

    for branch, (window, d) in enumerate(DIL_PATTERNS):
        nblk = MAX_DIL // d
        span = BLK * d

        def block_rows(j, d=d, nblk=nblk, span=span):
            if nblk == 1:
                qbase = pl.multiple_of(j * SUB, SUB)
                return qbase, cur + qbase, other + qbase, t > 0
            r = j // nblk
            nb = j % nblk
            qbase = pl.multiple_of(nb * span + r * SUB, SUB)
            kpbase = pl.multiple_of(
                jnp.where(nb == 0, other + (nblk - 1) * span + r * SUB, cur + qbase - span), SUB)
            return qbase, cur + qbase, kpbase, (t > 0) | (nb > 0)

        unroll, lookahead = DIL_UNROLL[branch], DIL_LOOKAHEAD[branch]

        def body(i, carry, branch=branch, d=d, block_rows=block_rows, unroll=unroll, lookahead=lookahead):
            blocks = [block_rows(i * unroll + u) for u in range(unroll)]
            _dil_blocks(refs, branch, d, blocks, first=(branch == 0), lookahead=lookahead)
            return carry

        lax.fori_loop(0, SUPER // BLK // unroll, body, 0)

    def finish(i, carry):
        rows = pl.ds(pl.multiple_of(i * BLK, BLK), BLK)
        for p in range(N_PAIRS):
            o_ref[rows, p * LANES:(p + 1) * LANES] = (acc_o[p, rows, :] / acc_l[p, rows, :]).astype(BF16)
        return carry

    lax.fori_loop(0, SUPER // BLK, finish, 0)


def _dilated_bias():
    out = []
    i = jnp.arange(BLK)
    for _, d in DIL_PATTERNS:
        n_c = MAX_DIL // d
        pos = (BLK // d) * (i // (SUB * n_c)) + n_c * (i % SUB) + (i // SUB) % n_c
        prev_ok = pos[None, :] >= pos[:, None]
        cur_ok = pos[None, :] <= pos[:, None]
        out.append(jnp.where(jnp.concatenate([prev_ok, cur_ok], axis=1), 0.0, -jnp.inf))
    return jnp.stack(out).astype(F32)


def _dilated_attention(qkv, batch, seq):
    n = batch * seq
    tiles = seq // SUPER
    bias = _dilated_bias()
    spec = lambda which: pl.BlockSpec((SUPER, A_WIDTH), lambda b, t: (b * tiles + t, which))
    tile_f32 = pltpu.VMEM((N_PAIRS, SUPER, LANES), F32)
    ring_f32 = pltpu.VMEM((N_PAIRS, 2 * SUPER, LANES), F32)
    return pl.pallas_call(
        _dil_kernel,
        grid=(batch, tiles),
        in_specs=[spec(0), spec(1), spec(2), pl.BlockSpec(bias.shape, lambda b, t: (0, 0, 0))],
        out_specs=pl.BlockSpec((SUPER, A_WIDTH), lambda b, t: (b * tiles + t, 0)),
        out_shape=jax.ShapeDtypeStruct((n, A_WIDTH), BF16),
        scratch_shapes=[tile_f32, ring_f32, ring_f32, tile_f32, tile_f32, tile_f32],
        compiler_params=_params("parallel", "arbitrary"),
        name="dilated_attention",
    )(qkv, qkv, qkv, bias)


def _rotate(t, cos, sin_signed):
    return t * cos + pltpu.roll(t, HEAD_DIM, 1) * sin_signed


PROJ_PIECES = (
    (QKV_A + UV_B, 512), (QKV_A + UV_B + 512, 512), (QKV_A + UV_B + 1024, 512),
    (QKV_A, UV_B),
    (0, 640), (640, QKV_A - 640))


def _inmix_body(x_ref, g_ref, w_ref, perm_ref, cos_ref, sin_ref, sb_ref, lg_ref, lb_ref,
                dec_ref, qdec_ref, kdec_ref, cdm_ref, a_ref, yb_ref, yc_ref,
                zb_w, zc_w, zb_r, zc_r, state_ref, wt_ref):
    lo = _lane_lo((BLK, LANES))
    row = lax.broadcasted_iota(jnp.int32, (BLK, BLK), 0)
    col = lax.broadcasted_iota(jnp.int32, (BLK, BLK), 1)
    same_head = ((row >> 5) & 1) == (col >> 6)
    lane = lax.broadcasted_iota(jnp.int32, (BLK, LANES), 1)
    qk_head0 = (lane & (HEAD_DIM // 2)) == 0
    inv_n = 1.0 / HEAD_DIM

    def sgu_pre(rows):
        u = jax.nn.gelu(zb_r[rows, :B_WIDTH])
        v = jax.nn.gelu(zb_r[rows, B_WIDTH:])
        mu = jnp.mean(v, axis=-1, keepdims=True)
        var = jnp.mean(jnp.square(v - mu), axis=-1, keepdims=True)
        v = (v - mu) * lax.rsqrt(var + EPS) * lg_ref[...] + lb_ref[...]
        return u, v.astype(BF16)

    def sgu_post(rows, pre):
        u, v = pre
        for gp in range(B_WIDTH // LANES):
            cols = slice(gp * LANES, (gp + 1) * LANES)
            s = [_dot(wt_ref[2 * gp + e], v[:, cols]) for e in range(2)]
            gate = jnp.where(lo, s[0], s[1]) + sb_ref[:, cols]
            yb_ref[rows, cols] = (u[:, cols] * gate).astype(BF16)

    def ret_scores(row_list):
        work = [(rows, p) for rows in row_list for p in range(N_PAIRS)]
        out = []
        for rows, p in work:
            cols = slice(p * LANES, (p + 1) * LANES)
            cos, sin = cos_ref[rows, :], sin_ref[rows, :]
            q = _rotate(zc_r[rows, p * LANES:(p + 1) * LANES], cos, sin)
            k = _rotate(zc_r[rows, C_WIDTH + p * LANES:C_WIDTH + (p + 1) * LANES], cos, sin)
            vb = zc_r[rows, 2 * C_WIDTH + p * LANES:2 * C_WIDTH + (p + 1) * LANES].astype(BF16)
            qb, kb = q.astype(BF16), k.astype(BF16)
            scores = [_dot_nt(jnp.where(qk_head0 if e == 0 else ~qk_head0, qb, jnp.zeros_like(qb)), kb)
                      for e in range(2)]
            kv = _dot_tn((k * kdec_ref[:, cols]).astype(BF16), vb)
            out.append((scores, kv, (q * qdec_ref[:, cols]).astype(BF16), vb))
        return work, out

    def ret_values(work, stage1):
        out = []
        for (rows, p), (scores, kv, qd, vb) in zip(work, stage1):
            state = state_ref[p]
            y_x = _dot(qd, state.astype(BF16))
            state_ref[p] = state * cdm_ref[p] + jnp.where(same_head, kv, 0.0)
            ys = [_dot((scores[e] * dec_ref[2 * p + e]).astype(BF16), vb)
                  for e in range(2)]
            out.append((ys, y_x))
        return out

    def ret_norm(work, stage2):
        for (rows, p), (ys, y_x) in zip(work, stage2):
            cols = slice(p * LANES, (p + 1) * LANES)
            y = jnp.where(lo, ys[0], ys[1]) + y_x
            mu_lo = jnp.sum(jnp.where(lo, y, 0.0), axis=-1, keepdims=True) * inv_n
            mu_hi = jnp.sum(jnp.where(lo, 0.0, y), axis=-1, keepdims=True) * inv_n
            yc_ = y - jnp.where(lo, mu_lo, mu_hi)
            sq = yc_ * yc_
            var_lo = jnp.sum(jnp.where(lo, sq, 0.0), axis=-1, keepdims=True) * inv_n
            var_hi = jnp.sum(jnp.where(lo, 0.0, sq), axis=-1, keepdims=True) * inv_n
            yn = yc_ * lax.rsqrt(jnp.where(lo, var_lo, var_hi) + EPS)
            gate = zc_r[rows, 3 * C_WIDTH + p * LANES:3 * C_WIDTH + (p + 1) * LANES]
            yc_ref[rows, cols] = (jax.nn.silu(gate) * yn).astype(BF16)

    def project(lhs, piece):
        first, width = PROJ_PIECES[piece]
        y = _dot(lhs, w_ref[:, first:first + width])
        if first >= QKV_A + UV_B:
            zc_w[:, first - QKV_A - UV_B:first - QKV_A - UV_B + width] = y
        elif first >= QKV_A:
            zb_w[...] = y
        else:
            a_ref[:, first:first + width] = y.astype(BF16)

    tm = x_ref.shape[0]
    chunks = [slice(c * BLK, (c + 1) * BLK) for c in range(tm // BLK)]
    groups = [chunks[g:g + MIX_UNROLL] for g in range(0, len(chunks), MIX_UNROLL)]
    assert len(groups) == 2 and len(PROJ_PIECES) == 6

    pre = [sgu_pre(rows) for rows in groups[0]]
    work0, s1 = ret_scores(groups[0])
    h = _rms(x_ref[...], g_ref[0:1, :]).astype(BF16)
    hp = jnp.concatenate([_dot(perm_ref[...], h[c]).astype(BF16) for c in chunks], axis=0)
    project(h, 0)
    for rows, t in zip(groups[0], pre):
        sgu_post(rows, t)
    project(h, 1)
    s2 = ret_values(work0, s1)
    project(h, 2)
    ret_norm(work0, s2)
    pre = [sgu_pre(rows) for rows in groups[1]]
    work1, s1 = ret_scores(groups[1])
    project(h, 3)
    for rows, t in zip(groups[1], pre):
        sgu_post(rows, t)
    project(hp, 4)
    s2 = ret_values(work1, s1)
    project(hp, 5)
    ret_norm(work1, s2)


def _inmix_kernel(x_ref, g_ref, w_ref, perm_ref, cos_ref, sin_ref, sw_ref, sb_ref, lg_ref, lb_ref,
                  dec_ref, qdec_ref, kdec_ref, cdm_ref, a_ref, yb_ref, yc_ref,
                  zb0, zc0, zb1, zc1, state_ref, wt_ref, *, tiles_per_batch):
    i = pl.program_id(0)

    @pl.when(i == 0)
    def _():
        zb1[...] = jnp.zeros_like(zb1)
        zc1[...] = jnp.zeros_like(zc1)
        row = lax.broadcasted_iota(jnp.int32, (BLK, BLK), 0)
        col = lax.broadcasted_iota(jnp.int32, (BLK, BLK), 1)
        for g in range(SGU_GROUPS):
            wt_ref[g] = jnp.where(row >= col, sw_ref[g], 0.0).astype(BF16)

    @pl.when((i == 0) | ((i - 1) % tiles_per_batch == 0))
    def _():
        state_ref[...] = jnp.zeros_like(state_ref)

    common = (x_ref, g_ref, w_ref, perm_ref, cos_ref, sin_ref, sb_ref, lg_ref, lb_ref,
              dec_ref, qdec_ref, kdec_ref, cdm_ref, a_ref, yb_ref, yc_ref)

    @pl.when(i % 2 == 0)
    def _():
        _inmix_body(*common, zb0, zc0, zb1, zc1, state_ref, wt_ref)

    @pl.when(i % 2 == 1)
    def _():
        _inmix_body(*common, zb1, zc1, zb0, zc0, state_ref, wt_ref)


def _qk_lane_head():
    lane = jnp.arange(C_WIDTH)
    return 2 * (lane // LANES) + ((lane >> 5) & 1)


def _qk_reorder(w):
    lead = w.shape[:-1]
    w = w.reshape(lead + (N_PAIRS, 2, 2, HEAD_DIM // 2))
    return jnp.swapaxes(w, -3, -2).reshape(lead + (C_WIDTH,))


def _inproj_weight(w):
    q0 = QKV_A + UV_B
    w = w.astype(BF16)
    return jnp.concatenate([w[..., :q0], _qk_reorder(w[..., q0:q0 + C_WIDTH]),
                            _qk_reorder(w[..., q0 + C_WIDTH:q0 + 2 * C_WIDTH]),
                            w[..., q0 + 2 * C_WIDTH:]], axis=-1)


def _retention_tables():
    heads = jnp.arange(RET_HEADS, dtype=F32)
    log_g = jnp.log1p(-jnp.power(2.0, -5.0 - heads))
    idx = jnp.arange(BLK, dtype=F32)
    rel = idx[:, None] - idx[None, :]
    scale = HEAD_DIM ** -0.5
    decay = jnp.where(rel[None] >= 0,
                      jnp.exp(jnp.maximum(rel, 0.0)[None] * log_g[:, None, None]), 0.0) * scale
    k_dec = jnp.exp((BLK - 1 - idx)[:, None] * log_g[None, :]) * scale
    q_dec = jnp.exp((idx + 1.0)[:, None] * log_g[None, :])
    lane_head = _qk_lane_head()
    k_dec = k_dec[:, lane_head]
    q_dec = q_dec[:, lane_head]
    chunk_decay = jnp.exp(BLK * log_g)
    per_row = chunk_decay[lane_head].reshape(N_PAIRS, LANES)
    row_head = (jnp.arange(LANES) >> 5) & 1
    col_head = jnp.arange(LANES) >> 6
    same = row_head[:, None] == col_head[None, :]
    cdm = jnp.where(same[None], per_row[:, :, None], 0.0)
    return decay, q_dec, k_dec, cdm


def _inproj_mixers(x, g, w, perm, cos, sin, sgu_w, sgu_bias, ln_g, ln_b, l, batch, seq):
    n = x.shape[0]
    tm = ROW_TILE
    nt = n // tm
    decay, q_dec, k_dec, cdm = _retention_tables()
    cur = lambda i: (jnp.minimum(i, nt - 1), 0)
    prev = lambda i: (jnp.maximum(i - 1, 0), 0)
    const2 = lambda i: (0, 0)
    const3 = lambda i: (0, 0, 0)
    return pl.pallas_call(
        functools.partial(_inmix_kernel, tiles_per_batch=seq // tm),
        grid=(nt + 1,),
        in_specs=[pl.BlockSpec((tm, D_MODEL), cur),
                  _layer(g, l), _layer(w, l),
                  pl.BlockSpec(perm.shape, const2),
                  pl.BlockSpec((tm, LANES), prev),
                  pl.BlockSpec((tm, LANES), prev),
                  _layer(sgu_w, l), _layer(sgu_bias, l), _layer(ln_g, l), _layer(ln_b, l),
                  pl.BlockSpec(decay.shape, const3),
                  pl.BlockSpec(q_dec.shape, const2),
                  pl.BlockSpec(k_dec.shape, const2),
                  pl.BlockSpec(cdm.shape, const3)],
        out_specs=[pl.BlockSpec((tm, QKV_A), cur),
                   pl.BlockSpec((tm, B_WIDTH), prev),
                   pl.BlockSpec((tm, C_WIDTH), prev)],
        out_shape=[jax.ShapeDtypeStruct((n, QKV_A), BF16),
                   jax.ShapeDtypeStruct((n, B_WIDTH), BF16),
                   jax.ShapeDtypeStruct((n, C_WIDTH), BF16)],
        scratch_shapes=[pltpu.VMEM((tm, UV_B), F32), pltpu.VMEM((tm, QKVG_C), F32),
                        pltpu.VMEM((tm, UV_B), F32), pltpu.VMEM((tm, QKVG_C), F32),
                        pltpu.VMEM((N_PAIRS, LANES, LANES), F32),
                        pltpu.VMEM((SGU_GROUPS, BLK, BLK), BF16)],
        compiler_params=_params("arbitrary"),
        name="inproj_mixers",
    )(x, g, w, perm, cos, sin, sgu_w, sgu_bias, ln_g, ln_b, decay, q_dec, k_dec, cdm)


def _outproj_kernel(ya_ref, yb_ref, yc_ref, x_ref, g_ref, w_ref, unperm_ref, out_ref):
    tm = ya_ref.shape[0]
    ya = jnp.concatenate([_dot(unperm_ref[...], ya_ref[g * BLK:(g + 1) * BLK, :]).astype(BF16)
                          for g in range(tm // BLK)], axis=0)
    mix = jnp.concatenate([ya, yb_ref[...], yc_ref[...]], axis=1)
    out_ref[...] = x_ref[...] + _rms(_dot(mix, w_ref[...]), g_ref[1:2, :])


def _outproj(ya, yb, yc, x, g, w, unperm, l):
    n = x.shape[0]
    tm = ROW_TILE
    row = lambda i: (i, 0)
    const = lambda i: (0, 0)
    return pl.pallas_call(
        _outproj_kernel,
        grid=(n // tm,),
        in_specs=[pl.BlockSpec((tm, A_WIDTH), row),
                  pl.BlockSpec((tm, B_WIDTH), row),
                  pl.BlockSpec((tm, C_WIDTH), row),
                  pl.BlockSpec((tm, D_MODEL), row),
                  _layer(g, l), _layer(w, l),
                  pl.BlockSpec(unperm.shape, const)],
        out_specs=pl.BlockSpec((tm, D_MODEL), row),
        out_shape=jax.ShapeDtypeStruct((n, D_MODEL), F32),
        compiler_params=_params("parallel"),
        name="outproj",
    )(ya, yb, yc, x, g, w, unperm)


def _kv_kernel(m_ref, g_ref, w_ref, kv_ref):
    h = _rms(m_ref[...], g_ref[6:7, :]).astype(BF16)
    kv_ref[...] = _dot(h, w_ref[...]).astype(BF16)


def _memory_kv(mem, g, w, l):
    n = mem.shape[0]
    return pl.pallas_call(
        _kv_kernel,
        grid=(n // MEM_LEN,),
        in_specs=[pl.BlockSpec((MEM_LEN, D_MODEL), lambda i: (i, 0)),
                  _layer(g, l), _layer(w, l)],
        out_specs=pl.BlockSpec((MEM_LEN, 2 * D_MODEL), lambda i: (i, 0)),
        out_shape=jax.ShapeDtypeStruct((n, 2 * D_MODEL), BF16),
        compiler_params=_params("parallel"),
        name="memory_kv",
    )(mem, g, w)


def _cross_kernel(x_ref, kv_ref, g_ref, wq_ref, wo_ref, out_ref):
    x = x_ref[...]
    h = _rms(x, g_ref[2:3, :]).astype(BF16)
    q = _dot(h, wq_ref[...]).astype(BF16)
    scale = X_HEAD_DIM ** -0.5
    heads = [slice(hd * X_HEAD_DIM, (hd + 1) * X_HEAD_DIM) for hd in range(X_HEADS)]
    scores = [_dot_nt(q[:, cols], kv_ref[:, cols]) for cols in heads]
    outs = []
    for hd, s in enumerate(scores):
        s = s * scale
        m = jnp.max(s, axis=-1, keepdims=True)
        p = jnp.exp(s - m)
        p = p / jnp.sum(p, axis=-1, keepdims=True)
        v = kv_ref[:, D_MODEL + hd * X_HEAD_DIM:D_MODEL + (hd + 1) * X_HEAD_DIM]
        outs.append(_dot(p.astype(BF16), v).astype(BF16))
    acc = _dot(jnp.concatenate(outs, axis=1), wo_ref[...])
    out_ref[...] = x + _rms(acc, g_ref[3:4, :])


def _cross(x, kv, g, wq, wo, l, batch, seq):
    n = x.shape[0]
    tm = ROW_TILE
    per_batch = seq // tm
    row = lambda i: (i, 0)
    const = lambda i: (0, 0)
    return pl.pallas_call(
        _cross_kernel,
        grid=(n // tm,),
        in_specs=[pl.BlockSpec((tm, D_MODEL), row),
                  pl.BlockSpec((MEM_LEN, 2 * D_MODEL), lambda i: (i // per_batch, 0)),
                  _layer(g, l), _layer(wq, l), _layer(wo, l)],
        out_specs=pl.BlockSpec((tm, D_MODEL), row),
        out_shape=jax.ShapeDtypeStruct((n, D_MODEL), F32),
        compiler_params=_params("parallel"),
        name="cross_attention",
    )(x, kv, g, wq, wo)


def _mlp_kernel(x_ref, g_ref, wu_ref, wd_ref, out_ref):
    x = x_ref[...]
    h = _rms(x, g_ref[4:5, :]).astype(BF16)
    acc = jnp.zeros(x.shape, F32)
    for c in range(D_FF // FF_CHUNK):
        cols = slice(c * FF_CHUNK, (c + 1) * FF_CHUNK)
        f = jnp.square(jnp.maximum(_dot(h, wu_ref[:, cols]), 0.0))
        acc += _dot(f.astype(BF16), wd_ref[cols, :])
    out_ref[...] = x + _rms(acc, g_ref[5:6, :])


def _mlp(x, g, w_up, w_down, l):
    n = x.shape[0]
    tm = ROW_TILE
    row = lambda i: (i, 0)
    const = lambda i: (0, 0)
    return pl.pallas_call(
        _mlp_kernel,
        grid=(n // tm,),
        in_specs=[pl.BlockSpec((tm, D_MODEL), row),
                  _layer(g, l),
                  _layer(w_up, l, pipeline_mode=pl.Buffered(1)),
                  _layer(w_down, l, pipeline_mode=pl.Buffered(1))],
        out_specs=pl.BlockSpec((tm, D_MODEL), row),
        out_shape=jax.ShapeDtypeStruct((n, D_MODEL), F32),
        compiler_params=_params("parallel"),
        name="mlp",
    )(x, g, w_up, w_down)


def kernel(x, mem, positions, norm_g, w_in, sgu_w, sgu_b, sgu_ln_g, sgu_ln_b,
           w_out, x_wq, x_wkv, x_wo, w_up, w_down):
    batch, seq, _ = x.shape
    assert x.shape == (batch, seq, D_MODEL) and seq % (BLK * DIL_PATTERNS[-1][1]) == 0
    assert (batch * seq) % ROW_TILE == 0 and seq % ROW_TILE == 0
    assert mem.shape == (batch, MEM_LEN, D_MODEL) and w_in.shape == (DEPTH, D_MODEL, IN_WIDTH)
    xf = x.reshape(batch * seq, D_MODEL)
    memf = mem.reshape(batch * MEM_LEN, D_MODEL)
    cos, sin = _rope_tables(positions)
    perm = _group_permutation()
    w_in_b, w_out_b = _inproj_weight(w_in), w_out.astype(BF16)
    wq_b, wkv_b, wo_b = x_wq.astype(BF16), x_wkv.astype(BF16), x_wo.astype(BF16)
    w_up_b, w_down_b = w_up.astype(BF16), w_down.astype(BF16)
    sgu_bias = jnp.repeat(jnp.swapaxes(sgu_b, 1, 2), HEAD_DIM, axis=2)
    ln_g = sgu_ln_g.reshape(DEPTH, 1, B_WIDTH)
    ln_b = sgu_ln_b.reshape(DEPTH, 1, B_WIDTH)
    for l in range(DEPTH):
        za, yb, yc = _inproj_mixers(xf, norm_g, w_in_b, perm, cos, sin, sgu_w, sgu_bias, ln_g, ln_b,
                                    l, batch, seq)
        ya = _dilated_attention(za, batch, seq)
        xf = _outproj(ya, yb, yc, xf, norm_g, w_out_b, perm.T, l)
        kv = _memory_kv(memf, norm_g, wkv_b, l)
        xf = _cross(xf, kv, norm_g, wq_b, wo_b, l, batch, seq)
        xf = _mlp(xf, norm_g, w_up_b, w_down_b, l)
    return xf.reshape(batch, seq, D_MODEL)
```

```python
import functools

import jax
import jax.numpy as jnp
from jax import lax
from jax.experimental import pallas as pl
from jax.experimental.pallas import tpu as pltpu

F32 = jnp.float32
BF16 = jnp.bfloat16

D_MODEL = 1024
DEPTH = 2
MEM_LEN = 256
HEAD_DIM = 64
DIL_HEADS = 6
DIL_PATTERNS = ((128, 1), (512, 4), (2048, 16))
BLK = 128
SGU_GROUPS = 4
RET_HEADS = 6
ROPE_BASE = 10000.0
A_WIDTH = DIL_HEADS * HEAD_DIM
B_WIDTH = SGU_GROUPS * HEAD_DIM
C_WIDTH = RET_HEADS * HEAD_DIM
QKV_A = 3 * A_WIDTH
UV_B = 2 * B_WIDTH
QKVG_C = 4 * C_WIDTH
IN_WIDTH = QKV_A + UV_B + QKVG_C
X_HEADS = 4
X_HEAD_DIM = D_MODEL // X_HEADS
D_FF = 4 * D_MODEL
EPS = 1e-6

LANES = 128
N_PAIRS = A_WIDTH // LANES
VMEM_LIMIT = 56 * 1024 * 1024

SUPER = BLK * DIL_PATTERNS[-1][1]

ROW_TILE = 512
DIL_UNROLL = (4, 4, 4)
DIL_LOOKAHEAD = (12, 3, 3)
MIX_UNROLL = 2
FF_CHUNK = 1024


def _params(*sem):
    return pltpu.CompilerParams(dimension_semantics=sem, vmem_limit_bytes=VMEM_LIMIT)


def _layer(arr, l, **kw):
    zeros = (0,) * (arr.ndim - 1)
    return pl.BlockSpec((None,) + arr.shape[1:], lambda *_: (l,) + zeros, **kw)


def _rms(x, g):
    ms = jnp.mean(x * x, axis=-1, keepdims=True)
    return x * lax.rsqrt(ms + EPS) * g


def _dot(a, b):
    return jnp.dot(a, b, preferred_element_type=F32)


def _dot_nt(a, b):
    return lax.dot_general(a, b, (((1,), (1,)), ((), ())), preferred_element_type=F32)


def _dot_tn(a, b):
    return lax.dot_general(a, b, (((0,), (0,)), ((), ())), preferred_element_type=F32)


def _lane_lo(shape):
    lane = lax.broadcasted_iota(jnp.int32, shape, len(shape) - 1)
    return (lane & HEAD_DIM) == 0


def _rope_kernel(pos_ref, inv_ref, cos_ref, sin_ref):
    ang = pos_ref[...].astype(F32) * inv_ref[...]
    lane = lax.broadcasted_iota(jnp.int32, ang.shape, 1)
    sign = jnp.where(lane < HEAD_DIM, -1.0, 1.0)
    cos_ref[...] = jnp.cos(ang)
    sin_ref[...] = jnp.sin(ang) * sign


def _rope_tables(positions):
    n = positions.size
    half = HEAD_DIM // 2
    inv = 1.0 / (ROPE_BASE ** jnp.linspace(0.0, 1.0, half, dtype=F32))
    inv = jnp.tile(inv, LANES // half).reshape(1, LANES)
    tm = 1024
    return pl.pallas_call(
        _rope_kernel,
        grid=(n // tm,),
        in_specs=[pl.BlockSpec((tm, 1), lambda i: (i, 0)),
                  pl.BlockSpec((1, LANES), lambda i: (0, 0))],
        out_specs=[pl.BlockSpec((tm, LANES), lambda i: (i, 0))] * 2,
        out_shape=[jax.ShapeDtypeStruct((n, LANES), F32)] * 2,
        compiler_params=_params("parallel"),
        name="rope_tables",
    )(positions.reshape(n, 1), inv)


MAX_DIL = DIL_PATTERNS[-1][1]
SUB = 8


def _group_permutation():
    i = jnp.arange(BLK)
    src = MAX_DIL * (i % SUB) + i // SUB
    return (src[:, None] == jnp.arange(BLK)[None, :]).astype(BF16)


def _run_offsets(d):
    return [BLK * g + SUB * d * c for g in range(d) for c in range(MAX_DIL // d)]


def _block_load(ref, p, base, d):
    if d == 1:
        return ref[p, pl.ds(pl.multiple_of(base, BLK), BLK), :]
    return jnp.concatenate(
        [ref[p, pl.ds(pl.multiple_of(base + off, SUB), SUB), :] for off in _run_offsets(d)], axis=0)


def _block_store(ref, p, base, d, val):
    if d == 1:
        ref[p, pl.ds(pl.multiple_of(base, BLK), BLK), :] = val
        return
    for i, off in enumerate(_run_offsets(d)):
        ref[p, pl.ds(pl.multiple_of(base + off, SUB), SUB), :] = val[i * SUB:(i + 1) * SUB]


def _dil_blocks(refs, branch, d, blocks, first, lookahead):
    bias_ref, qp, kring, vring, acc_o, acc_m, acc_l = refs
    lo = _lane_lo((BLK, LANES))
    units = [(blk, p) for blk in blocks for p in range(N_PAIRS)]

    def scores(unit):
        (qbase, kcbase, kpbase, _), p = unit
        q = _block_load(qp, p, qbase, d).astype(BF16)
        k = jnp.concatenate([_block_load(kring, p, kpbase, d), _block_load(kring, p, kcbase, d)],
                            axis=0).astype(BF16)
        return [_dot_nt(jnp.where(lo if e == 0 else ~lo, q, jnp.zeros_like(q)), k) for e in range(2)]

    def softmax_pv(unit, s_pair):
        (qbase, kcbase, kpbase, has_prev), p = unit
        bias_prev = jnp.where(has_prev, bias_ref[branch, :, :BLK], -jnp.inf)
        bias_cur = bias_ref[branch, :, BLK:]
        v = jnp.concatenate([_block_load(vring, p, kpbase, d), _block_load(vring, p, kcbase, d)],
                            axis=0).astype(BF16)
        o_h, m_h, l_h = [], [], []
        for s in s_pair:
            sp = s[:, :BLK] + bias_prev
            sc = s[:, BLK:] + bias_cur
            m = jnp.max(jnp.maximum(sp, sc), axis=-1, keepdims=True)
            pp = jnp.exp(sp - m)
            pc = jnp.exp(sc - m)
            l_h.append(jnp.sum(pp + pc, axis=-1, keepdims=True))
            m_h.append(m)
            o_h.append(_dot(jnp.concatenate([pp, pc], axis=1).astype(BF16), v))
        return o_h, m_h, l_h

    def merge(unit, stats):
        (qbase, _, _, _), p = unit
        o_h, m_h, l_h = stats
        o_b = jnp.where(lo, o_h[0], o_h[1])
        m_b = jnp.where(lo, m_h[0], m_h[1])
        l_b = jnp.where(lo, l_h[0], l_h[1])
        if first:
            _block_store(acc_o, p, qbase, d, o_b)
            _block_store(acc_m, p, qbase, d, m_b)
            _block_store(acc_l, p, qbase, d, l_b)
        else:
            m_old = _block_load(acc_m, p, qbase, d)
            m_new = jnp.maximum(m_old, m_b)
            a_old = jnp.exp(m_old - m_new)
            a_b = jnp.exp(m_b - m_new)
            _block_store(acc_o, p, qbase, d, _block_load(acc_o, p, qbase, d) * a_old + o_b * a_b)
            _block_store(acc_l, p, qbase, d, _block_load(acc_l, p, qbase, d) * a_old + l_b * a_b)
            _block_store(acc_m, p, qbase, d, m_new)

    ahead = min(lookahead, len(units))
    pending = [scores(u) for u in units[:ahead]]
    stats_prev = None
    for i, unit in enumerate(units):
        if i + ahead < len(units):
            pending.append(scores(units[i + ahead]))
        stats = softmax_pv(unit, pending[i])
        if stats_prev is not None:
            merge(units[i - 1], stats_prev)
        stats_prev = stats
    merge(units[-1], stats_prev)


def _dil_kernel(q_ref, k_ref, v_ref, bias_ref, o_ref, qp, kring, vring, acc_o, acc_m, acc_l):
    t = pl.program_id(1)
    cur = pl.multiple_of((t % 2) * SUPER, SUPER)
    other = pl.multiple_of(SUPER - cur, SUPER)

    @pl.when(t == 0)
    def _():
        kring[:, SUPER:, :] = jnp.zeros((N_PAIRS, SUPER, LANES), F32)
        vring[:, SUPER:, :] = jnp.zeros((N_PAIRS, SUPER, LANES), F32)

    for p in range(N_PAIRS):
        cols = slice(p * LANES, (p + 1) * LANES)
        qp[p] = q_ref[:, cols].astype(F32) * (HEAD_DIM ** -0.5)
        kring[p, pl.ds(cur, SUPER), :] = k_ref[:, cols].astype(F32)
        vring[p, pl.ds(cur, SUPER), :] = v_ref[:, cols].astype(F32)
    refs = (bias_ref, qp, kring, vring, acc_o, acc_m, acc_l)

    for branch, (window, d) in enumerate(DIL_PATTERNS):
        nblk = MAX_DIL // d
        span = BLK * d

        def block_rows(j, d=d, nblk=nblk, span=span):
            if nblk == 1:
                qbase = pl.multiple_of(j * SUB, SUB)
                return qbase, cur + qbase, other + qbase, t > 0
            r = j // nblk
            nb = j % nblk
            qbase = pl.multiple_of(nb * span + r * SUB, SUB)
            kpbase = pl.multiple_of(
                jnp.where(nb == 0, other + (nblk - 1) * span + r * SUB, cur + qbase - span), SUB)
            return qbase, cur + qbase, kpbase, (t > 0) | (nb > 0)

        unroll, lookahead = DIL_UNROLL[branch], DIL_LOOKAHEAD[branch]

        def body(i, carry, branch=branch, d=d, block_rows=block_rows, unroll=unroll, lookahead=lookahead):
            blocks = [block_rows(i * unroll + u) for u in range(unroll)]
            _dil_blocks(refs, branch, d, blocks, first=(branch == 0), lookahead=lookahead)
            return carry

        lax.fori_loop(0, SUPER // BLK // unroll, body, 0)

    def finish(i, carry):
        rows = pl.ds(pl.multiple_of(i * BLK, BLK), BLK)
        for p in range(N_PAIRS):
            o_ref[rows, p * LANES:(p + 1) * LANES] = (acc_o[p, rows, :] / acc_l[p, rows, :]).astype(BF16)
        return carry

    lax.fori_loop(0, SUPER // BLK, finish, 0)


def _dilated_bias():
    out = []
    i = jnp.arange(BLK)
    for _, d in DIL_PATTERNS:
        n_c = MAX_DIL // d
        pos = (BLK // d) * (i // (SUB * n_c)) + n_c * (i % SUB) + (i // SUB) % n_c
        prev_ok = pos[None, :] >= pos[:, None]
        cur_ok = pos[None, :] <= pos[:, None]
        out.append(jnp.where(jnp.concatenate([prev_ok, cur_ok], axis=1), 0.0, -jnp.inf))
    return jnp.stack(out).astype(F32)


def _dilated_attention(qkv, batch, seq):
    n = batch * seq
    tiles = seq // SUPER
    bias = _dilated_bias()
    spec = lambda which: pl.BlockSpec((SUPER, A_WIDTH), lambda b, t: (b * tiles + t, which))
    tile_f32 = pltpu.VMEM((N_PAIRS, SUPER, LANES), F32)
    ring_f32 = pltpu.VMEM((N_PAIRS, 2 * SUPER, LANES), F32)
    return pl.pallas_call(
        _dil_kernel,
        grid=(batch, tiles),
        in_specs=[spec(0), spec(1), spec(2), pl.BlockSpec(bias.shape, lambda b, t: (0, 0, 0))],
        out_specs=pl.BlockSpec((SUPER, A_WIDTH), lambda b, t: (b * tiles + t, 0)),
        out_shape=jax.ShapeDtypeStruct((n, A_WIDTH), BF16),
        scratch_shapes=[tile_f32, ring_f32, ring_f32, tile_f32, tile_f32, tile_f32],
        compiler_params=_params("parallel", "arbitrary"),
        name="dilated_attention",
    )(qkv, qkv, qkv, bias)


def _rotate(t, cos, sin_signed):
    return t * cos + pltpu.roll(t, HEAD_DIM, 1) * sin_signed


PROJ_PIECES = (
    (QKV_A + UV_B, 512), (QKV_A + UV_B + 512, 512), (QKV_A + UV_B + 1024, 512),
    (QKV_A, UV_B),
    (0, 640), (640, QKV_A - 640))


def _inmix_body(x_ref, g_ref, w_ref, perm_ref, cos_ref, sin_ref, sb_ref, lg_ref, lb_ref,
                dec_ref, qdec_ref, kdec_ref, cdm_ref, a_ref, yb_ref, yc_ref,
                zb_w, zc_w, zb_r, zc_r, state_ref, wt_ref):
    lo = _lane_lo((BLK, LANES))
    row = lax.broadcasted_iota(jnp.int32, (BLK, BLK), 0)
    col = lax.broadcasted_iota(jnp.int32, (BLK, BLK), 1)
    same_head = ((row >> 5) & 1) == (col >> 6)
    lane = lax.broadcasted_iota(jnp.int32, (BLK, LANES), 1)
    qk_head0 = (lane & (HEAD_DIM // 2)) == 0
    inv_n = 1.0 / HEAD_DIM

    def sgu_pre(rows):
        u = jax.nn.gelu(zb_r[rows, :B_WIDTH])
        v = jax.nn.gelu(zb_r[rows, B_WIDTH:])
        mu = jnp.mean(v, axis=-1, keepdims=True)
        var = jnp.mean(jnp.square(v - mu), axis=-1, keepdims=True)
        v = (v - mu) * lax.rsqrt(var + EPS) * lg_ref[...] + lb_ref[...]
        return u, v.astype(BF16)

    def sgu_post(rows, pre):
        u, v = pre
        for gp in range(B_WIDTH // LANES):
            cols = slice(gp * LANES, (gp + 1) * LANES)
            s = [_dot(wt_ref[2 * gp + e], v[:, cols]) for e in range(2)]
            gate = jnp.where(lo, s[0], s[1]) + sb_ref[:, cols]
            yb_ref[rows, cols] = (u[:, cols] * gate).astype(BF16)

    def ret_scores(row_list):
        work = [(rows, p) for rows in row_list for p in range(N_PAIRS)]
        out = []
        for rows, p in work:
            cols = slice(p * LANES, (p + 1) * LANES)
            cos, sin = cos_ref[rows, :], sin_ref[rows, :]
            q = _rotate(zc_r[rows, p * LANES:(p + 1) * LANES], cos, sin)
            k = _rotate(zc_r[rows, C_WIDTH + p * LANES:C_WIDTH + (p + 1) * LANES], cos, sin)
            vb = zc_r[rows, 2 * C_WIDTH + p * LANES:2 * C_WIDTH + (p + 1) * LANES].astype(BF16)
            qb, kb = q.astype(BF16), k.astype(BF16)
            scores = [_dot_nt(jnp.where(qk_head0 if e == 0 else ~qk_head0, qb, jnp.zeros_like(qb)), kb)
                      for e in range(2)]
            kv = _dot_tn((k * kdec_ref[:, cols]).astype(BF16), vb)
            out.append((scores, kv, (q * qdec_ref[:, cols]).astype(BF16), vb))
        return work, out

    def ret_values(work, stage1):
        out = []
        for (rows, p), (scores, kv, qd, vb) in zip(work, stage1):
            state = state_ref[p]
            y_x = _dot(qd, state.astype(BF16))
            state_ref[p] = state * cdm_ref[p] + jnp.where(same_head, kv, 0.0)
            ys = [_dot((scores[e] * dec_ref[2 * p + e]).astype(BF16), vb)
                  for e in range(2)]
            out.append((ys, y_x))
        return out

    def ret_norm(work, stage2):
        for (rows, p), (ys, y_x) in zip(work, stage2):
            cols = slice(p * LANES, (p + 1) * LANES)
            y = jnp.where(lo, ys[0], ys[1]) + y_x
            mu_lo = jnp.sum(jnp.where(lo, y, 0.0), axis=-1, keepdims=True) * inv_n
            mu_hi = jnp.sum(jnp.where(lo, 0.0, y), axis=-1, keepdims=True) * inv_n
            yc_ = y - jnp.where(lo, mu_lo, mu_hi)
            sq = yc_ * yc_
            var_lo = jnp.sum(jnp.where(lo, sq, 0.0), axis=-1, keepdims=True) * inv_n
            var_hi = jnp.sum(jnp.where(lo, 0.0, sq), axis=-1, keepdims=True) * inv_n
            yn = yc_ * lax.rsqrt(jnp.where(lo, var_lo, var_hi) + EPS)
            gate = zc_r[rows, 3 * C_WIDTH + p * LANES:3 * C_WIDTH + (p + 1) * LANES]
            yc_ref[rows, cols] = (jax.nn.silu(gate) * yn).astype(BF16)

    def project(lhs, piece):
        first, width = PROJ_PIECES[piece]
        y = _dot(lhs, w_ref[:, first:first + width])
        if first >= QKV_A + UV_B:
            zc_w[:, first - QKV_A - UV_B:first - QKV_A - UV_B + width] = y
        elif first >= QKV_A:
            zb_w[...] = y
        else:
            a_ref[:, first:first + width] = y.astype(BF16)

    tm = x_ref.shape[0]
    chunks = [slice(c * BLK, (c + 1) * BLK) for c in range(tm // BLK)]
    groups = [chunks[g:g + MIX_UNROLL] for g in range(0, len(chunks), MIX_UNROLL)]
    assert len(groups) == 2 and len(PROJ_PIECES) == 6

    pre = [sgu_pre(rows) for rows in groups[0]]
    work0, s1 = ret_scores(groups[0])
    h = _rms(x_ref[...], g_ref[0:1, :]).astype(BF16)
    hp = jnp.concatenate([_dot(perm_ref[...], h[c]).astype(BF16) for c in chunks], axis=0)
    project(h, 0)
    for rows, t in zip(groups[0], pre):
        sgu_post(rows, t)
    project(h, 1)
    s2 = ret_values(work0, s1)
    project(h, 2)
    ret_norm(work0, s2)
    pre = [sgu_pre(rows) for rows in groups[1]]
    work1, s1 = ret_scores(groups[1])
    project(h, 3)
    for rows, t in zip(groups[1], pre):
        sgu_post(rows, t)
    project(hp, 4)
    s2 = ret_values(work1, s1)
    project(hp, 5)
    ret_norm(work1, s2)


def _inmix_kernel(x_ref, g_ref, w_ref, perm_ref, cos_ref, sin_ref, sw_ref, sb_ref, lg_ref, lb_ref,
                  dec_ref, qdec_ref, kdec_ref, cdm_ref, a_ref, yb_ref, yc_ref,
                  zb0, zc0, zb1, zc1, state_ref, wt_ref, *, tiles_per_batch):
    i = pl.program_id(0)

    @pl.when(i == 0)
    def _():
        zb1[...] = jnp.zeros_like(zb1)
        zc1[...] = jnp.zeros_like(zc1)
        row = lax.broadcasted_iota(jnp.int32, (BLK, BLK), 0)
        col = lax.broadcasted_iota(jnp.int32, (BLK, BLK), 1)
        for g in range(SGU_GROUPS):
            wt_ref[g] = jnp.where(row >= col, sw_ref[g], 0.0).astype(BF16)

    @pl.when((i == 0) | ((i - 1) % tiles_per_batch == 0))
    def _():
        state_ref[...] = jnp.zeros_like(state_ref)

    common = (x_ref, g_ref, w_ref, perm_ref, cos_ref, sin_ref, sb_ref, lg_ref, lb_ref,
              dec_ref, qdec_ref, kdec_ref, cdm_ref, a_ref, yb_ref, yc_ref)

    @pl.when(i % 2 == 0)
    def _():
        _inmix_body(*common, zb0, zc0, zb1, zc1, state_ref, wt_ref)

    @pl.when(i % 2 == 1)
    def _():
        _inmix_body(*common, zb1, zc1, zb0, zc0, state_ref, wt_ref)


def _qk_lane_head():
    lane = jnp.arange(C_WIDTH)
    return 2 * (lane // LANES) + ((lane >> 5) & 1)


def _qk_reorder(w):
    lead = w.shape[:-1]
    w = w.reshape(lead + (N_PAIRS, 2, 2, HEAD_DIM // 2))
    return jnp.swapaxes(w, -3, -2).reshape(lead + (C_WIDTH,))


def _inproj_weight(w):
    q0 = QKV_A + UV_B
    w = w.astype(BF16)
    return jnp.concatenate([w[..., :q0], _qk_reorder(w[..., q0:q0 + C_WIDTH]),
                            _qk_reorder(w[..., q0 + C_WIDTH:q0 + 2 * C_WIDTH]),
                            w[..., q0 + 2 * C_WIDTH:]], axis=-1)


def _retention_tables():
    heads = jnp.arange(RET_HEADS, dtype=F32)
    log_g = jnp.log1p(-jnp.power(2.0, -5.0 - heads))
    idx = jnp.arange(BLK, dtype=F32)
    rel = idx[:, None] - idx[None, :]
    scale = HEAD_DIM ** -0.5
    decay = jnp.where(rel[None] >= 0,
                      jnp.exp(jnp.maximum(rel, 0.0)[None] * log_g[:, None, None]), 0.0) * scale
    k_dec = jnp.exp((BLK - 1 - idx)[:, None] * log_g[None, :]) * scale
    q_dec = jnp.exp((idx + 1.0)[:, None] * log_g[None, :])
    lane_head = _qk_lane_head()
    k_dec = k_dec[:, lane_head]
    q_dec = q_dec[:, lane_head]
    chunk_decay = jnp.exp(BLK * log_g)
    per_row = chunk_decay[lane_head].reshape(N_PAIRS, LANES)
    row_head = (jnp.arange(LANES) >> 5) & 1
    col_head = jnp.arange(LANES) >> 6
    same = row_head[:, None] == col_head[None, :]
    cdm = jnp.where(same[None], per_row[:, :, None], 0.0)
    return decay, q_dec, k_dec, cdm


def _inproj_mixers(x, g, w, perm, cos, sin, sgu_w, sgu_bias, ln_g, ln_b, l, batch, seq):
    n = x.shape[0]
    tm = ROW_TILE
    nt = n // tm
    decay, q_dec, k_dec, cdm = _retention_tables()
    cur = lambda i: (jnp.minimum(i, nt - 1), 0)
    prev = lambda i: (jnp.maximum(i - 1, 0), 0)
    const2 = lambda i: (0, 0)
    const3 = lambda i: (0, 0, 0)
    return pl.pallas_call(
        functools.partial(_inmix_kernel, tiles_per_batch=seq // tm),
        grid=(nt + 1,),
        in_specs=[pl.BlockSpec((tm, D_MODEL), cur),
                  _layer(g, l), _layer(w, l),
                  pl.BlockSpec(perm.shape, const2),
                  pl.BlockSpec((tm, LANES), prev),
                  pl.BlockSpec((tm, LANES), prev),
                  _layer(sgu_w, l), _layer(sgu_bias, l), _layer(ln_g, l), _layer(ln_b, l),
                  pl.BlockSpec(decay.shape, const3),
                  pl.BlockSpec(q_dec.shape, const2),
                  pl.BlockSpec(k_dec.shape, const2),
                  pl.BlockSpec(cdm.shape, const3)],
        out_specs=[pl.BlockSpec((tm, QKV_A), cur),
                   pl.BlockSpec((tm, B_WIDTH), prev),
                   pl.BlockSpec((tm, C_WIDTH), prev)],
        out_shape=[jax.ShapeDtypeStruct((n, QKV_A), BF16),
                   jax.ShapeDtypeStruct((n, B_WIDTH), BF16),
                   jax.ShapeDtypeStruct((n, C_WIDTH), BF16)],
        scratch_shapes=[pltpu.VMEM((tm, UV_B), F32), pltpu.VMEM((tm, QKVG_C), F32),
                        pltpu.VMEM((tm, UV_B), F32), pltpu.VMEM((tm, QKVG_C), F32),
                        pltpu.VMEM((N_PAIRS, LANES, LANES), F32),
                        pltpu.VMEM((SGU_GROUPS, BLK, BLK), BF16)],
        compiler_params=_params("arbitrary"),
        name="inproj_mixers",
    )(x, g, w, perm, cos, sin, sgu_w, sgu_bias, ln_g, ln_b, decay, q_dec, k_dec, cdm)


def _kv_kernel(m_ref, g_ref, w_ref, kv_ref):
    h = _rms(m_ref[...], g_ref[6:7, :]).astype(BF16)
    kv_ref[...] = _dot(h, w_ref[...]).astype(BF16)


def _memory_kv(mem, g, w, l):
    n = mem.shape[0]
    return pl.pallas_call(
        _kv_kernel,
        grid=(n // MEM_LEN,),
        in_specs=[pl.BlockSpec((MEM_LEN, D_MODEL), lambda i: (i, 0)),
                  _layer(g, l), _layer(w, l)],
        out_specs=pl.BlockSpec((MEM_LEN, 2 * D_MODEL), lambda i: (i, 0)),
        out_shape=jax.ShapeDtypeStruct((n, 2 * D_MODEL), BF16),
        compiler_params=_params("parallel"),
        name="memory_kv",
    )(mem, g, w)


def _post_mix_kernel(ya_ref, yb_ref, yc_ref, x_ref, kv_ref, g_ref, w_ref, unperm_ref, wq_ref, wo_ref,
                     out_ref):
    tm = x_ref.shape[0]
    halves = [slice(i * (tm // 2), (i + 1) * (tm // 2)) for i in range(2)]
    heads = [slice(hd * X_HEAD_DIM, (hd + 1) * X_HEAD_DIM) for hd in range(X_HEADS)]
    scale = X_HEAD_DIM ** -0.5

    def unpermute(rows):
        n_grp = (rows.stop - rows.start) // BLK
        return jnp.concatenate(
            [_dot(unperm_ref[...], ya_ref[rows.start + g * BLK:rows.start + (g + 1) * BLK, :]).astype(BF16)
             for g in range(n_grp)], axis=0)

    ya = [unpermute(rows) for rows in halves]
    proj = [_dot(jnp.concatenate([ya[i], yb_ref[rows, :], yc_ref[rows, :]], axis=1), w_ref[...])
            for i, rows in enumerate(halves)]
    x1 = [x_ref[rows, :] + _rms(proj[i], g_ref[1:2, :]) for i, rows in enumerate(halves)]
    q = [_dot(_rms(x1[i], g_ref[2:3, :]).astype(BF16), wq_ref[...]).astype(BF16) for i in range(2)]
    scores = [[_dot_nt(q[i][:, cols], kv_ref[:, cols]) for cols in heads] for i in range(2)]
    att = []
    for i in range(2):
        outs = []
        for hd, s in enumerate(scores[i]):
            s = s * scale
            m = jnp.max(s, axis=-1, keepdims=True)
            p = jnp.exp(s - m)
            p = p / jnp.sum(p, axis=-1, keepdims=True)
            v = kv_ref[:, D_MODEL + hd * X_HEAD_DIM:D_MODEL + (hd + 1) * X_HEAD_DIM]
            outs.append(_dot(p.astype(BF16), v).astype(BF16))
        att.append(jnp.concatenate(outs, axis=1))
    acc = [_dot(att[i], wo_ref[...]) for i in range(2)]
    for i, rows in enumerate(halves):
        out_ref[rows, :] = x1[i] + _rms(acc[i], g_ref[3:4, :])


def _post_mix(ya, yb, yc, x, kv, g, w_out, unperm, wq, wo, l, batch, seq):
    n = x.shape[0]
    tm = ROW_TILE
    per_batch = seq // tm
    row = lambda i: (i, 0)
    const = lambda i: (0, 0)
    return pl.pallas_call(
        _post_mix_kernel,
        grid=(n // tm,),
        in_specs=[pl.BlockSpec((tm, A_WIDTH), row),
                  pl.BlockSpec((tm, B_WIDTH), row),
                  pl.BlockSpec((tm, C_WIDTH), row),
                  pl.BlockSpec((tm, D_MODEL), row),
                  pl.BlockSpec((MEM_LEN, 2 * D_MODEL), lambda i: (i // per_batch, 0)),
                  _layer(g, l), _layer(w_out, l),
                  pl.BlockSpec(unperm.shape, const),
                  _layer(wq, l), _layer(wo, l)],
        out_specs=pl.BlockSpec((tm, D_MODEL), row),
        out_shape=jax.ShapeDtypeStruct((n, D_MODEL), F32),
        compiler_params=_params("parallel"),
        name="outproj_cross",
    )(ya, yb, yc, x, kv, g, w_out, unperm, wq, wo)


def _mlp_kernel(x_ref, g_ref, wu_ref, wd_ref, out_ref):
    x = x_ref[...]
    h = _rms(x, g_ref[4:5, :]).astype(BF16)
    acc = jnp.zeros(x.shape, F32)
    for c in range(D_FF // FF_CHUNK):
        cols = slice(c * FF_CHUNK, (c + 1) * FF_CHUNK)
        f = jnp.square(jnp.maximum(_dot(h, wu_ref[:, cols]), 0.0))
        acc += _dot(f.astype(BF16), wd_ref[cols, :])
    out_ref[...] = x + _rms(acc, g_ref[5:6, :])


def _mlp(x, g, w_up, w_down, l):
    n = x.shape[0]
    tm = ROW_TILE
    row = lambda i: (i, 0)
    const = lambda i: (0, 0)
    return pl.pallas_call(
        _mlp_kernel,
        grid=(n // tm,),
        in_specs=[pl.BlockSpec((tm, D_MODEL), row),
                  _layer(g, l),
                  _layer(w_up, l, pipeline_mode=pl.Buffered(1)),
                  _layer(w_down, l, pipeline_mode=pl.Buffered(1))],
        out_specs=pl.BlockSpec((tm, D_MODEL), row),
        out_shape=jax.ShapeDtypeStruct((n, D_MODEL), F32),
        compiler_params=_params("parallel"),
        name="mlp",
    )(x, g, w_up, w_down)


def kernel(x, mem, positions, norm_g, w_in, sgu_w, sgu_b, sgu_ln_g, sgu_ln_b,
           w_out, x_wq, x_wkv, x_wo, w_up, w_down):
    batch, seq, _ = x.shape
    assert x.shape == (batch, seq, D_MODEL) and seq % (BLK * DIL_PATTERNS[-1][1]) == 0
    assert (batch * seq) % ROW_TILE == 0 and seq % ROW_TILE == 0
    assert mem.shape == (batch, MEM_LEN, D_MODEL) and w_in.shape == (DEPTH, D_MODEL, IN_WIDTH)
    xf = x.reshape(batch * seq, D_MODEL)
    memf = mem.reshape(batch * MEM_LEN, D_MODEL)
    cos, sin = _rope_tables(positions)
    perm = _group_permutation()
    w_in_b, w_out_b = _inproj_weight(w_in), w_out.astype(BF16)
    wq_b, wkv_b, wo_b = x_wq.astype(BF16), x_wkv.astype(BF16), x_wo.astype(BF16)
    w_up_b, w_down_b = w_up.astype(BF16), w_down.astype(BF16)
    sgu_bias = jnp.repeat(jnp.swapaxes(sgu_b, 1, 2), HEAD_DIM, axis=2)
    ln_g = sgu_ln_g.reshape(DEPTH, 1, B_WIDTH)
    ln_b = sgu_ln_b.reshape(DEPTH, 1, B_WIDTH)
    for l in range(DEPTH):
        za, yb, yc = _inproj_mixers(xf, norm_g, w_in_b, perm, cos, sin, sgu_w, sgu_bias, ln_g, ln_b,
                                    l, batch, seq)
        ya = _dilated_attention(za, batch, seq)
        kv = _memory_kv(memf, norm_g, wkv_b, l)
        xf = _post_mix(ya, yb, yc, xf, kv, norm_g, w_out_b, perm.T, wq_b, wo_b, l, batch, seq)
        xf = _mlp(xf, norm_g, w_up_b, w_down_b, l)
    return xf.reshape(batch, seq, D_MODEL)
```

```python
import functools

import jax
import jax.numpy as jnp
from jax import lax
from jax.experimental import pallas as pl
from jax.experimental.pallas import tpu as pltpu

F32 = jnp.float32
BF16 = jnp.bfloat16

D_MODEL = 1024
DEPTH = 2
MEM_LEN = 256
HEAD_DIM = 64
DIL_HEADS = 6
DIL_PATTERNS = ((128, 1), (512, 4), (2048, 16))
BLK = 128
SGU_GROUPS = 4
RET_HEADS = 6
ROPE_BASE = 10000.0
A_WIDTH = DIL_HEADS * HEAD_DIM
B_WIDTH = SGU_GROUPS * HEAD_DIM
C_WIDTH = RET_HEADS * HEAD_DIM
QKV_A = 3 * A_WIDTH
UV_B = 2 * B_WIDTH
QKVG_C = 4 * C_WIDTH
IN_WIDTH = QKV_A + UV_B + QKVG_C
X_HEADS = 4
X_HEAD_DIM = D_MODEL // X_HEADS
D_FF = 4 * D_MODEL
EPS = 1e-6

LANES = 128
N_PAIRS = A_WIDTH // LANES
VMEM_LIMIT = 56 * 1024 * 1024

SUPER = BLK * DIL_PATTERNS[-1][1]

ROW_TILE = 512
DIL_UNROLL = (4, 4, 4)
DIL_LOOKAHEAD = (12, 3, 3)
MIX_UNROLL = 2
FF_CHUNK = 1024


def _params(*sem):
    return pltpu.CompilerParams(dimension_semantics=sem, vmem_limit_bytes=VMEM_LIMIT)


def _layer(arr, l, **kw):
    zeros = (0,) * (arr.ndim - 1)
    return pl.BlockSpec((None,) + arr.shape[1:], lambda *_: (l,) + zeros, **kw)


def _rms(x, g):
    ms = jnp.mean(x * x, axis=-1, keepdims=True)
    return x * lax.rsqrt(ms + EPS) * g


def _dot(a, b):
    return jnp.dot(a, b, preferred_element_type=F32)


def _dot_nt(a, b):
    return lax.dot_general(a, b, (((1,), (1,)), ((), ())), preferred_element_type=F32)


def _dot_tn(a, b):
    return lax.dot_general(a, b, (((0,), (0,)), ((), ())), preferred_element_type=F32)


def _lane_lo(shape):
    lane = lax.broadcasted_iota(jnp.int32, shape, len(shape) - 1)
    return (lane & HEAD_DIM) == 0


def _rope_kernel(pos_ref, inv_ref, cos_ref, sin_ref):
    ang = pos_ref[...].astype(F32) * inv_ref[...]
    lane = lax.broadcasted_iota(jnp.int32, ang.shape, 1)
    sign = jnp.where(lane < HEAD_DIM, -1.0, 1.0)
    cos_ref[...] = jnp.cos(ang)
    sin_ref[...] = jnp.sin(ang) * sign


def _rope_tables(positions):
    n = positions.size
    half = HEAD_DIM // 2
    inv = 1.0 / (ROPE_BASE ** jnp.linspace(0.0, 1.0, half, dtype=F32))
    inv = jnp.tile(inv, LANES // half).reshape(1, LANES)
    tm = 1024
    return pl.pallas_call(
        _rope_kernel,
        grid=(n // tm,),
        in_specs=[pl.BlockSpec((tm, 1), lambda i: (i, 0)),
                  pl.BlockSpec((1, LANES), lambda i: (0, 0))],
        out_specs=[pl.BlockSpec((tm, LANES), lambda i: (i, 0))] * 2,
        out_shape=[jax.ShapeDtypeStruct((n, LANES), F32)] * 2,
        compiler_params=_params("parallel"),
        name="rope_tables",
    )(positions.reshape(n, 1), inv)


MAX_DIL = DIL_PATTERNS[-1][1]
SUB = 8


def _group_permutation():
    i = jnp.arange(BLK)
    src = MAX_DIL * (i % SUB) + i // SUB
    return (src[:, None] == jnp.arange(BLK)[None, :]).astype(BF16)


def _run_offsets(d):
    return [BLK * g + SUB * d * c for g in range(d) for c in range(MAX_DIL // d)]


def _block_load(ref, p, base, d):
    if d == 1:
        return ref[p, pl.ds(pl.multiple_of(base, BLK), BLK), :]
    return jnp.concatenate(
        [ref[p, pl.ds(pl.multiple_of(base + off, SUB), SUB), :] for off in _run_offsets(d)], axis=0)


def _block_store(ref, p, base, d, val):
    if d == 1:
        ref[p, pl.ds(pl.multiple_of(base, BLK), BLK), :] = val
        return
    for i, off in enumerate(_run_offsets(d)):
        ref[p, pl.ds(pl.multiple_of(base + off, SUB), SUB), :] = val[i * SUB:(i + 1) * SUB]


def _dil_blocks(refs, branch, d, blocks, first, lookahead):
    bias_ref, qp, kring, vring, acc_o, acc_m, acc_l = refs
    lo = _lane_lo((BLK, LANES))
    units = [(blk, p) for blk in blocks for p in range(N_PAIRS)]

    def scores(unit):
        (qbase, kcbase, kpbase, _), p = unit
        q = _block_load(qp, p, qbase, d).astype(BF16)
        k = jnp.concatenate([_block_load(kring, p, kpbase, d), _block_load(kring, p, kcbase, d)],
                            axis=0).astype(BF16)
        return [_dot_nt(jnp.where(lo if e == 0 else ~lo, q, jnp.zeros_like(q)), k) for e in range(2)]

    def softmax_pv(unit, s_pair):
        (qbase, kcbase, kpbase, has_prev), p = unit
        bias_prev = jnp.where(has_prev, bias_ref[branch, :, :BLK], -jnp.inf)
        bias_cur = bias_ref[branch, :, BLK:]
        v = jnp.concatenate([_block_load(vring, p, kpbase, d), _block_load(vring, p, kcbase, d)],
                            axis=0).astype(BF16)
        o_h, m_h, l_h = [], [], []
        for s in s_pair:
            sp = s[:, :BLK] + bias_prev
            sc = s[:, BLK:] + bias_cur
            m = jnp.max(jnp.maximum(sp, sc), axis=-1, keepdims=True)
            pp = jnp.exp(sp - m)
            pc = jnp.exp(sc - m)
            l_h.append(jnp.sum(pp + pc, axis=-1, keepdims=True))
            m_h.append(m)
            o_h.append(_dot(jnp.concatenate([pp, pc], axis=1).astype(BF16), v))
        return o_h, m_h, l_h

    def merge(unit, stats):
        (qbase, _, _, _), p = unit
        o_h, m_h, l_h = stats
        o_b = jnp.where(lo, o_h[0], o_h[1])
        m_b = jnp.where(lo, m_h[0], m_h[1])
        l_b = jnp.where(lo, l_h[0], l_h[1])
        if first:
            _block_store(acc_o, p, qbase, d, o_b)
            _block_store(acc_m, p, qbase, d, m_b)
            _block_store(acc_l, p, qbase, d, l_b)
        else:
            m_old = _block_load(acc_m, p, qbase, d)
            m_new = jnp.maximum(m_old, m_b)
            a_old = jnp.exp(m_old - m_new)
            a_b = jnp.exp(m_b - m_new)
            _block_store(acc_o, p, qbase, d, _block_load(acc_o, p, qbase, d) * a_old + o_b * a_b)
            _block_store(acc_l, p, qbase, d, _block_load(acc_l, p, qbase, d) * a_old + l_b * a_b)
            _block_store(acc_m, p, qbase, d, m_new)

    ahead = min(lookahead, len(units))
    pending = [scores(u) for u in units[:ahead]]
    stats_prev = None
    for i, unit in enumerate(units):
        if i + ahead < len(units):
            pending.append(scores(units[i + ahead]))
        stats = softmax_pv(unit, pending[i])
        if stats_prev is not None:
            merge(units[i - 1], stats_prev)
        stats_prev = stats
    merge(units[-1], stats_prev)


def _dil_kernel(q_ref, k_ref, v_ref, bias_ref, o_ref, qp, kring, vring, acc_o, acc_m, acc_l):
    t = pl.program_id(1)
    cur = pl.multiple_of((t % 2) * SUPER, SUPER)
    other = pl.multiple_of(SUPER - cur, SUPER)

    @pl.when(t == 0)
    def _():
        kring[:, SUPER:, :] = jnp.zeros((N_PAIRS, SUPER, LANES), F32)
        vring[:, SUPER:, :] = jnp.zeros((N_PAIRS, SUPER, LANES), F32)

    for p in range(N_PAIRS):
        cols = slice(p * LANES, (p + 1) * LANES)
        qp[p] = q_ref[:, cols].astype(F32) * (HEAD_DIM ** -0.5)
        kring[p, pl.ds(cur, SUPER), :] = k_ref[:, cols].astype(F32)
        vring[p, pl.ds(cur, SUPER), :] = v_ref[:, cols].astype(F32)
    refs = (bias_ref, qp, kring, vring, acc_o, acc_m, acc_l)

    for branch, (window, d) in enumerate(DIL_PATTERNS):
        nblk = MAX_DIL // d
        span = BLK * d

        def block_rows(j, d=d, nblk=nblk, span=span):
            if nblk == 1:
                qbase = pl.multiple_of(j * SUB, SUB)
                return qbase, cur + qbase, other + qbase, t > 0
            r = j // nblk
            nb = j % nblk
            qbase = pl.multiple_of(nb * span + r * SUB, SUB)
            kpbase = pl.multiple_of(
                jnp.where(nb == 0, other + (nblk - 1) * span + r * SUB, cur + qbase - span), SUB)
            return qbase, cur + qbase, kpbase, (t > 0) | (nb > 0)

        unroll, lookahead = DIL_UNROLL[branch], DIL_LOOKAHEAD[branch]

        def body(i, carry, branch=branch, d=d, block_rows=block_rows, unroll=unroll, lookahead=lookahead):
            blocks = [block_rows(i * unroll + u) for u in range(unroll)]
            _dil_blocks(refs, branch, d, blocks, first=(branch == 0), lookahead=lookahead)
            return carry

        lax.fori_loop(0, SUPER // BLK // unroll, body, 0)

    def finish(i, carry):
        rows = pl.ds(pl.multiple_of(i * BLK, BLK), BLK)
        for p in range(N_PAIRS):
            o_ref[rows, p * LANES:(p + 1) * LANES] = (acc_o[p, rows, :] / acc_l[p, rows, :]).astype(BF16)
        return carry

    lax.fori_loop(0, SUPER // BLK, finish, 0)


def _dilated_bias():
    out = []
    i = jnp.arange(BLK)
    for _, d in DIL_PATTERNS:
        n_c = MAX_DIL // d
        pos = (BLK // d) * (i // (SUB * n_c)) + n_c * (i % SUB) + (i // SUB) % n_c
        prev_ok = pos[None, :] >= pos[:, None]
        cur_ok = pos[None, :] <= pos[:, None]
        out.append(jnp.where(jnp.concatenate([prev_ok, cur_ok], axis=1), 0.0, -jnp.inf))
    return jnp.stack(out).astype(F32)


def _dilated_attention(qkv, batch, seq):
    n = batch * seq
    tiles = seq // SUPER
    bias = _dilated_bias()
    spec = lambda which: pl.BlockSpec((SUPER, A_WIDTH), lambda b, t: (b * tiles + t, which))
    tile_f32 = pltpu.VMEM((N_PAIRS, SUPER, LANES), F32)
    ring_f32 = pltpu.VMEM((N_PAIRS, 2 * SUPER, LANES), F32)
    return pl.pallas_call(
        _dil_kernel,
        grid=(batch, tiles),
        in_specs=[spec(0), spec(1), spec(2), pl.BlockSpec(bias.shape, lambda b, t: (0, 0, 0))],
        out_specs=pl.BlockSpec((SUPER, A_WIDTH), lambda b, t: (b * tiles + t, 0)),
        out_shape=jax.ShapeDtypeStruct((n, A_WIDTH), BF16),
        scratch_shapes=[tile_f32, ring_f32, ring_f32, tile_f32, tile_f32, tile_f32],
        compiler_params=_params("parallel", "arbitrary"),
        name="dilated_attention",
    )(qkv, qkv, qkv, bias)


def _rotate(t, cos, sin_signed):
    return t * cos + pltpu.roll(t, HEAD_DIM, 1) * sin_signed


PROJ_PIECES = (
    (QKV_A + UV_B, 512), (QKV_A + UV_B + 512, 512), (QKV_A + UV_B + 1024, 512),
    (QKV_A, UV_B),
    (0, 640), (640, QKV_A - 640))


def _inmix_body(x_ref, g_ref, w_ref, perm_ref, cos_ref, sin_ref, sb_ref, lg_ref, lb_ref,
                dec_ref, qdec_ref, kdec_ref, cdm_ref, a_ref, yb_ref, yc_ref,
                zb_w, zc_w, zb_r, zc_r, state_ref, wt_ref):
    lo = _lane_lo((BLK, LANES))
    row = lax.broadcasted_iota(jnp.int32, (BLK, BLK), 0)
    col = lax.broadcasted_iota(jnp.int32, (BLK, BLK), 1)
    same_head = ((row >> 5) & 1) == (col >> 6)
    lane = lax.broadcasted_iota(jnp.int32, (BLK, LANES), 1)
    qk_head0 = (lane & (HEAD_DIM // 2)) == 0
    inv_n = 1.0 / HEAD_DIM

    def sgu_pre(rows):
        u = jax.nn.gelu(zb_r[rows, :B_WIDTH])
        v = jax.nn.gelu(zb_r[rows, B_WIDTH:])
        mu = jnp.mean(v, axis=-1, keepdims=True)
        var = jnp.mean(jnp.square(v - mu), axis=-1, keepdims=True)
        v = (v - mu) * lax.rsqrt(var + EPS) * lg_ref[...] + lb_ref[...]
        return u, v.astype(BF16)

    def sgu_post(rows, pre):
        u, v = pre
        for gp in range(B_WIDTH // LANES):
            cols = slice(gp * LANES, (gp + 1) * LANES)
            vp = v[:, cols]
            v2 = jnp.concatenate([jnp.where(lo, vp, jnp.zeros_like(vp)),
                                  jnp.where(lo, jnp.zeros_like(vp), vp)], axis=0)
            gate = _dot(wt_ref[gp], v2) + sb_ref[:, cols]
            yb_ref[rows, cols] = (u[:, cols] * gate).astype(BF16)

    def ret_scores(row_list):
        work = [(rows, p) for rows in row_list for p in range(N_PAIRS)]
        out = []
        for rows, p in work:
            cols = slice(p * LANES, (p + 1) * LANES)
            cos, sin = cos_ref[rows, :], sin_ref[rows, :]
            q = _rotate(zc_r[rows, p * LANES:(p + 1) * LANES], cos, sin)
            k = _rotate(zc_r[rows, C_WIDTH + p * LANES:C_WIDTH + (p + 1) * LANES], cos, sin)
            vb = zc_r[rows, 2 * C_WIDTH + p * LANES:2 * C_WIDTH + (p + 1) * LANES].astype(BF16)
            qb, kb = q.astype(BF16), k.astype(BF16)
            k2 = jnp.concatenate([jnp.where(qk_head0, kb, jnp.zeros_like(kb)),
                                  jnp.where(qk_head0, jnp.zeros_like(kb), kb)], axis=0)
            scores = _dot_nt(qb, k2)
            kv = _dot_tn((k * kdec_ref[:, cols]).astype(BF16), vb)
            out.append((scores, kv, (q * qdec_ref[:, cols]).astype(BF16), vb))
        return work, out

    def ret_values(work, stage1):
        out = []
        for (rows, p), (scores, kv, qd, vb) in zip(work, stage1):
            state = state_ref[p]
            y_x = _dot(qd, state.astype(BF16))
            state_ref[p] = state * cdm_ref[p] + jnp.where(same_head, kv, 0.0)
            v2 = jnp.concatenate([jnp.where(lo, vb, jnp.zeros_like(vb)),
                                  jnp.where(lo, jnp.zeros_like(vb), vb)], axis=0)
            y_in = _dot((scores * dec_ref[p]).astype(BF16), v2)
            out.append((y_in, y_x))
        return out

    def ret_norm(work, stage2):
        for (rows, p), (y_in, y_x) in zip(work, stage2):
            cols = slice(p * LANES, (p + 1) * LANES)
            y = y_in + y_x
            mu_lo = jnp.sum(jnp.where(lo, y, 0.0), axis=-1, keepdims=True) * inv_n
            mu_hi = jnp.sum(jnp.where(lo, 0.0, y), axis=-1, keepdims=True) * inv_n
            yc_ = y - jnp.where(lo, mu_lo, mu_hi)
            sq = yc_ * yc_
            var_lo = jnp.sum(jnp.where(lo, sq, 0.0), axis=-1, keepdims=True) * inv_n
            var_hi = jnp.sum(jnp.where(lo, 0.0, sq), axis=-1, keepdims=True) * inv_n
            yn = yc_ * lax.rsqrt(jnp.where(lo, var_lo, var_hi) + EPS)
            gate = zc_r[rows, 3 * C_WIDTH + p * LANES:3 * C_WIDTH + (p + 1) * LANES]
            yc_ref[rows, cols] = (jax.nn.silu(gate) * yn).astype(BF16)

    def project(lhs, piece):
        first, width = PROJ_PIECES[piece]
        y = _dot(lhs, w_ref[:, first:first + width])
        if first >= QKV_A + UV_B:
            zc_w[:, first - QKV_A - UV_B:first - QKV_A - UV_B + width] = y
        elif first >= QKV_A:
            zb_w[...] = y
        else:
            a_ref[:, first:first + width] = y.astype(BF16)

    tm = x_ref.shape[0]
    chunks = [slice(c * BLK, (c + 1) * BLK) for c in range(tm // BLK)]
    groups = [chunks[g:g + MIX_UNROLL] for g in range(0, len(chunks), MIX_UNROLL)]
    assert len(groups) == 2 and len(PROJ_PIECES) == 6

    h = _rms(x_ref[...], g_ref[0:1, :]).astype(BF16)
    hp = jnp.concatenate([_dot(perm_ref[...], h[c]).astype(BF16) for c in chunks], axis=0)
    project(h, 0)
    pre = [sgu_pre(rows) for rows in groups[0]]
    work0, s1 = ret_scores(groups[0])
    project(h, 1)
    for rows, t in zip(groups[0], pre):
        sgu_post(rows, t)
    s2 = ret_values(work0, s1)
    project(h, 2)
    ret_norm(work0, s2)
    pre = [sgu_pre(rows) for rows in groups[1]]
    work1, s1 = ret_scores(groups[1])
    project(h, 3)
    for rows, t in zip(groups[1], pre):
        sgu_post(rows, t)
    s2 = ret_values(work1, s1)
    project(hp, 4)
    ret_norm(work1, s2)
    project(hp, 5)


def _inmix_kernel(x_ref, g_ref, w_ref, perm_ref, cos_ref, sin_ref, sw_ref, sb_ref, lg_ref, lb_ref,
                  dec_ref, qdec_ref, kdec_ref, cdm_ref, a_ref, yb_ref, yc_ref,
                  zb0, zc0, zb1, zc1, state_ref, wt_ref, *, tiles_per_batch):
    i = pl.program_id(0)

    @pl.when(i == 0)
    def _():
        zb1[...] = jnp.zeros_like(zb1)
        zc1[...] = jnp.zeros_like(zc1)
        row = lax.broadcasted_iota(jnp.int32, (BLK, BLK), 0)
        col = lax.broadcasted_iota(jnp.int32, (BLK, BLK), 1)
        for gp in range(SGU_GROUPS // 2):
            wt_ref[gp] = jnp.concatenate([jnp.where(row >= col, sw_ref[2 * gp + e], 0.0)
                                          for e in range(2)], axis=1).astype(BF16)

    @pl.when((i == 0) | ((i - 1) % tiles_per_batch == 0))
    def _():
        state_ref[...] = jnp.zeros_like(state_ref)

    common = (x_ref, g_ref, w_ref, perm_ref, cos_ref, sin_ref, sb_ref, lg_ref, lb_ref,
              dec_ref, qdec_ref, kdec_ref, cdm_ref, a_ref, yb_ref, yc_ref)

    @pl.when(i % 2 == 0)
    def _():
        _inmix_body(*common, zb0, zc0, zb1, zc1, state_ref, wt_ref)

    @pl.when(i % 2 == 1)
    def _():
        _inmix_body(*common, zb1, zc1, zb0, zc0, state_ref, wt_ref)


def _qk_lane_head():
    lane = jnp.arange(C_WIDTH)
    return 2 * (lane // LANES) + ((lane >> 5) & 1)


def _qk_reorder(w):
    lead = w.shape[:-1]
    w = w.reshape(lead + (N_PAIRS, 2, 2, HEAD_DIM // 2))
    return jnp.swapaxes(w, -3, -2).reshape(lead + (C_WIDTH,))


def _inproj_weight(w):
    q0 = QKV_A + UV_B
    w = w.astype(BF16)
    return jnp.concatenate([w[..., :q0], _qk_reorder(w[..., q0:q0 + C_WIDTH]),
                            _qk_reorder(w[..., q0 + C_WIDTH:q0 + 2 * C_WIDTH]),
                            w[..., q0 + 2 * C_WIDTH:]], axis=-1)


def _retention_tables():
    heads = jnp.arange(RET_HEADS, dtype=F32)
    log_g = jnp.log1p(-jnp.power(2.0, -5.0 - heads))
    idx = jnp.arange(BLK, dtype=F32)
    rel = idx[:, None] - idx[None, :]
    scale = HEAD_DIM ** -0.5
    decay = jnp.where(rel[None] >= 0,
                      jnp.exp(jnp.maximum(rel, 0.0)[None] * log_g[:, None, None]), 0.0) * scale
    k_dec = jnp.exp((BLK - 1 - idx)[:, None] * log_g[None, :]) * scale
    q_dec = jnp.exp((idx + 1.0)[:, None] * log_g[None, :])
    lane_head = _qk_lane_head()
    k_dec = k_dec[:, lane_head]
    q_dec = q_dec[:, lane_head]
    chunk_decay = jnp.exp(BLK * log_g)
    per_row = chunk_decay[lane_head].reshape(N_PAIRS, LANES)
    row_head = (jnp.arange(LANES) >> 5) & 1
    col_head = jnp.arange(LANES) >> 6
    same = row_head[:, None] == col_head[None, :]
    cdm = jnp.where(same[None], per_row[:, :, None], 0.0)
    decay = jnp.concatenate([decay[0::2], decay[1::2]], axis=2)
    return decay, q_dec, k_dec, cdm


def _inproj_mixers(x, g, w, perm, cos, sin, sgu_w, sgu_bias, ln_g, ln_b, l, batch, seq):
    n = x.shape[0]
    tm = ROW_TILE
    nt = n // tm
    decay, q_dec, k_dec, cdm = _retention_tables()
    cur = lambda i: (jnp.minimum(i, nt - 1), 0)
    prev = lambda i: (jnp.maximum(i - 1, 0), 0)
    const2 = lambda i: (0, 0)
    const3 = lambda i: (0, 0, 0)
    return pl.pallas_call(
        functools.partial(_inmix_kernel, tiles_per_batch=seq // tm),
        grid=(nt + 1,),
        in_specs=[pl.BlockSpec((tm, D_MODEL), cur),
                  _layer(g, l), _layer(w, l),
                  pl.BlockSpec(perm.shape, const2),
                  pl.BlockSpec((tm, LANES), prev),
                  pl.BlockSpec((tm, LANES), prev),
                  _layer(sgu_w, l), _layer(sgu_bias, l), _layer(ln_g, l), _layer(ln_b, l),
                  pl.BlockSpec(decay.shape, const3),
                  pl.BlockSpec(q_dec.shape, const2),
                  pl.BlockSpec(k_dec.shape, const2),
                  pl.BlockSpec(cdm.shape, const3)],
        out_specs=[pl.BlockSpec((tm, QKV_A), cur),
                   pl.BlockSpec((tm, B_WIDTH), prev),
                   pl.BlockSpec((tm, C_WIDTH), prev)],
        out_shape=[jax.ShapeDtypeStruct((n, QKV_A), BF16),
                   jax.ShapeDtypeStruct((n, B_WIDTH), BF16),
                   jax.ShapeDtypeStruct((n, C_WIDTH), BF16)],
        scratch_shapes=[pltpu.VMEM((tm, UV_B), F32), pltpu.VMEM((tm, QKVG_C), F32),
                        pltpu.VMEM((tm, UV_B), F32), pltpu.VMEM((tm, QKVG_C), F32),
                        pltpu.VMEM((N_PAIRS, LANES, LANES), F32),
                        pltpu.VMEM((SGU_GROUPS // 2, BLK, 2 * BLK), BF16)],
        compiler_params=_params("arbitrary"),
        name="inproj_mixers",
    )(x, g, w, perm, cos, sin, sgu_w, sgu_bias, ln_g, ln_b, decay, q_dec, k_dec, cdm)


def _kv_kernel(m_ref, g_ref, w_ref, kv_ref):
    h = _rms(m_ref[...], g_ref[6:7, :]).astype(BF16)
    kv_ref[...] = _dot(h, w_ref[...]).astype(BF16)


def _memory_kv(mem, g, w, l):
    n = mem.shape[0]
    return pl.pallas_call(
        _kv_kernel,
        grid=(n // MEM_LEN,),
        in_specs=[pl.BlockSpec((MEM_LEN, D_MODEL), lambda i: (i, 0)),
                  _layer(g, l), _layer(w, l)],
        out_specs=pl.BlockSpec((MEM_LEN, 2 * D_MODEL), lambda i: (i, 0)),
        out_shape=jax.ShapeDtypeStruct((n, 2 * D_MODEL), BF16),
        compiler_params=_params("parallel"),
        name="memory_kv",
    )(mem, g, w)


def _post_mix_kernel(ya_ref, yb_ref, yc_ref, x_ref, kv_ref, g_ref, w_ref, unperm_ref, wq_ref, wo_ref,
                     out_ref):
    tm = x_ref.shape[0]
    halves = [slice(i * (tm // 2), (i + 1) * (tm // 2)) for i in range(2)]
    heads = [slice(hd * X_HEAD_DIM, (hd + 1) * X_HEAD_DIM) for hd in range(X_HEADS)]
    scale = X_HEAD_DIM ** -0.5

    def unpermute(rows):
        n_grp = (rows.stop - rows.start) // BLK
        return jnp.concatenate(
            [_dot(unperm_ref[...], ya_ref[rows.start + g * BLK:rows.start + (g + 1) * BLK, :]).astype(BF16)
             for g in range(n_grp)], axis=0)

    ya = [unpermute(rows) for rows in halves]
    proj = [_dot(jnp.concatenate([ya[i], yb_ref[rows, :], yc_ref[rows, :]], axis=1), w_ref[...])
            for i, rows in enumerate(halves)]
    x1 = [x_ref[rows, :] + _rms(proj[i], g_ref[1:2, :]) for i, rows in enumerate(halves)]
    q = [_dot(_rms(x1[i], g_ref[2:3, :]).astype(BF16), wq_ref[...]).astype(BF16) for i in range(2)]
    scores = [[_dot_nt(q[i][:, cols], kv_ref[:, cols]) for cols in heads] for i in range(2)]
    att = []
    for i in range(2):
        outs = []
        for hd, s in enumerate(scores[i]):
            s = s * scale
            m = jnp.max(s, axis=-1, keepdims=True)
            p = jnp.exp(s - m)
            p = p / jnp.sum(p, axis=-1, keepdims=True)
            v = kv_ref[:, D_MODEL + hd * X_HEAD_DIM:D_MODEL + (hd + 1) * X_HEAD_DIM]
            outs.append(_dot(p.astype(BF16), v).astype(BF16))
        att.append(jnp.concatenate(outs, axis=1))
    acc = [_dot(att[i], wo_ref[...]) for i in range(2)]
    for i, rows in enumerate(halves):
        out_ref[rows, :] = x1[i] + _rms(acc[i], g_ref[3:4, :])


def _post_mix(ya, yb, yc, x, kv, g, w_out, unperm, wq, wo, l, batch, seq):
    n = x.shape[0]
    tm = ROW_TILE
    per_batch = seq // tm
    row = lambda i: (i, 0)
    const = lambda i: (0, 0)
    return pl.pallas_call(
        _post_mix_kernel,
        grid=(n // tm,),
        in_specs=[pl.BlockSpec((tm, A_WIDTH), row),
                  pl.BlockSpec((tm, B_WIDTH), row),
                  pl.BlockSpec((tm, C_WIDTH), row),
                  pl.BlockSpec((tm, D_MODEL), row),
                  pl.BlockSpec((MEM_LEN, 2 * D_MODEL), lambda i: (i // per_batch, 0)),
                  _layer(g, l), _layer(w_out, l),
                  pl.BlockSpec(unperm.shape, const),
                  _layer(wq, l), _layer(wo, l)],
        out_specs=pl.BlockSpec((tm, D_MODEL), row),
        out_shape=jax.ShapeDtypeStruct((n, D_MODEL), F32),
        compiler_params=_params("parallel"),
        name="outproj_cross",
    )(ya, yb, yc, x, kv, g, w_out, unperm, wq, wo)


def _mlp_kernel(x_ref, g_ref, wu_ref, wd_ref, out_ref):
    x = x_ref[...]
    h = _rms(x, g_ref[4:5, :]).astype(BF16)
    acc = jnp.zeros(x.shape, F32)
    for c in range(D_FF // FF_CHUNK):
        cols = slice(c * FF_CHUNK, (c + 1) * FF_CHUNK)
        f = jnp.square(jnp.maximum(_dot(h, wu_ref[:, cols]), 0.0))
        acc += _dot(f.astype(BF16), wd_ref[cols, :])
    out_ref[...] = x + _rms(acc, g_ref[5:6, :])


def _mlp(x, g, w_up, w_down, l):
    n = x.shape[0]
    tm = ROW_TILE
    row = lambda i: (i, 0)
    const = lambda i: (0, 0)
    return pl.pallas_call(
        _mlp_kernel,
        grid=(n // tm,),
        in_specs=[pl.BlockSpec((tm, D_MODEL), row),
                  _layer(g, l),
                  _layer(w_up, l, pipeline_mode=pl.Buffered(1)),
                  _layer(w_down, l, pipeline_mode=pl.Buffered(1))],
        out_specs=pl.BlockSpec((tm, D_MODEL), row),
        out_shape=jax.ShapeDtypeStruct((n, D_MODEL), F32),
        compiler_params=_params("parallel"),
        name="mlp",
    )(x, g, w_up, w_down)


def kernel(x, mem, positions, norm_g, w_in, sgu_w, sgu_b, sgu_ln_g, sgu_ln_b,
           w_out, x_wq, x_wkv, x_wo, w_up, w_down):
    batch, seq, _ = x.shape
    assert x.shape == (batch, seq, D_MODEL) and seq % (BLK * DIL_PATTERNS[-1][1]) == 0
    assert (batch * seq) % ROW_TILE == 0 and seq % ROW_TILE == 0
    assert mem.shape == (batch, MEM_LEN, D_MODEL) and w_in.shape == (DEPTH, D_MODEL, IN_WIDTH)
    xf = x.reshape(batch * seq, D_MODEL)
    memf = mem.reshape(batch * MEM_LEN, D_MODEL)
    cos, sin = _rope_tables(positions)
    perm = _group_permutation()
    w_in_b, w_out_b = _inproj_weight(w_in), w_out.astype(BF16)
    wq_b, wkv_b, wo_b = x_wq.astype(BF16), x_wkv.astype(BF16), x_wo.astype(BF16)
    w_up_b, w_down_b = w_up.astype(BF16), w_down.astype(BF16)
    sgu_bias = jnp.repeat(jnp.swapaxes(sgu_b, 1, 2), HEAD_DIM, axis=2)
    ln_g = sgu_ln_g.reshape(DEPTH, 1, B_WIDTH)
    ln_b = sgu_ln_b.reshape(DEPTH, 1, B_WIDTH)
    for l in range(DEPTH):
        za, yb, yc = _inproj_mixers(xf, norm_g, w_in_b, perm, cos, sin, sgu_w, sgu_bias, ln_g, ln_b,
                                    l, batch, seq)
        ya = _dilated_attention(za, batch, seq)
        kv = _memory_kv(memf, norm_g, wkv_b, l)
        xf = _post_mix(ya, yb, yc, xf, kv, norm_g, w_out_b, perm.T, wq_b, wo_b, l, batch, seq)
        xf = _mlp(xf, norm_g, w_up_b, w_down_b, l)
    return xf.reshape(batch, seq, D_MODEL)
```

```python
import functools

import jax
import jax.numpy as jnp
from jax import lax
from jax.experimental import pallas as pl
from jax.experimental.pallas import tpu as pltpu

F32 = jnp.float32
BF16 = jnp.bfloat16

D_MODEL = 1024
DEPTH = 2
MEM_LEN = 256
HEAD_DIM = 64
DIL_HEADS = 6
DIL_PATTERNS = ((128, 1), (512, 4), (2048, 16))
BLK = 128
SGU_GROUPS = 4
RET_HEADS = 6
ROPE_BASE = 10000.0
A_WIDTH = DIL_HEADS * HEAD_DIM
B_WIDTH = SGU_GROUPS * HEAD_DIM
C_WIDTH = RET_HEADS * HEAD_DIM
QKV_A = 3 * A_WIDTH
UV_B = 2 * B_WIDTH
QKVG_C = 4 * C_WIDTH
IN_WIDTH = QKV_A + UV_B + QKVG_C
X_HEADS = 4
X_HEAD_DIM = D_MODEL // X_HEADS
D_FF = 4 * D_MODEL
EPS = 1e-6

LANES = 128
N_PAIRS = A_WIDTH // LANES
VMEM_LIMIT = 56 * 1024 * 1024

SUPER = BLK * DIL_PATTERNS[-1][1]

ROW_TILE = 512
DIL_UNROLL = (4, 4, 4)
DIL_LOOKAHEAD = (12, 1, 1)
MIX_UNROLL = 2
MLP_TILE = 1024
FF_CHUNK = 1024


def _params(*sem):
    return pltpu.CompilerParams(dimension_semantics=sem, vmem_limit_bytes=VMEM_LIMIT)


def _layer(arr, l, **kw):
    zeros = (0,) * (arr.ndim - 1)
    return pl.BlockSpec((None,) + arr.shape[1:], lambda *_: (l,) + zeros, **kw)


def _rms(x, g):
    ms = jnp.mean(x * x, axis=-1, keepdims=True)
    return x * lax.rsqrt(ms + EPS) * g


def _dot(a, b):
    return jnp.dot(a, b, preferred_element_type=F32)


def _dot_nt(a, b):
    return lax.dot_general(a, b, (((1,), (1,)), ((), ())), preferred_element_type=F32)


def _dot_tn(a, b):
    return lax.dot_general(a, b, (((0,), (0,)), ((), ())), preferred_element_type=F32)


def _lane_lo(shape):
    lane = lax.broadcasted_iota(jnp.int32, shape, len(shape) - 1)
    return (lane & HEAD_DIM) == 0


def _rope_kernel(pos_ref, inv_ref, cos_ref, sin_ref):
    ang = pos_ref[...].astype(F32) * inv_ref[...]
    lane = lax.broadcasted_iota(jnp.int32, ang.shape, 1)
    sign = jnp.where(lane < HEAD_DIM, -1.0, 1.0)
    cos_ref[...] = jnp.cos(ang)
    sin_ref[...] = jnp.sin(ang) * sign


def _rope_tables(positions):
    n = positions.size
    half = HEAD_DIM // 2
    inv = 1.0 / (ROPE_BASE ** jnp.linspace(0.0, 1.0, half, dtype=F32))
    inv = jnp.tile(inv, LANES // half).reshape(1, LANES)
    tm = 1024
    return pl.pallas_call(
        _rope_kernel,
        grid=(n // tm,),
        in_specs=[pl.BlockSpec((tm, 1), lambda i: (i, 0)),
                  pl.BlockSpec((1, LANES), lambda i: (0, 0))],
        out_specs=[pl.BlockSpec((tm, LANES), lambda i: (i, 0))] * 2,
        out_shape=[jax.ShapeDtypeStruct((n, LANES), F32)] * 2,
        compiler_params=_params("parallel"),
        name="rope_tables",
    )(positions.reshape(n, 1), inv)


MAX_DIL = DIL_PATTERNS[-1][1]
SUB = 8


def _group_permutation():
    i = jnp.arange(BLK)
    src = MAX_DIL * (i % SUB) + i // SUB
    return (src[:, None] == jnp.arange(BLK)[None, :]).astype(BF16)


def _run_offsets(d):
    return [BLK * g + SUB * d * c for g in range(d) for c in range(MAX_DIL // d)]


def _block_load(ref, p, base, d):
    if d == 1:
        return ref[p, pl.ds(pl.multiple_of(base, BLK), BLK), :]
    return jnp.concatenate(
        [ref[p, pl.ds(pl.multiple_of(base + off, SUB), SUB), :] for off in _run_offsets(d)], axis=0)


def _block_store(ref, p, base, d, val):
    if d == 1:
        ref[p, pl.ds(pl.multiple_of(base, BLK), BLK), :] = val
        return
    for i, off in enumerate(_run_offsets(d)):
        ref[p, pl.ds(pl.multiple_of(base + off, SUB), SUB), :] = val[i * SUB:(i + 1) * SUB]


def _dil_blocks(refs, branch, d, blocks, first, lookahead):
    bias_ref, qp, kring, vring, acc_o, acc_m, acc_l = refs
    lo = _lane_lo((BLK, LANES))
    units = [(blk, p) for blk in blocks for p in range(N_PAIRS)]

    def scores(unit):
        (qbase, kcbase, kpbase, _), p = unit
        q = _block_load(qp, p, qbase, d).astype(BF16)
        k = jnp.concatenate([_block_load(kring, p, kpbase, d), _block_load(kring, p, kcbase, d)],
                            axis=0).astype(BF16)
        return [_dot_nt(jnp.where(lo if e == 0 else ~lo, q, jnp.zeros_like(q)), k) for e in range(2)]

    def softmax_pv(unit, s_pair):
        (qbase, kcbase, kpbase, has_prev), p = unit
        bias_prev = jnp.where(has_prev, bias_ref[branch, :, :BLK], -jnp.inf)
        bias_cur = bias_ref[branch, :, BLK:]
        v = jnp.concatenate([_block_load(vring, p, kpbase, d), _block_load(vring, p, kcbase, d)],
                            axis=0).astype(BF16)
        o_h, m_h, l_h = [], [], []
        for s in s_pair:
            sp = s[:, :BLK] + bias_prev
            sc = s[:, BLK:] + bias_cur
            m = jnp.max(jnp.maximum(sp, sc), axis=-1, keepdims=True)
            pp = jnp.exp(sp - m)
            pc = jnp.exp(sc - m)
            l_h.append(jnp.sum(pp + pc, axis=-1, keepdims=True))
            m_h.append(m)
            o_h.append(_dot(jnp.concatenate([pp, pc], axis=1).astype(BF16), v))
        return o_h, m_h, l_h

    def merge(unit, stats):
        (qbase, _, _, _), p = unit
        o_h, m_h, l_h = stats
        o_b = jnp.where(lo, o_h[0], o_h[1])
        m_b = jnp.where(lo, m_h[0], m_h[1])
        l_b = jnp.where(lo, l_h[0], l_h[1])
        if first:
            _block_store(acc_o, p, qbase, d, o_b)
            _block_store(acc_m, p, qbase, d, m_b)
            _block_store(acc_l, p, qbase, d, l_b)
        else:
            m_old = _block_load(acc_m, p, qbase, d)
            m_new = jnp.maximum(m_old, m_b)
            a_old = jnp.exp(m_old - m_new)
            a_b = jnp.exp(m_b - m_new)
            _block_store(acc_o, p, qbase, d, _block_load(acc_o, p, qbase, d) * a_old + o_b * a_b)
            _block_store(acc_l, p, qbase, d, _block_load(acc_l, p, qbase, d) * a_old + l_b * a_b)
            _block_store(acc_m, p, qbase, d, m_new)

    ahead = min(lookahead, len(units))
    pending = [scores(u) for u in units[:ahead]]
    stats_prev = None
    for i, unit in enumerate(units):
        if i + ahead < len(units):
            pending.append(scores(units[i + ahead]))
        stats = softmax_pv(unit, pending[i])
        if stats_prev is not None:
            merge(units[i - 1], stats_prev)
        stats_prev = stats
    merge(units[-1], stats_prev)


def _dil_kernel(q_ref, k_ref, v_ref, bias_ref, o_ref, qp, kring, vring, acc_o, acc_m, acc_l):
    t = pl.program_id(1)
    cur = pl.multiple_of((t % 2) * SUPER, SUPER)
    other = pl.multiple_of(SUPER - cur, SUPER)

    @pl.when(t == 0)
    def _():
        kring[:, SUPER:, :] = jnp.zeros((N_PAIRS, SUPER, LANES), F32)
        vring[:, SUPER:, :] = jnp.zeros((N_PAIRS, SUPER, LANES), F32)

    for p in range(N_PAIRS):
        cols = slice(p * LANES, (p + 1) * LANES)
        qp[p] = q_ref[:, cols].astype(F32) * (HEAD_DIM ** -0.5)
        kring[p, pl.ds(cur, SUPER), :] = k_ref[:, cols].astype(F32)
        vring[p, pl.ds(cur, SUPER), :] = v_ref[:, cols].astype(F32)
    refs = (bias_ref, qp, kring, vring, acc_o, acc_m, acc_l)

    for branch, (window, d) in enumerate(DIL_PATTERNS):
        nblk = MAX_DIL // d
        span = BLK * d

        def block_rows(j, d=d, nblk=nblk, span=span):
            if nblk == 1:
                qbase = pl.multiple_of(j * SUB, SUB)
                return qbase, cur + qbase, other + qbase, t > 0
            r = j // nblk
            nb = j % nblk
            qbase = pl.multiple_of(nb * span + r * SUB, SUB)
            kpbase = pl.multiple_of(
                jnp.where(nb == 0, other + (nblk - 1) * span + r * SUB, cur + qbase - span), SUB)
            return qbase, cur + qbase, kpbase, (t > 0) | (nb > 0)

        unroll, lookahead = DIL_UNROLL[branch], DIL_LOOKAHEAD[branch]

        def body(i, carry, branch=branch, d=d, block_rows=block_rows, unroll=unroll, lookahead=lookahead):
            blocks = [block_rows(i * unroll + u) for u in range(unroll)]
            _dil_blocks(refs, branch, d, blocks, first=(branch == 0), lookahead=lookahead)
            return carry

        lax.fori_loop(0, SUPER // BLK // unroll, body, 0)

    def finish(i, carry):
        rows = pl.ds(pl.multiple_of(i * BLK, BLK), BLK)
        for p in range(N_PAIRS):
            o_ref[rows, p * LANES:(p + 1) * LANES] = (acc_o[p, rows, :] / acc_l[p, rows, :]).astype(BF16)
        return carry

    lax.fori_loop(0, SUPER // BLK, finish, 0)


def _dilated_bias():
    out = []
    i = jnp.arange(BLK)
    for _, d in DIL_PATTERNS:
        n_c = MAX_DIL // d
        pos = (BLK // d) * (i // (SUB * n_c)) + n_c * (i % SUB) + (i // SUB) % n_c
        prev_ok = pos[None, :] >= pos[:, None]
        cur_ok = pos[None, :] <= pos[:, None]
        out.append(jnp.where(jnp.concatenate([prev_ok, cur_ok], axis=1), 0.0, -jnp.inf))
    return jnp.stack(out).astype(F32)


def _dilated_attention(qkv, batch, seq):
    n = batch * seq
    tiles = seq // SUPER
    bias = _dilated_bias()
    spec = lambda which: pl.BlockSpec((SUPER, A_WIDTH), lambda b, t: (b * tiles + t, which))
    tile_f32 = pltpu.VMEM((N_PAIRS, SUPER, LANES), F32)
    ring_f32 = pltpu.VMEM((N_PAIRS, 2 * SUPER, LANES), F32)
    return pl.pallas_call(
        _dil_kernel,
        grid=(batch, tiles),
        in_specs=[spec(0), spec(1), spec(2), pl.BlockSpec(bias.shape, lambda b, t: (0, 0, 0))],
        out_specs=pl.BlockSpec((SUPER, A_WIDTH), lambda b, t: (b * tiles + t, 0)),
        out_shape=jax.ShapeDtypeStruct((n, A_WIDTH), BF16),
        scratch_shapes=[tile_f32, ring_f32, ring_f32, tile_f32, tile_f32, tile_f32],
        compiler_params=_params("parallel", "arbitrary"),
        name="dilated_attention",
    )(qkv, qkv, qkv, bias)


def _rotate(t, cos, sin_signed):
    return t * cos + pltpu.roll(t, HEAD_DIM, 1) * sin_signed


PROJ_PIECES = (
    (QKV_A + UV_B, 512), (QKV_A + UV_B + 512, 512), (QKV_A + UV_B + 1024, 512),
    (QKV_A, UV_B),
    (0, 640), (640, QKV_A - 640))


def _inmix_body(x_ref, g_ref, w_ref, perm_ref, cos_ref, sin_ref, sb_ref, lg_ref, lb_ref,
                dec_ref, qdec_ref, kdec_ref, cdm_ref, a_ref, yb_ref, yc_ref,
                zb_w, zc_w, zb_r, zc_r, state_ref, wt_ref):
    lo = _lane_lo((BLK, LANES))
    row = lax.broadcasted_iota(jnp.int32, (BLK, BLK), 0)
    col = lax.broadcasted_iota(jnp.int32, (BLK, BLK), 1)
    same_head = ((row >> 5) & 1) == (col >> 6)
    lane = lax.broadcasted_iota(jnp.int32, (BLK, LANES), 1)
    qk_head0 = (lane & (HEAD_DIM // 2)) == 0
    inv_n = 1.0 / HEAD_DIM

    def sgu_pre(rows):
        u = jax.nn.gelu(zb_r[rows, :B_WIDTH])
        v = jax.nn.gelu(zb_r[rows, B_WIDTH:])
        mu = jnp.mean(v, axis=-1, keepdims=True)
        var = jnp.mean(jnp.square(v - mu), axis=-1, keepdims=True)
        v = (v - mu) * lax.rsqrt(var + EPS) * lg_ref[...] + lb_ref[...]
        return u, v.astype(BF16)

    def sgu_post(rows, pre):
        u, v = pre
        for gp in range(B_WIDTH // LANES):
            cols = slice(gp * LANES, (gp + 1) * LANES)
            vp = v[:, cols]
            v2 = jnp.concatenate([jnp.where(lo, vp, jnp.zeros_like(vp)),
                                  jnp.where(lo, jnp.zeros_like(vp), vp)], axis=0)
            gate = _dot(wt_ref[gp], v2) + sb_ref[:, cols]
            yb_ref[rows, cols] = (u[:, cols] * gate).astype(BF16)

    def ret_scores(row_list):
        work = [(rows, p) for rows in row_list for p in range(N_PAIRS)]
        out = []
        for rows, p in work:
            cols = slice(p * LANES, (p + 1) * LANES)
            cos, sin = cos_ref[rows, :], sin_ref[rows, :]
            q = _rotate(zc_r[rows, p * LANES:(p + 1) * LANES], cos, sin)
            k = _rotate(zc_r[rows, C_WIDTH + p * LANES:C_WIDTH + (p + 1) * LANES], cos, sin)
            vb = zc_r[rows, 2 * C_WIDTH + p * LANES:2 * C_WIDTH + (p + 1) * LANES].astype(BF16)
            qb, kb = q.astype(BF16), k.astype(BF16)
            k2 = jnp.concatenate([jnp.where(qk_head0, kb, jnp.zeros_like(kb)),
                                  jnp.where(qk_head0, jnp.zeros_like(kb), kb)], axis=0)
            scores = _dot_nt(qb, k2)
            kv = _dot_tn((k * kdec_ref[:, cols]).astype(BF16), vb)
            out.append((scores, kv, (q * qdec_ref[:, cols]).astype(BF16), vb))
        return work, out

    def ret_values(work, stage1):
        out = []
        for (rows, p), (scores, kv, qd, vb) in zip(work, stage1):
            state = state_ref[p]
            y_x = _dot(qd, state.astype(BF16))
            state_ref[p] = state * cdm_ref[p] + jnp.where(same_head, kv, 0.0)
            v2 = jnp.concatenate([jnp.where(lo, vb, jnp.zeros_like(vb)),
                                  jnp.where(lo, jnp.zeros_like(vb), vb)], axis=0)
            y_in = _dot((scores * dec_ref[p]).astype(BF16), v2)
            out.append((y_in, y_x))
        return out

    def ret_norm(work, stage2):
        for (rows, p), (y_in, y_x) in zip(work, stage2):
            cols = slice(p * LANES, (p + 1) * LANES)
            y = y_in + y_x
            mu_lo = jnp.sum(jnp.where(lo, y, 0.0), axis=-1, keepdims=True) * inv_n
            mu_hi = jnp.sum(jnp.where(lo, 0.0, y), axis=-1, keepdims=True) * inv_n
            yc_ = y - jnp.where(lo, mu_lo, mu_hi)
            sq = yc_ * yc_
            var_lo = jnp.sum(jnp.where(lo, sq, 0.0), axis=-1, keepdims=True) * inv_n
            var_hi = jnp.sum(jnp.where(lo, 0.0, sq), axis=-1, keepdims=True) * inv_n
            yn = yc_ * lax.rsqrt(jnp.where(lo, var_lo, var_hi) + EPS)
            gate = zc_r[rows, 3 * C_WIDTH + p * LANES:3 * C_WIDTH + (p + 1) * LANES]
            yc_ref[rows, cols] = (jax.nn.silu(gate) * yn).astype(BF16)

    def project(lhs, piece):
        first, width = PROJ_PIECES[piece]
        y = _dot(lhs, w_ref[:, first:first + width])
        if first >= QKV_A + UV_B:
            zc_w[:, first - QKV_A - UV_B:first - QKV_A - UV_B + width] = y
        elif first >= QKV_A:
            zb_w[...] = y
        else:
            a_ref[:, first:first + width] = y.astype(BF16)

    tm = x_ref.shape[0]
    chunks = [slice(c * BLK, (c + 1) * BLK) for c in range(tm // BLK)]
    groups = [chunks[g:g + MIX_UNROLL] for g in range(0, len(chunks), MIX_UNROLL)]
    assert len(groups) == 2 and len(PROJ_PIECES) == 6

    h = _rms(x_ref[...], g_ref[0:1, :]).astype(BF16)
    hp = jnp.concatenate([_dot(perm_ref[...], h[c]).astype(BF16) for c in chunks], axis=0)
    project(h, 0)
    pre = [sgu_pre(rows) for rows in groups[0]]
    work0, s1 = ret_scores(groups[0])
    project(h, 1)
    for rows, t in zip(groups[0], pre):
        sgu_post(rows, t)
    s2 = ret_values(work0, s1)
    project(h, 2)
    ret_norm(work0, s2)
    pre = [sgu_pre(rows) for rows in groups[1]]
    work1, s1 = ret_scores(groups[1])
    project(h, 3)
    for rows, t in zip(groups[1], pre):
        sgu_post(rows, t)
    s2 = ret_values(work1, s1)
    project(hp, 4)
    ret_norm(work1, s2)
    project(hp, 5)


def _inmix_kernel(x_ref, g_ref, w_ref, perm_ref, cos_ref, sin_ref, sw_ref, sb_ref, lg_ref, lb_ref,
                  dec_ref, qdec_ref, kdec_ref, cdm_ref, a_ref, yb_ref, yc_ref,
                  zb0, zc0, zb1, zc1, state_ref, wt_ref, *, tiles_per_batch):
    i = pl.program_id(0)

    @pl.when(i == 0)
    def _():
        zb1[...] = jnp.zeros_like(zb1)
        zc1[...] = jnp.zeros_like(zc1)
        row = lax.broadcasted_iota(jnp.int32, (BLK, BLK), 0)
        col = lax.broadcasted_iota(jnp.int32, (BLK, BLK), 1)
        for gp in range(SGU_GROUPS // 2):
            wt_ref[gp] = jnp.concatenate([jnp.where(row >= col, sw_ref[2 * gp + e], 0.0)
                                          for e in range(2)], axis=1).astype(BF16)

    @pl.when((i == 0) | ((i - 1) % tiles_per_batch == 0))
    def _():
        state_ref[...] = jnp.zeros_like(state_ref)

    common = (x_ref, g_ref, w_ref, perm_ref, cos_ref, sin_ref, sb_ref, lg_ref, lb_ref,
              dec_ref, qdec_ref, kdec_ref, cdm_ref, a_ref, yb_ref, yc_ref)

    @pl.when(i % 2 == 0)
    def _():
        _inmix_body(*common, zb0, zc0, zb1, zc1, state_ref, wt_ref)

    @pl.when(i % 2 == 1)
    def _():
        _inmix_body(*common, zb1, zc1, zb0, zc0, state_ref, wt_ref)


def _qk_lane_head():
    lane = jnp.arange(C_WIDTH)
    return 2 * (lane // LANES) + ((lane >> 5) & 1)


def _qk_reorder(w):
    lead = w.shape[:-1]
    w = w.reshape(lead + (N_PAIRS, 2, 2, HEAD_DIM // 2))
    return jnp.swapaxes(w, -3, -2).reshape(lead + (C_WIDTH,))


def _inproj_weight(w):
    q0 = QKV_A + UV_B
    w = w.astype(BF16)
    return jnp.concatenate([w[..., :q0], _qk_reorder(w[..., q0:q0 + C_WIDTH]),
                            _qk_reorder(w[..., q0 + C_WIDTH:q0 + 2 * C_WIDTH]),
                            w[..., q0 + 2 * C_WIDTH:]], axis=-1)


def _retention_tables():
    heads = jnp.arange(RET_HEADS, dtype=F32)
    log_g = jnp.log1p(-jnp.power(2.0, -5.0 - heads))
    idx = jnp.arange(BLK, dtype=F32)
    rel = idx[:, None] - idx[None, :]
    scale = HEAD_DIM ** -0.5
    decay = jnp.where(rel[None] >= 0,
                      jnp.exp(jnp.maximum(rel, 0.0)[None] * log_g[:, None, None]), 0.0) * scale
    k_dec = jnp.exp((BLK - 1 - idx)[:, None] * log_g[None, :]) * scale
    q_dec = jnp.exp((idx + 1.0)[:, None] * log_g[None, :])
    lane_head = _qk_lane_head()
    k_dec = k_dec[:, lane_head]
    q_dec = q_dec[:, lane_head]
    chunk_decay = jnp.exp(BLK * log_g)
    per_row = chunk_decay[lane_head].reshape(N_PAIRS, LANES)
    row_head = (jnp.arange(LANES) >> 5) & 1
    col_head = jnp.arange(LANES) >> 6
    same = row_head[:, None] == col_head[None, :]
    cdm = jnp.where(same[None], per_row[:, :, None], 0.0)
    decay = jnp.concatenate([decay[0::2], decay[1::2]], axis=2)
    return decay, q_dec, k_dec, cdm


def _inproj_mixers(x, g, w, perm, cos, sin, sgu_w, sgu_bias, ln_g, ln_b, l, batch, seq):
    n = x.shape[0]
    tm = ROW_TILE
    nt = n // tm
    decay, q_dec, k_dec, cdm = _retention_tables()
    cur = lambda i: (jnp.minimum(i, nt - 1), 0)
    prev = lambda i: (jnp.maximum(i - 1, 0), 0)
    const2 = lambda i: (0, 0)
    const3 = lambda i: (0, 0, 0)
    return pl.pallas_call(
        functools.partial(_inmix_kernel, tiles_per_batch=seq // tm),
        grid=(nt + 1,),
        in_specs=[pl.BlockSpec((tm, D_MODEL), cur),
                  _layer(g, l), _layer(w, l),
                  pl.BlockSpec(perm.shape, const2),
                  pl.BlockSpec((tm, LANES), prev),
                  pl.BlockSpec((tm, LANES), prev),
                  _layer(sgu_w, l), _layer(sgu_bias, l), _layer(ln_g, l), _layer(ln_b, l),
                  pl.BlockSpec(decay.shape, const3),
                  pl.BlockSpec(q_dec.shape, const2),
                  pl.BlockSpec(k_dec.shape, const2),
                  pl.BlockSpec(cdm.shape, const3)],
        out_specs=[pl.BlockSpec((tm, QKV_A), cur),
                   pl.BlockSpec((tm, B_WIDTH), prev),
                   pl.BlockSpec((tm, C_WIDTH), prev)],
        out_shape=[jax.ShapeDtypeStruct((n, QKV_A), BF16),
                   jax.ShapeDtypeStruct((n, B_WIDTH), BF16),
                   jax.ShapeDtypeStruct((n, C_WIDTH), BF16)],
        scratch_shapes=[pltpu.VMEM((tm, UV_B), F32), pltpu.VMEM((tm, QKVG_C), F32),
                        pltpu.VMEM((tm, UV_B), F32), pltpu.VMEM((tm, QKVG_C), F32),
                        pltpu.VMEM((N_PAIRS, LANES, LANES), F32),
                        pltpu.VMEM((SGU_GROUPS // 2, BLK, 2 * BLK), BF16)],
        compiler_params=_params("arbitrary"),
        name="inproj_mixers",
    )(x, g, w, perm, cos, sin, sgu_w, sgu_bias, ln_g, ln_b, decay, q_dec, k_dec, cdm)


def _kv_kernel(m_ref, g_ref, w_ref, kv_ref):
    h = _rms(m_ref[...], g_ref[6:7, :]).astype(BF16)
    kv_ref[...] = _dot(h, w_ref[...]).astype(BF16)


def _memory_kv(mem, g, w, l):
    n = mem.shape[0]
    return pl.pallas_call(
        _kv_kernel,
        grid=(n // MEM_LEN,),
        in_specs=[pl.BlockSpec((MEM_LEN, D_MODEL), lambda i: (i, 0)),
                  _layer(g, l), _layer(w, l)],
        out_specs=pl.BlockSpec((MEM_LEN, 2 * D_MODEL), lambda i: (i, 0)),
        out_shape=jax.ShapeDtypeStruct((n, 2 * D_MODEL), BF16),
        compiler_params=_params("parallel"),
        name="memory_kv",
    )(mem, g, w)


def _post_mix_kernel(ya_ref, yb_ref, yc_ref, x_ref, kv_ref, g_ref, w_ref, unperm_ref, wq_ref, wo_ref,
                     out_ref):
    tm = x_ref.shape[0]
    halves = [slice(i * (tm // 2), (i + 1) * (tm // 2)) for i in range(2)]
    heads = [slice(hd * X_HEAD_DIM, (hd + 1) * X_HEAD_DIM) for hd in range(X_HEADS)]
    scale = X_HEAD_DIM ** -0.5

    def unpermute(rows):
        n_grp = (rows.stop - rows.start) // BLK
        return jnp.concatenate(
            [_dot(unperm_ref[...], ya_ref[rows.start + g * BLK:rows.start + (g + 1) * BLK, :]).astype(BF16)
             for g in range(n_grp)], axis=0)

    ya = [unpermute(rows) for rows in halves]
    proj = [_dot(jnp.concatenate([ya[i], yb_ref[rows, :], yc_ref[rows, :]], axis=1), w_ref[...])
            for i, rows in enumerate(halves)]
    x1 = [x_ref[rows, :] + _rms(proj[i], g_ref[1:2, :]) for i, rows in enumerate(halves)]
    q = [_dot(_rms(x1[i], g_ref[2:3, :]).astype(BF16), wq_ref[...]).astype(BF16) for i in range(2)]
    scores = [[_dot_nt(q[i][:, cols], kv_ref[:, cols]) for cols in heads] for i in range(2)]
    att = []
    for i in range(2):
        outs = []
        for hd, s in enumerate(scores[i]):
            s = s * scale
            m = jnp.max(s, axis=-1, keepdims=True)
            p = jnp.exp(s - m)
            p = p / jnp.sum(p, axis=-1, keepdims=True)
            v = kv_ref[:, D_MODEL + hd * X_HEAD_DIM:D_MODEL + (hd + 1) * X_HEAD_DIM]
            outs.append(_dot(p.astype(BF16), v).astype(BF16))
        att.append(jnp.concatenate(outs, axis=1))
    acc = [_dot(att[i], wo_ref[...]) for i in range(2)]
    for i, rows in enumerate(halves):
        out_ref[rows, :] = x1[i] + _rms(acc[i], g_ref[3:4, :])


def _post_mix(ya, yb, yc, x, kv, g, w_out, unperm, wq, wo, l, batch, seq):
    n = x.shape[0]
    tm = ROW_TILE
    per_batch = seq // tm
    row = lambda i: (i, 0)
    const = lambda i: (0, 0)
    return pl.pallas_call(
        _post_mix_kernel,
        grid=(n // tm,),
        in_specs=[pl.BlockSpec((tm, A_WIDTH), row),
                  pl.BlockSpec((tm, B_WIDTH), row),
                  pl.BlockSpec((tm, C_WIDTH), row),
                  pl.BlockSpec((tm, D_MODEL), row),
                  pl.BlockSpec((MEM_LEN, 2 * D_MODEL), lambda i: (i // per_batch, 0)),
                  _layer(g, l), _layer(w_out, l),
                  pl.BlockSpec(unperm.shape, const),
                  _layer(wq, l), _layer(wo, l)],
        out_specs=pl.BlockSpec((tm, D_MODEL), row),
        out_shape=jax.ShapeDtypeStruct((n, D_MODEL), F32),
        compiler_params=_params("parallel"),
        name="outproj_cross",
    )(ya, yb, yc, x, kv, g, w_out, unperm, wq, wo)


def _mlp_kernel(x_ref, g_ref, wu_ref, wd_ref, out_ref):
    x = x_ref[...]
    h = _rms(x, g_ref[4:5, :]).astype(BF16)
    acc = jnp.zeros(x.shape, F32)
    for c in range(D_FF // FF_CHUNK):
        cols = slice(c * FF_CHUNK, (c + 1) * FF_CHUNK)
        f = jnp.square(jnp.maximum(_dot(h, wu_ref[:, cols]), 0.0))
        acc += _dot(f.astype(BF16), wd_ref[cols, :])
    out_ref[...] = x + _rms(acc, g_ref[5:6, :])


def _mlp(x, g, w_up, w_down, l):
    n = x.shape[0]
    tm = MLP_TILE
    row = lambda i: (i, 0)
    const = lambda i: (0, 0)
    return pl.pallas_call(
        _mlp_kernel,
        grid=(n // tm,),
        in_specs=[pl.BlockSpec((tm, D_MODEL), row),
                  _layer(g, l),
                  _layer(w_up, l, pipeline_mode=pl.Buffered(1)),
                  _layer(w_down, l, pipeline_mode=pl.Buffered(1))],
        out_specs=pl.BlockSpec((tm, D_MODEL), row),
        out_shape=jax.ShapeDtypeStruct((n, D_MODEL), F32),
        compiler_params=_params("parallel"),
        name="mlp",
    )(x, g, w_up, w_down)


def kernel(x, mem, positions, norm_g, w_in, sgu_w, sgu_b, sgu_ln_g, sgu_ln_b,
           w_out, x_wq, x_wkv, x_wo, w_up, w_down):
    batch, seq, _ = x.shape
    assert x.shape == (batch, seq, D_MODEL) and seq % (BLK * DIL_PATTERNS[-1][1]) == 0
    assert (batch * seq) % ROW_TILE == 0 and seq % ROW_TILE == 0
    assert mem.shape == (batch, MEM_LEN, D_MODEL) and w_in.shape == (DEPTH, D_MODEL, IN_WIDTH)
    xf = x.reshape(batch * seq, D_MODEL)
    memf = mem.reshape(batch * MEM_LEN, D_MODEL)
    cos, sin = _rope_tables(positions)
    perm = _group_permutation()
    w_in_b, w_out_b = _inproj_weight(w_in), w_out.astype(BF16)
    wq_b, wkv_b, wo_b = x_wq.astype(BF16), x_wkv.astype(BF16), x_wo.astype(BF16)
    w_up_b, w_down_b = w_up.astype(BF16), w_down.astype(BF16)
    sgu_bias = jnp.repeat(jnp.swapaxes(sgu_b, 1, 2), HEAD_DIM, axis=2)
    ln_g = sgu_ln_g.reshape(DEPTH, 1, B_WIDTH)
    ln_b = sgu_ln_b.reshape(DEPTH, 1, B_WIDTH)
    for l in range(DEPTH):
        za, yb, yc = _inproj_mixers(xf, norm_g, w_in_b, perm, cos, sin, sgu_w, sgu_bias, ln_g, ln_b,
                                    l, batch, seq)
        ya = _dilated_attention(za, batch, seq)
        kv = _memory_kv(memf, norm_g, wkv_b, l)
        xf = _post_mix(ya, yb, yc, xf, kv, norm_g, w_out_b, perm.T, wq_b, wo_b, l, batch, seq)
        xf = _mlp(xf, norm_g, w_up_b, w_down_b, l)
    return xf.reshape(batch, seq, D_MODEL)
```

```python
import functools

import jax
import jax.numpy as jnp
import numpy as np
from jax import lax
from jax.experimental import pallas as pl
from jax.experimental.pallas import tpu as pltpu

F32 = jnp.float32
BF16 = jnp.bfloat16

D_MODEL = 1024
DEPTH = 2
MEM_LEN = 256
HEAD_DIM = 64
DIL_HEADS = 6
DIL_PATTERNS = ((128, 1), (512, 4), (2048, 16))
BLK = 128
SGU_GROUPS = 4
RET_HEADS = 6
ROPE_BASE = 10000.0
A_WIDTH = DIL_HEADS * HEAD_DIM
B_WIDTH = SGU_GROUPS * HEAD_DIM
C_WIDTH = RET_HEADS * HEAD_DIM
QKV_A = 3 * A_WIDTH
UV_B = 2 * B_WIDTH
QKVG_C = 4 * C_WIDTH
IN_WIDTH = QKV_A + UV_B + QKVG_C
X_HEADS = 4
X_HEAD_DIM = D_MODEL // X_HEADS
D_FF = 4 * D_MODEL
EPS = 1e-6

LANES = 128
N_PAIRS = A_WIDTH // LANES
VMEM_LIMIT = 56 * 1024 * 1024

SUPER = BLK * DIL_PATTERNS[-1][1]

ROW_TILE = 512
DIL_UNROLL = (4, 4, 4)
DIL_LOOKAHEAD = (12, 3, 3)
MIX_UNROLL = 2
MLP_TILE = 1024
FF_CHUNK = 1024


def _params(*sem):
    return pltpu.CompilerParams(dimension_semantics=sem, vmem_limit_bytes=VMEM_LIMIT)


def _layer(arr, l, **kw):
    zeros = (0,) * (arr.ndim - 1)
    return pl.BlockSpec((None,) + arr.shape[1:], lambda *_: (l,) + zeros, **kw)


def _rms(x, g):
    ms = jnp.mean(x * x, axis=-1, keepdims=True)
    return x * lax.rsqrt(ms + EPS) * g


def _dot(a, b):
    return jnp.dot(a, b, preferred_element_type=F32)


def _dot_nt(a, b):
    return lax.dot_general(a, b, (((1,), (1,)), ((), ())), preferred_element_type=F32)


def _dot_tn(a, b):
    return lax.dot_general(a, b, (((0,), (0,)), ((), ())), preferred_element_type=F32)


def _lane_lo(shape):
    lane = lax.broadcasted_iota(jnp.int32, shape, len(shape) - 1)
    return (lane & HEAD_DIM) == 0


ROPE_PACK = LANES // (HEAD_DIM // 2)


def _rope_kernel(pos_ref, inv_ref, cos_ref, sin_ref):
    half = HEAD_DIM // 2
    ang = pos_ref[...].astype(F32) * inv_ref[...]
    cos, sin = jnp.cos(ang), jnp.sin(ang)
    lane = lax.broadcasted_iota(jnp.int32, ang.shape, 1)
    sign = jnp.where(lane < HEAD_DIM, -1.0, 1.0)
    rows = ang.shape[0]
    for t in range(ROPE_PACK):
        mine = (lane >= t * half) & (lane < (t + 1) * half)
        for table, out_ref, scale in ((cos, cos_ref, None), (sin, sin_ref, sign)):
            y = jnp.where(mine, table, 0.0)
            y = y + pltpu.roll(y, 2 * half, 1)
            y = y + pltpu.roll(y, half, 1)
            out_ref[pl.ds(t, rows, stride=ROPE_PACK), :] = y if scale is None else y * scale


def _rope_tables(positions):
    n = positions.size
    half = HEAD_DIM // 2
    inv = 1.0 / (ROPE_BASE ** jnp.linspace(0.0, 1.0, half, dtype=F32))
    inv = jnp.tile(inv, ROPE_PACK).reshape(1, LANES)
    pos = jnp.repeat(positions.reshape(n // ROPE_PACK, ROPE_PACK), half, axis=1)
    tm = 2048
    return pl.pallas_call(
        _rope_kernel,
        grid=(n // tm,),
        in_specs=[pl.BlockSpec((tm // ROPE_PACK, LANES), lambda i: (i, 0)),
                  pl.BlockSpec((1, LANES), lambda i: (0, 0))],
        out_specs=[pl.BlockSpec((tm, LANES), lambda i: (i, 0))] * 2,
        out_shape=[jax.ShapeDtypeStruct((n, LANES), F32)] * 2,
        compiler_params=_params("parallel"),
        name="rope_tables",
    )(pos, inv)


MAX_DIL = DIL_PATTERNS[-1][1]
SUB = 8


def _group_permutation():
    i = np.arange(BLK)
    src = MAX_DIL * (i % SUB) + i // SUB
    return jnp.asarray(src[:, None] == np.arange(BLK)[None, :], BF16)


def _run_offsets(d):
    return [BLK * g + SUB * d * c for g in range(d) for c in range(MAX_DIL // d)]


def _block_load(ref, p, base, d):
    if d == 1:
        return ref[p, pl.ds(pl.multiple_of(base, BLK), BLK), :]
    return jnp.concatenate(
        [ref[p, pl.ds(pl.multiple_of(base + off, SUB), SUB), :] for off in _run_offsets(d)], axis=0)


def _block_store(ref, p, base, d, val):
    if d == 1:
        ref[p, pl.ds(pl.multiple_of(base, BLK), BLK), :] = val
        return
    for i, off in enumerate(_run_offsets(d)):
        ref[p, pl.ds(pl.multiple_of(base + off, SUB), SUB), :] = val[i * SUB:(i + 1) * SUB]


def _dil_blocks(refs, branch, d, blocks, first, lookahead):
    bias_ref, qp, kring, vring, acc_o, acc_m, acc_l = refs
    lo = _lane_lo((BLK, LANES))
    units = [(blk, p) for blk in blocks for p in range(N_PAIRS)]

    def scores(unit):
        (qbase, kcbase, kpbase, _), p = unit
        q = _block_load(qp, p, qbase, d).astype(BF16)
        k = jnp.concatenate([_block_load(kring, p, kpbase, d), _block_load(kring, p, kcbase, d)],
                            axis=0).astype(BF16)
        return [_dot_nt(jnp.where(lo if e == 0 else ~lo, q, jnp.zeros_like(q)), k) for e in range(2)]

    def softmax_pv(unit, s_pair):
        (qbase, kcbase, kpbase, has_prev), p = unit
        bias_prev = jnp.where(has_prev, bias_ref[branch, :, :BLK], -jnp.inf)
        bias_cur = bias_ref[branch, :, BLK:]
        v = jnp.concatenate([_block_load(vring, p, kpbase, d), _block_load(vring, p, kcbase, d)],
                            axis=0).astype(BF16)
        o_h, m_h, l_h = [], [], []
        for s in s_pair:
            sp = s[:, :BLK] + bias_prev
            sc = s[:, BLK:] + bias_cur
            m = jnp.max(jnp.maximum(sp, sc), axis=-1, keepdims=True)
            pp = jnp.exp(sp - m)
            pc = jnp.exp(sc - m)
            l_h.append(jnp.sum(pp + pc, axis=-1, keepdims=True))
            m_h.append(m)
            o_h.append(_dot(jnp.concatenate([pp, pc], axis=1).astype(BF16), v))
        return o_h, m_h, l_h

    def merge(unit, stats):
        (qbase, _, _, _), p = unit
        o_h, m_h, l_h = stats
        o_b = jnp.where(lo, o_h[0], o_h[1])
        m_b = jnp.where(lo, m_h[0], m_h[1])
        l_b = jnp.where(lo, l_h[0], l_h[1])
        if first:
            _block_store(acc_o, p, qbase, d, o_b)
            _block_store(acc_m, p, qbase, d, m_b)
            _block_store(acc_l, p, qbase, d, l_b)
        else:
            m_old = _block_load(acc_m, p, qbase, d)
            m_new = jnp.maximum(m_old, m_b)
            a_old = jnp.exp(m_old - m_new)
            a_b = jnp.exp(m_b - m_new)
            _block_store(acc_o, p, qbase, d, _block_load(acc_o, p, qbase, d) * a_old + o_b * a_b)
            _block_store(acc_l, p, qbase, d, _block_load(acc_l, p, qbase, d) * a_old + l_b * a_b)
            _block_store(acc_m, p, qbase, d, m_new)

    ahead = min(lookahead, len(units))
    pending = [scores(u) for u in units[:ahead]]
    stats_prev = None
    for i, unit in enumerate(units):
        if i + ahead < len(units):
            pending.append(scores(units[i + ahead]))
        stats = softmax_pv(unit, pending[i])
        if stats_prev is not None:
            merge(units[i - 1], stats_prev)
        stats_prev = stats
    merge(units[-1], stats_prev)


def _dil_kernel(q_ref, k_ref, v_ref, bias_ref, o_ref, qp, kring, vring, acc_o, acc_m, acc_l):
    t = pl.program_id(1)
    cur = pl.multiple_of((t % 2) * SUPER, SUPER)
    other = pl.multiple_of(SUPER - cur, SUPER)

    @pl.when(t == 0)
    def _():
        kring[:, SUPER:, :] = jnp.zeros((N_PAIRS, SUPER, LANES), F32)
        vring[:, SUPER:, :] = jnp.zeros((N_PAIRS, SUPER, LANES), F32)

    for p in range(N_PAIRS):
        cols = slice(p * LANES, (p + 1) * LANES)
        qp[p] = q_ref[:, cols].astype(F32) * (HEAD_DIM ** -0.5)
        kring[p, pl.ds(cur, SUPER), :] = k_ref[:, cols].astype(F32)
        vring[p, pl.ds(cur, SUPER), :] = v_ref[:, cols].astype(F32)
    refs = (bias_ref, qp, kring, vring, acc_o, acc_m, acc_l)

    for branch, (window, d) in enumerate(DIL_PATTERNS):
        nblk = MAX_DIL // d
        span = BLK * d

        def block_rows(j, d=d, nblk=nblk, span=span):
            if nblk == 1:
                qbase = pl.multiple_of(j * SUB, SUB)
                return qbase, cur + qbase, other + qbase, t > 0
            r = j // nblk
            nb = j % nblk
            qbase = pl.multiple_of(nb * span + r * SUB, SUB)
            kpbase = pl.multiple_of(
                jnp.where(nb == 0, other + (nblk - 1) * span + r * SUB, cur + qbase - span), SUB)
            return qbase, cur + qbase, kpbase, (t > 0) | (nb > 0)

        unroll, lookahead = DIL_UNROLL[branch], DIL_LOOKAHEAD[branch]

        def body(i, carry, branch=branch, d=d, block_rows=block_rows, unroll=unroll, lookahead=lookahead):
            blocks = [block_rows(i * unroll + u) for u in range(unroll)]
            _dil_blocks(refs, branch, d, blocks, first=(branch == 0), lookahead=lookahead)
            return carry

        lax.fori_loop(0, SUPER // BLK // unroll, body, 0)

    def finish(i, carry):
        rows = pl.ds(pl.multiple_of(i * BLK, BLK), BLK)
        for p in range(N_PAIRS):
            o_ref[rows, p * LANES:(p + 1) * LANES] = (acc_o[p, rows, :] / acc_l[p, rows, :]).astype(BF16)
        return carry

    lax.fori_loop(0, SUPER // BLK, finish, 0)


def _dilated_bias():
    out = []
    i = np.arange(BLK)
    for _, d in DIL_PATTERNS:
        n_c = MAX_DIL // d
        pos = (BLK // d) * (i // (SUB * n_c)) + n_c * (i % SUB) + (i // SUB) % n_c
        prev_ok = pos[None, :] >= pos[:, None]
        cur_ok = pos[None, :] <= pos[:, None]
        out.append(np.where(np.concatenate([prev_ok, cur_ok], axis=1), 0.0, -np.inf))
    return np.stack(out).astype(np.float32)


def _dilated_attention(qkv, batch, seq):
    n = batch * seq
    tiles = seq // SUPER
    bias = _dilated_bias()
    spec = lambda which: pl.BlockSpec((SUPER, A_WIDTH), lambda b, t: (b * tiles + t, which))
    tile_f32 = pltpu.VMEM((N_PAIRS, SUPER, LANES), F32)
    ring_f32 = pltpu.VMEM((N_PAIRS, 2 * SUPER, LANES), F32)
    return pl.pallas_call(
        _dil_kernel,
        grid=(batch, tiles),
        in_specs=[spec(0), spec(1), spec(2), pl.BlockSpec(bias.shape, lambda b, t: (0, 0, 0))],
        out_specs=pl.BlockSpec((SUPER, A_WIDTH), lambda b, t: (b * tiles + t, 0)),
        out_shape=jax.ShapeDtypeStruct((n, A_WIDTH), BF16),
        scratch_shapes=[tile_f32, ring_f32, ring_f32, tile_f32, tile_f32, tile_f32],
        compiler_params=_params("parallel", "arbitrary"),
        name="dilated_attention",
    )(qkv, qkv, qkv, bias)


def _rotate(t, cos, sin_signed):
    return t * cos + pltpu.roll(t, HEAD_DIM, 1) * sin_signed


PROJ_PIECES = (
    (QKV_A + UV_B, 512), (QKV_A + UV_B + 512, 512), (QKV_A + UV_B + 1024, 512),
    (QKV_A, UV_B),
    (0, 640), (640, QKV_A - 640))


def _inmix_body(x_ref, g_ref, w_ref, perm_ref, cos_ref, sin_ref, sb_ref, lg_ref, lb_ref,
                dec_ref, qdec_ref, kdec_ref, cdm_ref, a_ref, yb_ref, yc_ref,
                zb_w, zc_w, zb_r, zc_r, state_ref, wt_ref):
    lo = _lane_lo((BLK, LANES))
    row = lax.broadcasted_iota(jnp.int32, (BLK, BLK), 0)
    col = lax.broadcasted_iota(jnp.int32, (BLK, BLK), 1)
    same_head = ((row >> 5) & 1) == (col >> 6)
    lane = lax.broadcasted_iota(jnp.int32, (BLK, LANES), 1)
    qk_head0 = (lane & (HEAD_DIM // 2)) == 0
    inv_n = 1.0 / HEAD_DIM

    def sgu_pre(rows):
        u = jax.nn.gelu(zb_r[rows, :B_WIDTH])
        v = jax.nn.gelu(zb_r[rows, B_WIDTH:])
        mu = jnp.mean(v, axis=-1, keepdims=True)
        var = jnp.mean(jnp.square(v - mu), axis=-1, keepdims=True)
        v = (v - mu) * lax.rsqrt(var + EPS) * lg_ref[...] + lb_ref[...]
        return u, v.astype(BF16)

    def sgu_post(rows, pre):
        u, v = pre
        for gp in range(B_WIDTH // LANES):
            cols = slice(gp * LANES, (gp + 1) * LANES)
            vp = v[:, cols]
            v2 = jnp.concatenate([jnp.where(lo, vp, jnp.zeros_like(vp)),
                                  jnp.where(lo, jnp.zeros_like(vp), vp)], axis=0)
            gate = _dot(wt_ref[gp], v2) + sb_ref[:, cols]
            yb_ref[rows, cols] = (u[:, cols] * gate).astype(BF16)

    def ret_scores(row_list):
        work = [(rows, p) for rows in row_list for p in range(N_PAIRS)]
        out = []
        for rows, p in work:
            cols = slice(p * LANES, (p + 1) * LANES)
            cos, sin = cos_ref[rows, :], sin_ref[rows, :]
            q = _rotate(zc_r[rows, p * LANES:(p + 1) * LANES], cos, sin)
            k = _rotate(zc_r[rows, C_WIDTH + p * LANES:C_WIDTH + (p + 1) * LANES], cos, sin)
            vb = zc_r[rows, 2 * C_WIDTH + p * LANES:2 * C_WIDTH + (p + 1) * LANES].astype(BF16)
            qb, kb = q.astype(BF16), k.astype(BF16)
            k2 = jnp.concatenate([jnp.where(qk_head0, kb, jnp.zeros_like(kb)),
                                  jnp.where(qk_head0, jnp.zeros_like(kb), kb)], axis=0)
            scores = _dot_nt(qb, k2)
            kv = _dot_tn((k * kdec_ref[:, cols]).astype(BF16), vb)
            out.append((scores, kv, (q * qdec_ref[:, cols]).astype(BF16), vb))
        return work, out

    def ret_values(work, stage1):
        out = []
        for (rows, p), (scores, kv, qd, vb) in zip(work, stage1):
            state = state_ref[p]
            y_x = _dot(qd, state.astype(BF16))
            state_ref[p] = state * cdm_ref[p] + jnp.where(same_head, kv, 0.0)
            v2 = jnp.concatenate([jnp.where(lo, vb, jnp.zeros_like(vb)),
                                  jnp.where(lo, jnp.zeros_like(vb), vb)], axis=0)
            y_in = _dot((scores * dec_ref[p]).astype(BF16), v2)
            out.append((y_in, y_x))
        return out

    def ret_norm(work, stage2):
        for (rows, p), (y_in, y_x) in zip(work, stage2):
            cols = slice(p * LANES, (p + 1) * LANES)
            y = y_in + y_x
            mu_lo = jnp.sum(jnp.where(lo, y, 0.0), axis=-1, keepdims=True) * inv_n
            mu_hi = jnp.sum(jnp.where(lo, 0.0, y), axis=-1, keepdims=True) * inv_n
            yc_ = y - jnp.where(lo, mu_lo, mu_hi)
            sq = yc_ * yc_
            var_lo = jnp.sum(jnp.where(lo, sq, 0.0), axis=-1, keepdims=True) * inv_n
            var_hi = jnp.sum(jnp.where(lo, 0.0, sq), axis=-1, keepdims=True) * inv_n
            yn = yc_ * lax.rsqrt(jnp.where(lo, var_lo, var_hi) + EPS)
            gate = zc_r[rows, 3 * C_WIDTH + p * LANES:3 * C_WIDTH + (p + 1) * LANES]
            yc_ref[rows, cols] = (jax.nn.silu(gate) * yn).astype(BF16)

    def project(lhs, piece):
        first, width = PROJ_PIECES[piece]
        y = _dot(lhs, w_ref[:, first:first + width])
        if first >= QKV_A + UV_B:
            zc_w[:, first - QKV_A - UV_B:first - QKV_A - UV_B + width] = y
        elif first >= QKV_A:
            zb_w[...] = y
        else:
            a_ref[:, first:first + width] = y.astype(BF16)

    tm = x_ref.shape[0]
    chunks = [slice(c * BLK, (c + 1) * BLK) for c in range(tm // BLK)]
    groups = [chunks[g:g + MIX_UNROLL] for g in range(0, len(chunks), MIX_UNROLL)]
    assert len(groups) == 2 and len(PROJ_PIECES) == 6

    h = _rms(x_ref[...], g_ref[0:1, :]).astype(BF16)
    hp = jnp.concatenate([_dot(perm_ref[...], h[c]).astype(BF16) for c in chunks], axis=0)
    project(h, 0)
    pre = [sgu_pre(rows) for rows in groups[0]]
    work0, s1 = ret_scores(groups[0])
    project(h, 1)
    for rows, t in zip(groups[0], pre):
        sgu_post(rows, t)
    s2 = ret_values(work0, s1)
    project(h, 2)
    ret_norm(work0, s2)
    pre = [sgu_pre(rows) for rows in groups[1]]
    work1, s1 = ret_scores(groups[1])
    project(h, 3)
    for rows, t in zip(groups[1], pre):
        sgu_post(rows, t)
    s2 = ret_values(work1, s1)
    project(hp, 4)
    ret_norm(work1, s2)
    project(hp, 5)


def _inmix_kernel(x_ref, g_ref, w_ref, perm_ref, cos_ref, sin_ref, sw_ref, sb_ref, lg_ref, lb_ref,
                  dec_ref, qdec_ref, kdec_ref, cdm_ref, a_ref, yb_ref, yc_ref,
                  zb0, zc0, zb1, zc1, state_ref, wt_ref, *, tiles_per_batch):
    i = pl.program_id(0)

    @pl.when(i == 0)
    def _():
        zb1[...] = jnp.zeros_like(zb1)
        zc1[...] = jnp.zeros_like(zc1)
        row = lax.broadcasted_iota(jnp.int32, (BLK, BLK), 0)
        col = lax.broadcasted_iota(jnp.int32, (BLK, BLK), 1)
        for gp in range(SGU_GROUPS // 2):
            wt_ref[gp] = jnp.concatenate([jnp.where(row >= col, sw_ref[2 * gp + e], 0.0)
                                          for e in range(2)], axis=1).astype(BF16)

    @pl.when((i == 0) | ((i - 1) % tiles_per_batch == 0))
    def _():
        state_ref[...] = jnp.zeros_like(state_ref)

    common = (x_ref, g_ref, w_ref, perm_ref, cos_ref, sin_ref, sb_ref, lg_ref, lb_ref,
              dec_ref, qdec_ref, kdec_ref, cdm_ref, a_ref, yb_ref, yc_ref)

    @pl.when(i % 2 == 0)
    def _():
        _inmix_body(*common, zb0, zc0, zb1, zc1, state_ref, wt_ref)

    @pl.when(i % 2 == 1)
    def _():
        _inmix_body(*common, zb1, zc1, zb0, zc0, state_ref, wt_ref)


def _qk_lane_head():
    lane = np.arange(C_WIDTH)
    return 2 * (lane // LANES) + ((lane >> 5) & 1)


def _qk_reorder(w):
    lead = w.shape[:-1]
    w = w.reshape(lead + (N_PAIRS, 2, 2, HEAD_DIM // 2))
    return jnp.swapaxes(w, -3, -2).reshape(lead + (C_WIDTH,))


def _inproj_weight(w):
    q0 = QKV_A + UV_B
    w = w.astype(BF16)
    return jnp.concatenate([w[..., :q0], _qk_reorder(w[..., q0:q0 + C_WIDTH]),
                            _qk_reorder(w[..., q0 + C_WIDTH:q0 + 2 * C_WIDTH]),
                            w[..., q0 + 2 * C_WIDTH:]], axis=-1)


def _retention_tables():
    f32 = np.float32
    heads = np.arange(RET_HEADS, dtype=f32)
    log_g = np.log1p(-np.power(f32(2.0), f32(-5.0) - heads)).astype(f32)
    idx = np.arange(BLK, dtype=f32)
    rel = idx[:, None] - idx[None, :]
    scale = f32(HEAD_DIM ** -0.5)
    decay = np.where(rel[None] >= 0,
                     np.exp(np.maximum(rel, f32(0.0))[None] * log_g[:, None, None]), f32(0.0)) * scale
    k_dec = np.exp((BLK - 1 - idx)[:, None] * log_g[None, :]) * scale
    q_dec = np.exp((idx + f32(1.0))[:, None] * log_g[None, :])
    lane_head = _qk_lane_head()
    k_dec = k_dec[:, lane_head]
    q_dec = q_dec[:, lane_head]
    chunk_decay = np.exp(f32(BLK) * log_g)
    per_row = chunk_decay[lane_head].reshape(N_PAIRS, LANES)
    row_head = (np.arange(LANES) >> 5) & 1
    col_head = np.arange(LANES) >> 6
    same = row_head[:, None] == col_head[None, :]
    cdm = np.where(same[None], per_row[:, :, None], f32(0.0))
    decay = np.concatenate([decay[0::2], decay[1::2]], axis=2)
    return tuple(t.astype(f32) for t in (decay, q_dec, k_dec, cdm))


def _inproj_mixers(x, g, w, perm, cos, sin, sgu_w, sgu_bias, ln_g, ln_b, l, batch, seq):
    n = x.shape[0]
    tm = ROW_TILE
    nt = n // tm
    decay, q_dec, k_dec, cdm = _retention_tables()
    cur = lambda i: (jnp.minimum(i, nt - 1), 0)
    prev = lambda i: (jnp.maximum(i - 1, 0), 0)
    const2 = lambda i: (0, 0)
    const3 = lambda i: (0, 0, 0)
    return pl.pallas_call(
        functools.partial(_inmix_kernel, tiles_per_batch=seq // tm),
        grid=(nt + 1,),
        in_specs=[pl.BlockSpec((tm, D_MODEL), cur),
                  _layer(g, l), _layer(w, l),
                  pl.BlockSpec(perm.shape, const2),
                  pl.BlockSpec((tm, LANES), prev),
                  pl.BlockSpec((tm, LANES), prev),
                  _layer(sgu_w, l), _layer(sgu_bias, l), _layer(ln_g, l), _layer(ln_b, l),
                  pl.BlockSpec(decay.shape, const3),
                  pl.BlockSpec(q_dec.shape, const2),
                  pl.BlockSpec(k_dec.shape, const2),
                  pl.BlockSpec(cdm.shape, const3)],
        out_specs=[pl.BlockSpec((tm, QKV_A), cur),
                   pl.BlockSpec((tm, B_WIDTH), prev),
                   pl.BlockSpec((tm, C_WIDTH), prev)],
        out_shape=[jax.ShapeDtypeStruct((n, QKV_A), BF16),
                   jax.ShapeDtypeStruct((n, B_WIDTH), BF16),
                   jax.ShapeDtypeStruct((n, C_WIDTH), BF16)],
        scratch_shapes=[pltpu.VMEM((tm, UV_B), F32), pltpu.VMEM((tm, QKVG_C), F32),
                        pltpu.VMEM((tm, UV_B), F32), pltpu.VMEM((tm, QKVG_C), F32),
                        pltpu.VMEM((N_PAIRS, LANES, LANES), F32),
                        pltpu.VMEM((SGU_GROUPS // 2, BLK, 2 * BLK), BF16)],
        compiler_params=_params("arbitrary"),
        name="inproj_mixers",
    )(x, g, w, perm, cos, sin, sgu_w, sgu_bias, ln_g, ln_b, decay, q_dec, k_dec, cdm)


def _kv_kernel(m_ref, g_ref, w_ref, kv_ref):
    h = _rms(m_ref[...], g_ref[6:7, :]).astype(BF16)
    kv_ref[...] = _dot(h, w_ref[...]).astype(BF16)


def _memory_kv(mem, g, w, l):
    n = mem.shape[0]
    return pl.pallas_call(
        _kv_kernel,
        grid=(n // MEM_LEN,),
        in_specs=[pl.BlockSpec((MEM_LEN, D_MODEL), lambda i: (i, 0)),
                  _layer(g, l), _layer(w, l)],
        out_specs=pl.BlockSpec((MEM_LEN, 2 * D_MODEL), lambda i: (i, 0)),
        out_shape=jax.ShapeDtypeStruct((n, 2 * D_MODEL), BF16),
        compiler_params=_params("parallel"),
        name="memory_kv",
    )(mem, g, w)


def _post_mix_kernel(ya_ref, yb_ref, yc_ref, x_ref, kv_ref, g_ref, w_ref, unperm_ref, wq_ref, wo_ref,
                     out_ref):
    tm = x_ref.shape[0]
    halves = [slice(i * (tm // 2), (i + 1) * (tm // 2)) for i in range(2)]
    heads = [slice(hd * X_HEAD_DIM, (hd + 1) * X_HEAD_DIM) for hd in range(X_HEADS)]
    scale = X_HEAD_DIM ** -0.5

    def unpermute(rows):
        n_grp = (rows.stop - rows.start) // BLK
        return jnp.concatenate(
            [_dot(unperm_ref[...], ya_ref[rows.start + g * BLK:rows.start + (g + 1) * BLK, :]).astype(BF16)
             for g in range(n_grp)], axis=0)

    ya = [unpermute(rows) for rows in halves]
    proj = [_dot(jnp.concatenate([ya[i], yb_ref[rows, :], yc_ref[rows, :]], axis=1), w_ref[...])
            for i, rows in enumerate(halves)]
    x1 = [x_ref[rows, :] + _rms(proj[i], g_ref[1:2, :]) for i, rows in enumerate(halves)]
    q = [_dot(_rms(x1[i], g_ref[2:3, :]).astype(BF16), wq_ref[...]).astype(BF16) for i in range(2)]
    scores = [[_dot_nt(q[i][:, cols], kv_ref[:, cols]) for cols in heads] for i in range(2)]
    att = []
    for i in range(2):
        outs = []
        for hd, s in enumerate(scores[i]):
            s = s * scale
            m = jnp.max(s, axis=-1, keepdims=True)
            p = jnp.exp(s - m)
            p = p / jnp.sum(p, axis=-1, keepdims=True)
            v = kv_ref[:, D_MODEL + hd * X_HEAD_DIM:D_MODEL + (hd + 1) * X_HEAD_DIM]
            outs.append(_dot(p.astype(BF16), v).astype(BF16))
        att.append(jnp.concatenate(outs, axis=1))
    acc = [_dot(att[i], wo_ref[...]) for i in range(2)]
    for i, rows in enumerate(halves):
        out_ref[rows, :] = x1[i] + _rms(acc[i], g_ref[3:4, :])


def _post_mix(ya, yb, yc, x, kv, g, w_out, unperm, wq, wo, l, batch, seq):
    n = x.shape[0]
    tm = ROW_TILE
    per_batch = seq // tm
    row = lambda i: (i, 0)
    const = lambda i: (0, 0)
    return pl.pallas_call(
        _post_mix_kernel,
        grid=(n // tm,),
        in_specs=[pl.BlockSpec((tm, A_WIDTH), row),
                  pl.BlockSpec((tm, B_WIDTH), row),
                  pl.BlockSpec((tm, C_WIDTH), row),
                  pl.BlockSpec((tm, D_MODEL), row),
                  pl.BlockSpec((MEM_LEN, 2 * D_MODEL), lambda i: (i // per_batch, 0)),
                  _layer(g, l), _layer(w_out, l),
                  pl.BlockSpec(unperm.shape, const),
                  _layer(wq, l), _layer(wo, l)],
        out_specs=pl.BlockSpec((tm, D_MODEL), row),
        out_shape=jax.ShapeDtypeStruct((n, D_MODEL), F32),
        compiler_params=_params("parallel"),
        name="outproj_cross",
    )(ya, yb, yc, x, kv, g, w_out, unperm, wq, wo)


def _mlp_kernel(x_ref, g_ref, wu_ref, wd_ref, out_ref):
    x = x_ref[...]
    h = _rms(x, g_ref[4:5, :]).astype(BF16)
    acc = jnp.zeros(x.shape, F32)
    for c in range(D_FF // FF_CHUNK):
        cols = slice(c * FF_CHUNK, (c + 1) * FF_CHUNK)
        f = jnp.square(jnp.maximum(_dot(h, wu_ref[:, cols]), 0.0))
        acc += _dot(f.astype(BF16), wd_ref[cols, :])
    out_ref[...] = x + _rms(acc, g_ref[5:6, :])


def _mlp(x, g, w_up, w_down, l):
    n = x.shape[0]
    tm = MLP_TILE
    row = lambda i: (i, 0)
    const = lambda i: (0, 0)
    return pl.pallas_call(
        _mlp_kernel,
        grid=(n // tm,),
        in_specs=[pl.BlockSpec((tm, D_MODEL), row),
                  _layer(g, l),
                  _layer(w_up, l, pipeline_mode=pl.Buffered(1)),
                  _layer(w_down, l, pipeline_mode=pl.Buffered(1))],
        out_specs=pl.BlockSpec((tm, D_MODEL), row),
        out_shape=jax.ShapeDtypeStruct((n, D_MODEL), F32),
        compiler_params=_params("parallel"),
        name="mlp",
    )(x, g, w_up, w_down)


def kernel(x, mem, positions, norm_g, w_in, sgu_w, sgu_b, sgu_ln_g, sgu_ln_b,
           w_out, x_wq, x_wkv, x_wo, w_up, w_down):
    batch, seq, _ = x.shape
    assert x.shape == (batch, seq, D_MODEL) and seq % (BLK * DIL_PATTERNS[-1][1]) == 0
    assert (batch * seq) % ROW_TILE == 0 and seq % ROW_TILE == 0
    assert mem.shape == (batch, MEM_LEN, D_MODEL) and w_in.shape == (DEPTH, D_MODEL, IN_WIDTH)
    xf = x.reshape(batch * seq, D_MODEL)
    memf = mem.reshape(batch * MEM_LEN, D_MODEL)
    cos, sin = _rope_tables(positions)
    perm = _group_permutation()
    w_in_b, w_out_b = _inproj_weight(w_in), w_out.astype(BF16)
    wq_b, wkv_b, wo_b = x_wq.astype(BF16), x_wkv.astype(BF16), x_wo.astype(BF16)
    w_up_b, w_down_b = w_up.astype(BF16), w_down.astype(BF16)
    sgu_bias = jnp.repeat(jnp.swapaxes(sgu_b, 1, 2), HEAD_DIM, axis=2)
    ln_g = sgu_ln_g.reshape(DEPTH, 1, B_WIDTH)
    ln_b = sgu_ln_b.reshape(DEPTH, 1, B_WIDTH)
    for l in range(DEPTH):
        za, yb, yc = _inproj_mixers(xf, norm_g, w_in_b, perm, cos, sin, sgu_w, sgu_bias, ln_g, ln_b,
                                    l, batch, seq)
        ya = _dilated_attention(za, batch, seq)
        kv = _memory_kv(memf, norm_g, wkv_b, l)
        xf = _post_mix(ya, yb, yc, xf, kv, norm_g, w_out_b, perm.T, wq_b, wo_b, l, batch, seq)
        xf = _mlp(xf, norm_g, w_up_b, w_down_b, l)
    return xf.reshape(batch, seq, D_MODEL)
```

```python
import functools

import jax
import jax.numpy as jnp
import numpy as np
from jax import lax
from jax.experimental import pallas as pl
from jax.experimental.pallas import tpu as pltpu

F32 = jnp.float32
BF16 = jnp.bfloat16

D_MODEL = 1024
DEPTH = 2
MEM_LEN = 256
HEAD_DIM = 64
DIL_HEADS = 6
DIL_PATTERNS = ((128, 1), (512, 4), (2048, 16))
BLK = 128
SGU_GROUPS = 4
RET_HEADS = 6
ROPE_BASE = 10000.0
A_WIDTH = DIL_HEADS * HEAD_DIM
B_WIDTH = SGU_GROUPS * HEAD_DIM
C_WIDTH = RET_HEADS * HEAD_DIM
QKV_A = 3 * A_WIDTH
UV_B = 2 * B_WIDTH
QKVG_C = 4 * C_WIDTH
IN_WIDTH = QKV_A + UV_B + QKVG_C
X_HEADS = 4
X_HEAD_DIM = D_MODEL // X_HEADS
D_FF = 4 * D_MODEL
EPS = 1e-6

LANES = 128
N_PAIRS = A_WIDTH // LANES
VMEM_LIMIT = 56 * 1024 * 1024

SUPER = BLK * DIL_PATTERNS[-1][1]

ROW_TILE = 512
DIL_UNROLL = (4, 4, 4)
DIL_LOOKAHEAD = (12, 3, 3)
MIX_UNROLL = 2
MLP_TILE = 1024
FF_CHUNK = 1024


def _params(*sem):
    return pltpu.CompilerParams(dimension_semantics=sem, vmem_limit_bytes=VMEM_LIMIT)


def _layer(arr, l, **kw):
    zeros = (0,) * (arr.ndim - 1)
    return pl.BlockSpec((None,) + arr.shape[1:], lambda *_: (l,) + zeros, **kw)


def _rms(x, g):
    ms = jnp.mean(x * x, axis=-1, keepdims=True)
    return x * lax.rsqrt(ms + EPS) * g


def _dot(a, b):
    return jnp.dot(a, b, preferred_element_type=F32)


def _dot_nt(a, b):
    return lax.dot_general(a, b, (((1,), (1,)), ((), ())), preferred_element_type=F32)


def _dot_tn(a, b):
    return lax.dot_general(a, b, (((0,), (0,)), ((), ())), preferred_element_type=F32)


def _lane_lo(shape):
    lane = lax.broadcasted_iota(jnp.int32, shape, len(shape) - 1)
    return (lane & HEAD_DIM) == 0


ROPE_PACK = LANES // (HEAD_DIM // 2)


def _rope_kernel(pos_ref, inv_ref, cos_ref, sin_ref):
    half = HEAD_DIM // 2
    ang = pos_ref[...].astype(F32) * inv_ref[...]
    cos, sin = jnp.cos(ang), jnp.sin(ang)
    lane = lax.broadcasted_iota(jnp.int32, ang.shape, 1)
    sign = jnp.where(lane < HEAD_DIM, -1.0, 1.0)
    rows = ang.shape[0]
    for t in range(ROPE_PACK):
        mine = (lane >= t * half) & (lane < (t + 1) * half)
        for table, out_ref, scale in ((cos, cos_ref, None), (sin, sin_ref, sign)):
            y = jnp.where(mine, table, 0.0)
            y = y + pltpu.roll(y, 2 * half, 1)
            y = y + pltpu.roll(y, half, 1)
            out_ref[pl.ds(t, rows, stride=ROPE_PACK), :] = y if scale is None else y * scale


def _rope_tables(positions):
    n = positions.size
    half = HEAD_DIM // 2
    inv = 1.0 / (ROPE_BASE ** jnp.linspace(0.0, 1.0, half, dtype=F32))
    inv = jnp.tile(inv, ROPE_PACK).reshape(1, LANES)
    pos = jnp.repeat(positions.reshape(n // ROPE_PACK, ROPE_PACK), half, axis=1)
    tm = 2048
    return pl.pallas_call(
        _rope_kernel,
        grid=(n // tm,),
        in_specs=[pl.BlockSpec((tm // ROPE_PACK, LANES), lambda i: (i, 0)),
                  pl.BlockSpec((1, LANES), lambda i: (0, 0))],
        out_specs=[pl.BlockSpec((tm, LANES), lambda i: (i, 0))] * 2,
        out_shape=[jax.ShapeDtypeStruct((n, LANES), F32)] * 2,
        compiler_params=_params("parallel"),
        name="rope_tables",
    )(pos, inv)


MAX_DIL = DIL_PATTERNS[-1][1]
SUB = 8


def _group_permutation():
    i = np.arange(BLK)
    src = MAX_DIL * (i % SUB) + i // SUB
    return jnp.asarray(src[:, None] == np.arange(BLK)[None, :], BF16)


def _run_offsets(d):
    return [BLK * g + SUB * d * c for g in range(d) for c in range(MAX_DIL // d)]


def _block_load(ref, p, base, d):
    if d == 1:
        return ref[p, pl.ds(pl.multiple_of(base, BLK), BLK), :]
    return jnp.concatenate(
        [ref[p, pl.ds(pl.multiple_of(base + off, SUB), SUB), :] for off in _run_offsets(d)], axis=0)


def _block_store(ref, p, base, d, val):
    if d == 1:
        ref[p, pl.ds(pl.multiple_of(base, BLK), BLK), :] = val
        return
    for i, off in enumerate(_run_offsets(d)):
        ref[p, pl.ds(pl.multiple_of(base + off, SUB), SUB), :] = val[i * SUB:(i + 1) * SUB]


def _dil_blocks(refs, branch, d, blocks, first, lookahead):
    bias_ref, qp, kring, vring, acc_o, acc_m, acc_l = refs
    lo = _lane_lo((BLK, LANES))
    units = [(blk, p) for blk in blocks for p in range(N_PAIRS)]

    def scores(unit):
        (qbase, kcbase, kpbase, _), p = unit
        q = _block_load(qp, p, qbase, d).astype(BF16)
        k = jnp.concatenate([_block_load(kring, p, kpbase, d), _block_load(kring, p, kcbase, d)],
                            axis=0).astype(BF16)
        return [_dot_nt(jnp.where(lo if e == 0 else ~lo, q, jnp.zeros_like(q)), k) for e in range(2)]

    def softmax_pv(unit, s_pair):
        (qbase, kcbase, kpbase, has_prev), p = unit
        bias_prev = jnp.where(has_prev, bias_ref[branch, :, :BLK], -jnp.inf)
        bias_cur = bias_ref[branch, :, BLK:]
        v = jnp.concatenate([_block_load(vring, p, kpbase, d), _block_load(vring, p, kcbase, d)],
                            axis=0).astype(BF16)
        o_h, m_h, l_h = [], [], []
        for s in s_pair:
            sp = s[:, :BLK] + bias_prev
            sc = s[:, BLK:] + bias_cur
            m = jnp.max(jnp.maximum(sp, sc), axis=-1, keepdims=True)
            pp = jnp.exp(sp - m)
            pc = jnp.exp(sc - m)
            l_h.append(jnp.sum(pp + pc, axis=-1, keepdims=True))
            m_h.append(m)
            o_h.append(_dot(jnp.concatenate([pp, pc], axis=1).astype(BF16), v))
        return o_h, m_h, l_h

    def merge(unit, stats):
        (qbase, _, _, _), p = unit
        o_h, m_h, l_h = stats
        o_b = jnp.where(lo, o_h[0], o_h[1])
        m_b = jnp.where(lo, m_h[0], m_h[1])
        l_b = jnp.where(lo, l_h[0], l_h[1])
        if first:
            _block_store(acc_o, p, qbase, d, o_b)
            _block_store(acc_m, p, qbase, d, m_b)
            _block_store(acc_l, p, qbase, d, l_b)
        else:
            m_old = _block_load(acc_m, p, qbase, d)
            m_new = jnp.maximum(m_old, m_b)
            a_old = jnp.exp(m_old - m_new)
            a_b = jnp.exp(m_b - m_new)
            _block_store(acc_o, p, qbase, d, _block_load(acc_o, p, qbase, d) * a_old + o_b * a_b)
            _block_store(acc_l, p, qbase, d, _block_load(acc_l, p, qbase, d) * a_old + l_b * a_b)
            _block_store(acc_m, p, qbase, d, m_new)

    ahead = min(lookahead, len(units))
    pending = [scores(u) for u in units[:ahead]]
    stats_prev = None
    for i, unit in enumerate(units):
        if i + ahead < len(units):
            pending.append(scores(units[i + ahead]))
        stats = softmax_pv(unit, pending[i])
        if stats_prev is not None:
            merge(units[i - 1], stats_prev)
        stats_prev = stats
    merge(units[-1], stats_prev)


def _dil_kernel(q_ref, k_ref, v_ref, bias_ref, o_ref, qp, kring, vring, acc_o, acc_m, acc_l):
    t = pl.program_id(1)
    cur = pl.multiple_of((t % 2) * SUPER, SUPER)
    other = pl.multiple_of(SUPER - cur, SUPER)

    @pl.when(t == 0)
    def _():
        kring[:, SUPER:, :] = jnp.zeros((N_PAIRS, SUPER, LANES), F32)
        vring[:, SUPER:, :] = jnp.zeros((N_PAIRS, SUPER, LANES), F32)

    for p in range(N_PAIRS):
        cols = slice(p * LANES, (p + 1) * LANES)
        qp[p] = q_ref[:, cols].astype(F32) * (HEAD_DIM ** -0.5)
        kring[p, pl.ds(cur, SUPER), :] = k_ref[:, cols].astype(F32)
        vring[p, pl.ds(cur, SUPER), :] = v_ref[:, cols].astype(F32)
    refs = (bias_ref, qp, kring, vring, acc_o, acc_m, acc_l)

    for branch, (window, d) in enumerate(DIL_PATTERNS):
        nblk = MAX_DIL // d
        span = BLK * d

        def block_rows(j, d=d, nblk=nblk, span=span):
            if nblk == 1:
                qbase = pl.multiple_of(j * SUB, SUB)
                return qbase, cur + qbase, other + qbase, t > 0
            r = j // nblk
            nb = j % nblk
            qbase = pl.multiple_of(nb * span + r * SUB, SUB)
            kpbase = pl.multiple_of(
                jnp.where(nb == 0, other + (nblk - 1) * span + r * SUB, cur + qbase - span), SUB)
            return qbase, cur + qbase, kpbase, (t > 0) | (nb > 0)

        unroll, lookahead = DIL_UNROLL[branch], DIL_LOOKAHEAD[branch]

        def body(i, carry, branch=branch, d=d, block_rows=block_rows, unroll=unroll, lookahead=lookahead):
            blocks = [block_rows(i * unroll + u) for u in range(unroll)]
            _dil_blocks(refs, branch, d, blocks, first=(branch == 0), lookahead=lookahead)
            return carry

        lax.fori_loop(0, SUPER // BLK // unroll, body, 0)

    def finish(i, carry):
        rows = pl.ds(pl.multiple_of(i * BLK, BLK), BLK)
        for p in range(N_PAIRS):
            o_ref[rows, p * LANES:(p + 1) * LANES] = (acc_o[p, rows, :] / acc_l[p, rows, :]).astype(BF16)
        return carry

    lax.fori_loop(0, SUPER // BLK, finish, 0)


def _dilated_bias():
    out = []
    i = np.arange(BLK)
    for _, d in DIL_PATTERNS:
        n_c = MAX_DIL // d
        pos = (BLK // d) * (i // (SUB * n_c)) + n_c * (i % SUB) + (i // SUB) % n_c
        prev_ok = pos[None, :] >= pos[:, None]
        cur_ok = pos[None, :] <= pos[:, None]
        out.append(np.where(np.concatenate([prev_ok, cur_ok], axis=1), 0.0, -np.inf))
    return np.stack(out).astype(np.float32)


def _dilated_attention(qkv, batch, seq):
    n = batch * seq
    tiles = seq // SUPER
    bias = _dilated_bias()
    spec = lambda which: pl.BlockSpec((SUPER, A_WIDTH), lambda b, t: (b * tiles + t, which))
    tile_f32 = pltpu.VMEM((N_PAIRS, SUPER, LANES), F32)
    ring_f32 = pltpu.VMEM((N_PAIRS, 2 * SUPER, LANES), F32)
    return pl.pallas_call(
        _dil_kernel,
        grid=(batch, tiles),
        in_specs=[spec(0), spec(1), spec(2), pl.BlockSpec(bias.shape, lambda b, t: (0, 0, 0))],
        out_specs=pl.BlockSpec((SUPER, A_WIDTH), lambda b, t: (b * tiles + t, 0)),
        out_shape=jax.ShapeDtypeStruct((n, A_WIDTH), BF16),
        scratch_shapes=[tile_f32, ring_f32, ring_f32, tile_f32, tile_f32, tile_f32],
        compiler_params=_params("parallel", "arbitrary"),
        name="dilated_attention",
    )(qkv, qkv, qkv, bias)


def _rotate(t, cos, sin_signed):
    return t * cos + pltpu.roll(t, HEAD_DIM, 1) * sin_signed


PROJ_PIECES = (
    (QKV_A + UV_B, 512), (QKV_A + UV_B + 512, 512), (QKV_A + UV_B + 1024, 512),
    (QKV_A, UV_B),
    (0, 640), (640, QKV_A - 640))


def _inmix_body(x_ref, g_ref, w_ref, perm_ref, cos_ref, sin_ref, sb_ref, lg_ref, lb_ref,
                dec_ref, qdec_ref, kdec_ref, cdm_ref, a_ref, yb_ref, yc_ref,
                zb_w, zc_w, zb_r, zc_r, state_ref, wt_ref):
    lo = _lane_lo((BLK, LANES))
    row = lax.broadcasted_iota(jnp.int32, (BLK, BLK), 0)
    col = lax.broadcasted_iota(jnp.int32, (BLK, BLK), 1)
    same_head = ((row >> 5) & 1) == (col >> 6)
    lane = lax.broadcasted_iota(jnp.int32, (BLK, LANES), 1)
    qk_head0 = (lane & (HEAD_DIM // 2)) == 0
    inv_n = 1.0 / HEAD_DIM

    def sgu_pre(rows):
        u = jax.nn.gelu(zb_r[rows, :B_WIDTH])
        v = jax.nn.gelu(zb_r[rows, B_WIDTH:])
        mu = jnp.mean(v, axis=-1, keepdims=True)
        var = jnp.mean(jnp.square(v - mu), axis=-1, keepdims=True)
        v = (v - mu) * lax.rsqrt(var + EPS) * lg_ref[...] + lb_ref[...]
        return u, v.astype(BF16)

    def sgu_post(rows, pre):
        u, v = pre
        for gp in range(B_WIDTH // LANES):
            cols = slice(gp * LANES, (gp + 1) * LANES)
            vp = v[:, cols]
            v2 = jnp.concatenate([jnp.where(lo, vp, jnp.zeros_like(vp)),
                                  jnp.where(lo, jnp.zeros_like(vp), vp)], axis=0)
            gate = _dot(wt_ref[gp], v2) + sb_ref[:, cols]
            yb_ref[rows, cols] = (u[:, cols] * gate).astype(BF16)

    def ret_scores(row_list):
        work = [(rows, p) for rows in row_list for p in range(N_PAIRS)]
        out = []
        for rows, p in work:
            cols = slice(p * LANES, (p + 1) * LANES)
            cos, sin = cos_ref[rows, :], sin_ref[rows, :]
            q = _rotate(zc_r[rows, p * LANES:(p + 1) * LANES], cos, sin)
            k = _rotate(zc_r[rows, C_WIDTH + p * LANES:C_WIDTH + (p + 1) * LANES], cos, sin)
            vb = zc_r[rows, 2 * C_WIDTH + p * LANES:2 * C_WIDTH + (p + 1) * LANES].astype(BF16)
            qb, kb = q.astype(BF16), k.astype(BF16)
            k2 = jnp.concatenate([jnp.where(qk_head0, kb, jnp.zeros_like(kb)),
                                  jnp.where(qk_head0, jnp.zeros_like(kb), kb)], axis=0)
            scores = _dot_nt(qb, k2)
            kv = _dot_tn((k * kdec_ref[:, cols]).astype(BF16), vb)
            out.append((scores, kv, (q * qdec_ref[:, cols]).astype(BF16), vb))
        return work, out

    def ret_values(work, stage1):
        out = []
        for (rows, p), (scores, kv, qd, vb) in zip(work, stage1):
            state = state_ref[p]
            y_x = _dot(qd, state.astype(BF16))
            state_ref[p] = state * cdm_ref[p] + jnp.where(same_head, kv, 0.0)
            v2 = jnp.concatenate([jnp.where(lo, vb, jnp.zeros_like(vb)),
                                  jnp.where(lo, jnp.zeros_like(vb), vb)], axis=0)
            y_in = _dot((scores * dec_ref[p]).astype(BF16), v2)
            out.append((y_in, y_x))
        return out

    def ret_norm(work, stage2):
        for (rows, p), (y_in, y_x) in zip(work, stage2):
            cols = slice(p * LANES, (p + 1) * LANES)
            y = y_in + y_x
            mu_lo = jnp.sum(jnp.where(lo, y, 0.0), axis=-1, keepdims=True) * inv_n
            mu_hi = jnp.sum(jnp.where(lo, 0.0, y), axis=-1, keepdims=True) * inv_n
            yc_ = y - jnp.where(lo, mu_lo, mu_hi)
            sq = yc_ * yc_
            var_lo = jnp.sum(jnp.where(lo, sq, 0.0), axis=-1, keepdims=True) * inv_n
            var_hi = jnp.sum(jnp.where(lo, 0.0, sq), axis=-1, keepdims=True) * inv_n
            yn = yc_ * lax.rsqrt(jnp.where(lo, var_lo, var_hi) + EPS)
            gate = zc_r[rows, 3 * C_WIDTH + p * LANES:3 * C_WIDTH + (p + 1) * LANES]
            yc_ref[rows, cols] = (jax.nn.silu(gate) * yn).astype(BF16)

    def project(lhs, piece):
        first, width = PROJ_PIECES[piece]
        y = _dot(lhs, w_ref[:, first:first + width])
        if first >= QKV_A + UV_B:
            zc_w[:, first - QKV_A - UV_B:first - QKV_A - UV_B + width] = y
        elif first >= QKV_A:
            zb_w[...] = y
        else:
            a_ref[:, first:first + width] = y.astype(BF16)

    tm = x_ref.shape[0]
    chunks = [slice(c * BLK, (c + 1) * BLK) for c in range(tm // BLK)]
    groups = [chunks[g:g + MIX_UNROLL] for g in range(0, len(chunks), MIX_UNROLL)]
    assert len(groups) == 2 and len(PROJ_PIECES) == 6

    h = _rms(x_ref[...], g_ref[0:1, :]).astype(BF16)
    hp = jnp.concatenate([_dot(perm_ref[...], h[c]).astype(BF16) for c in chunks], axis=0)
    project(h, 0)
    pre = [sgu_pre(rows) for rows in groups[0]]
    work0, s1 = ret_scores(groups[0])
    project(h, 1)
    for rows, t in zip(groups[0], pre):
        sgu_post(rows, t)
    s2 = ret_values(work0, s1)
    project(h, 2)
    ret_norm(work0, s2)
    pre = [sgu_pre(rows) for rows in groups[1]]
    work1, s1 = ret_scores(groups[1])
    project(h, 3)
    for rows, t in zip(groups[1], pre):
        sgu_post(rows, t)
    s2 = ret_values(work1, s1)
    project(hp, 4)
    ret_norm(work1, s2)
    project(hp, 5)


def _inmix_kernel(x_ref, g_ref, w_ref, perm_ref, cos_ref, sin_ref, sw_ref, sb_ref, lg_ref, lb_ref,
                  dec_ref, qdec_ref, kdec_ref, cdm_ref, a_ref, yb_ref, yc_ref,
                  zb0, zc0, zb1, zc1, state_ref, wt_ref, wb_ref, *, tiles_per_batch):
    i = pl.program_id(0)

    @pl.when(i == 0)
    def _():
        half = HEAD_DIM // 2
        lane = lax.broadcasted_iota(jnp.int32, (BLK, LANES), 1)
        from_right = (lane >= half) & (lane < 2 * half)
        from_left = (lane >= 2 * half) & (lane < 3 * half)
        qk_tiles = range(QKV_A + UV_B, QKV_A + UV_B + 2 * C_WIDTH, LANES)

        def cast_rows(r, carry):
            rows = pl.ds(pl.multiple_of(r * BLK, BLK), BLK)
            for c0 in range(0, IN_WIDTH, LANES):
                t = w_ref[rows, c0:c0 + LANES]
                if c0 in qk_tiles:
                    t = jnp.where(from_right, pltpu.roll(t, LANES - half, 1),
                                  jnp.where(from_left, pltpu.roll(t, half, 1), t))
                wb_ref[rows, c0:c0 + LANES] = t.astype(BF16)
            return carry

        lax.fori_loop(0, D_MODEL // BLK, cast_rows, 0)
        zb1[...] = jnp.zeros_like(zb1)
        zc1[...] = jnp.zeros_like(zc1)
        row = lax.broadcasted_iota(jnp.int32, (BLK, BLK), 0)
        col = lax.broadcasted_iota(jnp.int32, (BLK, BLK), 1)
        for gp in range(SGU_GROUPS // 2):
            wt_ref[gp] = jnp.concatenate([jnp.where(row >= col, sw_ref[2 * gp + e], 0.0)
                                          for e in range(2)], axis=1).astype(BF16)

    @pl.when((i == 0) | ((i - 1) % tiles_per_batch == 0))
    def _():
        state_ref[...] = jnp.zeros_like(state_ref)

    common = (x_ref, g_ref, wb_ref, perm_ref, cos_ref, sin_ref, sb_ref, lg_ref, lb_ref,
              dec_ref, qdec_ref, kdec_ref, cdm_ref, a_ref, yb_ref, yc_ref)

    @pl.when(i % 2 == 0)
    def _():
        _inmix_body(*common, zb0, zc0, zb1, zc1, state_ref, wt_ref)

    @pl.when(i % 2 == 1)
    def _():
        _inmix_body(*common, zb1, zc1, zb0, zc0, state_ref, wt_ref)


def _qk_lane_head():
    lane = np.arange(C_WIDTH)
    return 2 * (lane // LANES) + ((lane >> 5) & 1)


def _retention_tables():
    f32 = np.float32
    heads = np.arange(RET_HEADS, dtype=f32)
    log_g = np.log1p(-np.power(f32(2.0), f32(-5.0) - heads)).astype(f32)
    idx = np.arange(BLK, dtype=f32)
    rel = idx[:, None] - idx[None, :]
    scale = f32(HEAD_DIM ** -0.5)
    decay = np.where(rel[None] >= 0,
                     np.exp(np.maximum(rel, f32(0.0))[None] * log_g[:, None, None]), f32(0.0)) * scale
    k_dec = np.exp((BLK - 1 - idx)[:, None] * log_g[None, :]) * scale
    q_dec = np.exp((idx + f32(1.0))[:, None] * log_g[None, :])
    lane_head = _qk_lane_head()
    k_dec = k_dec[:, lane_head]
    q_dec = q_dec[:, lane_head]
    chunk_decay = np.exp(f32(BLK) * log_g)
    per_row = chunk_decay[lane_head].reshape(N_PAIRS, LANES)
    row_head = (np.arange(LANES) >> 5) & 1
    col_head = np.arange(LANES) >> 6
    same = row_head[:, None] == col_head[None, :]
    cdm = np.where(same[None], per_row[:, :, None], f32(0.0))
    decay = np.concatenate([decay[0::2], decay[1::2]], axis=2)
    return tuple(t.astype(f32) for t in (decay, q_dec, k_dec, cdm))


def _inproj_mixers(x, g, w, perm, cos, sin, sgu_w, sgu_bias, ln_g, ln_b, l, batch, seq):
    n = x.shape[0]
    tm = ROW_TILE
    nt = n // tm
    decay, q_dec, k_dec, cdm = _retention_tables()
    cur = lambda i: (jnp.minimum(i, nt - 1), 0)
    prev = lambda i: (jnp.maximum(i - 1, 0), 0)
    const2 = lambda i: (0, 0)
    const3 = lambda i: (0, 0, 0)
    return pl.pallas_call(
        functools.partial(_inmix_kernel, tiles_per_batch=seq // tm),
        grid=(nt + 1,),
        in_specs=[pl.BlockSpec((tm, D_MODEL), cur),
                  _layer(g, l), _layer(w, l, pipeline_mode=pl.Buffered(1)),
                  pl.BlockSpec(perm.shape, const2),
                  pl.BlockSpec((tm, LANES), prev),
                  pl.BlockSpec((tm, LANES), prev),
                  _layer(sgu_w, l), _layer(sgu_bias, l), _layer(ln_g, l), _layer(ln_b, l),
                  pl.BlockSpec(decay.shape, const3),
                  pl.BlockSpec(q_dec.shape, const2),
                  pl.BlockSpec(k_dec.shape, const2),
                  pl.BlockSpec(cdm.shape, const3)],
        out_specs=[pl.BlockSpec((tm, QKV_A), cur),
                   pl.BlockSpec((tm, B_WIDTH), prev),
                   pl.BlockSpec((tm, C_WIDTH), prev)],
        out_shape=[jax.ShapeDtypeStruct((n, QKV_A), BF16),
                   jax.ShapeDtypeStruct((n, B_WIDTH), BF16),
                   jax.ShapeDtypeStruct((n, C_WIDTH), BF16)],
        scratch_shapes=[pltpu.VMEM((tm, UV_B), F32), pltpu.VMEM((tm, QKVG_C), F32),
                        pltpu.VMEM((tm, UV_B), F32), pltpu.VMEM((tm, QKVG_C), F32),
                        pltpu.VMEM((N_PAIRS, LANES, LANES), F32),
                        pltpu.VMEM((SGU_GROUPS // 2, BLK, 2 * BLK), BF16),
                        pltpu.VMEM((D_MODEL, IN_WIDTH), BF16)],
        compiler_params=_params("arbitrary"),
        name="inproj_mixers",
    )(x, g, w, perm, cos, sin, sgu_w, sgu_bias, ln_g, ln_b, decay, q_dec, k_dec, cdm)


def _kv_kernel(m_ref, g_ref, w_ref, kv_ref):
    h = _rms(m_ref[...], g_ref[6:7, :]).astype(BF16)
    kv_ref[...] = _dot(h, w_ref[...]).astype(BF16)


def _memory_kv(mem, g, w, l):
    n = mem.shape[0]
    return pl.pallas_call(
        _kv_kernel,
        grid=(n // MEM_LEN,),
        in_specs=[pl.BlockSpec((MEM_LEN, D_MODEL), lambda i: (i, 0)),
                  _layer(g, l), _layer(w, l)],
        out_specs=pl.BlockSpec((MEM_LEN, 2 * D_MODEL), lambda i: (i, 0)),
        out_shape=jax.ShapeDtypeStruct((n, 2 * D_MODEL), BF16),
        compiler_params=_params("parallel"),
        name="memory_kv",
    )(mem, g, w)


def _post_mix_kernel(ya_ref, yb_ref, yc_ref, x_ref, kv_ref, g_ref, w_ref, unperm_ref, wq_ref, wo_ref,
                     out_ref, wb_ref):
    @pl.when(pl.program_id(0) == 0)
    def _():
        def cast_rows(r, carry):
            rows = pl.ds(pl.multiple_of(r * BLK, BLK), BLK)
            for k, src in enumerate((w_ref, wq_ref, wo_ref)):
                wb_ref[k, rows, :] = src[rows, :].astype(BF16)
            return carry

        lax.fori_loop(0, D_MODEL // BLK, cast_rows, 0)

    tm = x_ref.shape[0]
    halves = [slice(i * (tm // 2), (i + 1) * (tm // 2)) for i in range(2)]
    heads = [slice(hd * X_HEAD_DIM, (hd + 1) * X_HEAD_DIM) for hd in range(X_HEADS)]
    scale = X_HEAD_DIM ** -0.5

    def unpermute(rows):
        n_grp = (rows.stop - rows.start) // BLK
        return jnp.concatenate(
            [_dot(unperm_ref[...], ya_ref[rows.start + g * BLK:rows.start + (g + 1) * BLK, :]).astype(BF16)
             for g in range(n_grp)], axis=0)

    ya = [unpermute(rows) for rows in halves]
    proj = [_dot(jnp.concatenate([ya[i], yb_ref[rows, :], yc_ref[rows, :]], axis=1), wb_ref[0])
            for i, rows in enumerate(halves)]
    x1 = [x_ref[rows, :] + _rms(proj[i], g_ref[1:2, :]) for i, rows in enumerate(halves)]
    q = [_dot(_rms(x1[i], g_ref[2:3, :]).astype(BF16), wb_ref[1]).astype(BF16) for i in range(2)]
    scores = [[_dot_nt(q[i][:, cols], kv_ref[:, cols]) for cols in heads] for i in range(2)]
    att = []
    for i in range(2):
        outs = []
        for hd, s in enumerate(scores[i]):
            s = s * scale
            m = jnp.max(s, axis=-1, keepdims=True)
            p = jnp.exp(s - m)
            p = p / jnp.sum(p, axis=-1, keepdims=True)
            v = kv_ref[:, D_MODEL + hd * X_HEAD_DIM:D_MODEL + (hd + 1) * X_HEAD_DIM]
            outs.append(_dot(p.astype(BF16), v).astype(BF16))
        att.append(jnp.concatenate(outs, axis=1))
    acc = [_dot(att[i], wb_ref[2]) for i in range(2)]
    for i, rows in enumerate(halves):
        out_ref[rows, :] = x1[i] + _rms(acc[i], g_ref[3:4, :])


def _post_mix(ya, yb, yc, x, kv, g, w_out, unperm, wq, wo, l, batch, seq):
    n = x.shape[0]
    tm = ROW_TILE
    per_batch = seq // tm
    row = lambda i: (i, 0)
    const = lambda i: (0, 0)
    return pl.pallas_call(
        _post_mix_kernel,
        grid=(n // tm,),
        in_specs=[pl.BlockSpec((tm, A_WIDTH), row),
                  pl.BlockSpec((tm, B_WIDTH), row),
                  pl.BlockSpec((tm, C_WIDTH), row),
                  pl.BlockSpec((tm, D_MODEL), row),
                  pl.BlockSpec((MEM_LEN, 2 * D_MODEL), lambda i: (i // per_batch, 0)),
                  _layer(g, l), _layer(w_out, l, pipeline_mode=pl.Buffered(1)),
                  pl.BlockSpec(unperm.shape, const),
                  _layer(wq, l, pipeline_mode=pl.Buffered(1)), _layer(wo, l, pipeline_mode=pl.Buffered(1))],
        out_specs=pl.BlockSpec((tm, D_MODEL), row),
        out_shape=jax.ShapeDtypeStruct((n, D_MODEL), F32),
        scratch_shapes=[pltpu.VMEM((3, D_MODEL, D_MODEL), BF16)],
        compiler_params=_params("arbitrary"),
        name="outproj_cross",
    )(ya, yb, yc, x, kv, g, w_out, unperm, wq, wo)


def _mlp_kernel(x_ref, g_ref, wu_ref, wd_ref, out_ref):
    x = x_ref[...]
    h = _rms(x, g_ref[4:5, :]).astype(BF16)
    acc = jnp.zeros(x.shape, F32)
    for c in range(D_FF // FF_CHUNK):
        cols = slice(c * FF_CHUNK, (c + 1) * FF_CHUNK)
        f = jnp.square(jnp.maximum(_dot(h, wu_ref[:, cols]), 0.0))
        acc += _dot(f.astype(BF16), wd_ref[cols, :])
    out_ref[...] = x + _rms(acc, g_ref[5:6, :])


def _mlp(x, g, w_up, w_down, l):
    n = x.shape[0]
    tm = MLP_TILE
    row = lambda i: (i, 0)
    const = lambda i: (0, 0)
    return pl.pallas_call(
        _mlp_kernel,
        grid=(n // tm,),
        in_specs=[pl.BlockSpec((tm, D_MODEL), row),
                  _layer(g, l),
                  _layer(w_up, l, pipeline_mode=pl.Buffered(1)),
                  _layer(w_down, l, pipeline_mode=pl.Buffered(1))],
        out_specs=pl.BlockSpec((tm, D_MODEL), row),
        out_shape=jax.ShapeDtypeStruct((n, D_MODEL), F32),
        compiler_params=_params("parallel"),
        name="mlp",
    )(x, g, w_up, w_down)


def kernel(x, mem, positions, norm_g, w_in, sgu_w, sgu_b, sgu_ln_g, sgu_ln_b,
           w_out, x_wq, x_wkv, x_wo, w_up, w_down):
    batch, seq, _ = x.shape
    assert x.shape == (batch, seq, D_MODEL) and seq % (BLK * DIL_PATTERNS[-1][1]) == 0
    assert (batch * seq) % ROW_TILE == 0 and seq % ROW_TILE == 0
    assert mem.shape == (batch, MEM_LEN, D_MODEL) and w_in.shape == (DEPTH, D_MODEL, IN_WIDTH)
    xf = x.reshape(batch * seq, D_MODEL)
    memf = mem.reshape(batch * MEM_LEN, D_MODEL)
    cos, sin = _rope_tables(positions)
    perm = _group_permutation()
    wkv_b = x_wkv.astype(BF16)
    w_up_b, w_down_b = w_up.astype(BF16), w_down.astype(BF16)
    sgu_bias = jnp.repeat(jnp.swapaxes(sgu_b, 1, 2), HEAD_DIM, axis=2)
    ln_g = sgu_ln_g.reshape(DEPTH, 1, B_WIDTH)
    ln_b = sgu_ln_b.reshape(DEPTH, 1, B_WIDTH)
    for l in range(DEPTH):
        za, yb, yc = _inproj_mixers(xf, norm_g, w_in, perm, cos, sin, sgu_w, sgu_bias, ln_g, ln_b,
                                    l, batch, seq)
        ya = _dilated_attention(za, batch, seq)
        kv = _memory_kv(memf, norm_g, wkv_b, l)
        xf = _post_mix(ya, yb, yc, xf, kv, norm_g, w_out, perm.T, x_wq, x_wo, l, batch, seq)
        xf = _mlp(xf, norm_g, w_up_b, w_down_b, l)
    return xf.reshape(batch, seq, D_MODEL)
```

```python
import functools

import jax
import jax.numpy as jnp
import numpy as np
from jax import lax
from jax.experimental import pallas as pl
from jax.experimental.pallas import tpu as pltpu

F32 = jnp.float32
BF16 = jnp.bfloat16

D_MODEL = 1024
DEPTH = 2
MEM_LEN = 256
HEAD_DIM = 64
DIL_HEADS = 6
DIL_PATTERNS = ((128, 1), (512, 4), (2048, 16))
BLK = 128
SGU_GROUPS = 4
RET_HEADS = 6
ROPE_BASE = 10000.0
A_WIDTH = DIL_HEADS * HEAD_DIM
B_WIDTH = SGU_GROUPS * HEAD_DIM
C_WIDTH = RET_HEADS * HEAD_DIM
QKV_A = 3 * A_WIDTH
UV_B = 2 * B_WIDTH
QKVG_C = 4 * C_WIDTH
IN_WIDTH = QKV_A + UV_B + QKVG_C
X_HEADS = 4
X_HEAD_DIM = D_MODEL // X_HEADS
D_FF = 4 * D_MODEL
EPS = 1e-6

LANES = 128
N_PAIRS = A_WIDTH // LANES
VMEM_LIMIT = 56 * 1024 * 1024

SUPER = BLK * DIL_PATTERNS[-1][1]

ROW_TILE = 512
DIL_UNROLL = (4, 4, 4)
DIL_LOOKAHEAD = (12, 3, 3)
MIX_UNROLL = 2
POST_TILE = 1024
POST_PART = 256
MLP_TILE = 1024
FF_CHUNK = 1024


def _params(*sem):
    return pltpu.CompilerParams(dimension_semantics=sem, vmem_limit_bytes=VMEM_LIMIT)


def _layer(arr, l, **kw):
    zeros = (0,) * (arr.ndim - 1)
    return pl.BlockSpec((None,) + arr.shape[1:], lambda *_: (l,) + zeros, **kw)


def _rms(x, g):
    ms = jnp.mean(x * x, axis=-1, keepdims=True)
    return x * lax.rsqrt(ms + EPS) * g


def _dot(a, b):
    return jnp.dot(a, b, preferred_element_type=F32)


def _dot_nt(a, b):
    return lax.dot_general(a, b, (((1,), (1,)), ((), ())), preferred_element_type=F32)


def _dot_tn(a, b):
    return lax.dot_general(a, b, (((0,), (0,)), ((), ())), preferred_element_type=F32)


def _lane_lo(shape):
    lane = lax.broadcasted_iota(jnp.int32, shape, len(shape) - 1)
    return (lane & HEAD_DIM) == 0


ROPE_PACK = LANES // (HEAD_DIM // 2)


def _rope_kernel(pos_ref, inv_ref, cos_ref, sin_ref):
    half = HEAD_DIM // 2
    ang = pos_ref[...].astype(F32) * inv_ref[...]
    cos, sin = jnp.cos(ang), jnp.sin(ang)
    lane = lax.broadcasted_iota(jnp.int32, ang.shape, 1)
    sign = jnp.where(lane < HEAD_DIM, -1.0, 1.0)
    rows = ang.shape[0]
    for t in range(ROPE_PACK):
        mine = (lane >= t * half) & (lane < (t + 1) * half)
        for table, out_ref, scale in ((cos, cos_ref, None), (sin, sin_ref, sign)):
            y = jnp.where(mine, table, 0.0)
            y = y + pltpu.roll(y, 2 * half, 1)
            y = y + pltpu.roll(y, half, 1)
            out_ref[pl.ds(t, rows, stride=ROPE_PACK), :] = y if scale is None else y * scale


def _rope_tables(positions):
    n = positions.size
    half = HEAD_DIM // 2
    inv = 1.0 / (ROPE_BASE ** jnp.linspace(0.0, 1.0, half, dtype=F32))
    inv = jnp.tile(inv, ROPE_PACK).reshape(1, LANES)
    pos = jnp.repeat(positions.reshape(n // ROPE_PACK, ROPE_PACK), half, axis=1)
    tm = 2048
    return pl.pallas_call(
        _rope_kernel,
        grid=(n // tm,),
        in_specs=[pl.BlockSpec((tm // ROPE_PACK, LANES), lambda i: (i, 0)),
                  pl.BlockSpec((1, LANES), lambda i: (0, 0))],
        out_specs=[pl.BlockSpec((tm, LANES), lambda i: (i, 0))] * 2,
        out_shape=[jax.ShapeDtypeStruct((n, LANES), F32)] * 2,
        compiler_params=_params("parallel"),
        name="rope_tables",
    )(pos, inv)


MAX_DIL = DIL_PATTERNS[-1][1]
SUB = 8


def _group_permutation():
    i = np.arange(BLK)
    src = MAX_DIL * (i % SUB) + i // SUB
    return jnp.asarray(src[:, None] == np.arange(BLK)[None, :], BF16)


def _run_offsets(d):
    return [BLK * g + SUB * d * c for g in range(d) for c in range(MAX_DIL // d)]


def _block_load(ref, p, base, d):
    if d == 1:
        return ref[p, pl.ds(pl.multiple_of(base, BLK), BLK), :]
    return jnp.concatenate(
        [ref[p, pl.ds(pl.multiple_of(base + off, SUB), SUB), :] for off in _run_offsets(d)], axis=0)


def _block_store(ref, p, base, d, val):
    if d == 1:
        ref[p, pl.ds(pl.multiple_of(base, BLK), BLK), :] = val
        return
    for i, off in enumerate(_run_offsets(d)):
        ref[p, pl.ds(pl.multiple_of(base + off, SUB), SUB), :] = val[i * SUB:(i + 1) * SUB]


def _dil_blocks(refs, branch, d, blocks, first, lookahead):
    bias_ref, qp, kring, vring, acc_o, acc_m, acc_l = refs
    lo = _lane_lo((BLK, LANES))
    units = [(blk, p) for blk in blocks for p in range(N_PAIRS)]

    def scores(unit):
        (qbase, kcbase, kpbase, _), p = unit
        q = _block_load(qp, p, qbase, d).astype(BF16)
        k = jnp.concatenate([_block_load(kring, p, kpbase, d), _block_load(kring, p, kcbase, d)],
                            axis=0).astype(BF16)
        return [_dot_nt(jnp.where(lo if e == 0 else ~lo, q, jnp.zeros_like(q)), k) for e in range(2)]

    def softmax_pv(unit, s_pair):
        (qbase, kcbase, kpbase, has_prev), p = unit
        bias_prev = jnp.where(has_prev, bias_ref[branch, :, :BLK], -jnp.inf)
        bias_cur = bias_ref[branch, :, BLK:]
        v = jnp.concatenate([_block_load(vring, p, kpbase, d), _block_load(vring, p, kcbase, d)],
                            axis=0).astype(BF16)
        o_h, m_h, l_h = [], [], []
        for s in s_pair:
            sp = s[:, :BLK] + bias_prev
            sc = s[:, BLK:] + bias_cur
            m = jnp.max(jnp.maximum(sp, sc), axis=-1, keepdims=True)
            pp = jnp.exp(sp - m)
            pc = jnp.exp(sc - m)
            l_h.append(jnp.sum(pp + pc, axis=-1, keepdims=True))
            m_h.append(m)
            o_h.append(_dot(jnp.concatenate([pp, pc], axis=1).astype(BF16), v))
        return o_h, m_h, l_h

    def merge(unit, stats):
        (qbase, _, _, _), p = unit
        o_h, m_h, l_h = stats
        o_b = jnp.where(lo, o_h[0], o_h[1])
        m_b = jnp.where(lo, m_h[0], m_h[1])
        l_b = jnp.where(lo, l_h[0], l_h[1])
        if first:
            _block_store(acc_o, p, qbase, d, o_b)
            _block_store(acc_m, p, qbase, d, m_b)
            _block_store(acc_l, p, qbase, d, l_b)
        else:
            m_old = _block_load(acc_m, p, qbase, d)
            m_new = jnp.maximum(m_old, m_b)
            a_old = jnp.exp(m_old - m_new)
            a_b = jnp.exp(m_b - m_new)
            _block_store(acc_o, p, qbase, d, _block_load(acc_o, p, qbase, d) * a_old + o_b * a_b)
            _block_store(acc_l, p, qbase, d, _block_load(acc_l, p, qbase, d) * a_old + l_b * a_b)
            _block_store(acc_m, p, qbase, d, m_new)

    ahead = min(lookahead, len(units))
    pending = [scores(u) for u in units[:ahead]]
    stats_prev = None
    for i, unit in enumerate(units):
        if i + ahead < len(units):
            pending.append(scores(units[i + ahead]))
        stats = softmax_pv(unit, pending[i])
        if stats_prev is not None:
            merge(units[i - 1], stats_prev)
        stats_prev = stats
    merge(units[-1], stats_prev)


def _dil_kernel(q_ref, k_ref, v_ref, bias_ref, o_ref, qp, kring, vring, acc_o, acc_m, acc_l):
    t = pl.program_id(1)
    cur = pl.multiple_of((t % 2) * SUPER, SUPER)
    other = pl.multiple_of(SUPER - cur, SUPER)

    @pl.when(t == 0)
    def _():
        kring[:, SUPER:, :] = jnp.zeros((N_PAIRS, SUPER, LANES), F32)
        vring[:, SUPER:, :] = jnp.zeros((N_PAIRS, SUPER, LANES), F32)

    for p in range(N_PAIRS):
        cols = slice(p * LANES, (p + 1) * LANES)
        qp[p] = q_ref[:, cols].astype(F32) * (HEAD_DIM ** -0.5)
        kring[p, pl.ds(cur, SUPER), :] = k_ref[:, cols].astype(F32)
        vring[p, pl.ds(cur, SUPER), :] = v_ref[:, cols].astype(F32)
    refs = (bias_ref, qp, kring, vring, acc_o, acc_m, acc_l)

    for branch, (window, d) in enumerate(DIL_PATTERNS):
        nblk = MAX_DIL // d
        span = BLK * d

        def block_rows(j, d=d, nblk=nblk, span=span):
            if nblk == 1:
                qbase = pl.multiple_of(j * SUB, SUB)
                return qbase, cur + qbase, other + qbase, t > 0
            r = j // nblk
            nb = j % nblk
            qbase = pl.multiple_of(nb * span + r * SUB, SUB)
            kpbase = pl.multiple_of(
                jnp.where(nb == 0, other + (nblk - 1) * span + r * SUB, cur + qbase - span), SUB)
            return qbase, cur + qbase, kpbase, (t > 0) | (nb > 0)

        unroll, lookahead = DIL_UNROLL[branch], DIL_LOOKAHEAD[branch]

        def body(i, carry, branch=branch, d=d, block_rows=block_rows, unroll=unroll, lookahead=lookahead):
            blocks = [block_rows(i * unroll + u) for u in range(unroll)]
            _dil_blocks(refs, branch, d, blocks, first=(branch == 0), lookahead=lookahead)
            return carry

        lax.fori_loop(0, SUPER // BLK // unroll, body, 0)

    def finish(i, carry):
        rows = pl.ds(pl.multiple_of(i * BLK, BLK), BLK)
        for p in range(N_PAIRS):
            o_ref[rows, p * LANES:(p + 1) * LANES] = (acc_o[p, rows, :] / acc_l[p, rows, :]).astype(BF16)
        return carry

    lax.fori_loop(0, SUPER // BLK, finish, 0)


def _dilated_bias():
    out = []
    i = np.arange(BLK)
    for _, d in DIL_PATTERNS:
        n_c = MAX_DIL // d
        pos = (BLK // d) * (i // (SUB * n_c)) + n_c * (i % SUB) + (i // SUB) % n_c
        prev_ok = pos[None, :] >= pos[:, None]
        cur_ok = pos[None, :] <= pos[:, None]
        out.append(np.where(np.concatenate([prev_ok, cur_ok], axis=1), 0.0, -np.inf))
    return np.stack(out).astype(np.float32)


def _dilated_attention(qkv, batch, seq):
    n = batch * seq
    tiles = seq // SUPER
    bias = _dilated_bias()
    spec = lambda which: pl.BlockSpec((SUPER, A_WIDTH), lambda b, t: (b * tiles + t, which))
    tile_f32 = pltpu.VMEM((N_PAIRS, SUPER, LANES), F32)
    ring_f32 = pltpu.VMEM((N_PAIRS, 2 * SUPER, LANES), F32)
    return pl.pallas_call(
        _dil_kernel,
        grid=(batch, tiles),
        in_specs=[spec(0), spec(1), spec(2), pl.BlockSpec(bias.shape, lambda b, t: (0, 0, 0))],
        out_specs=pl.BlockSpec((SUPER, A_WIDTH), lambda b, t: (b * tiles + t, 0)),
        out_shape=jax.ShapeDtypeStruct((n, A_WIDTH), BF16),
        scratch_shapes=[tile_f32, ring_f32, ring_f32, tile_f32, tile_f32, tile_f32],
        compiler_params=_params("parallel", "arbitrary"),
        name="dilated_attention",
    )(qkv, qkv, qkv, bias)


def _rotate(t, cos, sin_signed):
    return t * cos + pltpu.roll(t, HEAD_DIM, 1) * sin_signed


PROJ_PIECES = (
    (QKV_A + UV_B, 512), (QKV_A + UV_B + 512, 512), (QKV_A + UV_B + 1024, 512),
    (QKV_A, UV_B),
    (0, 640), (640, QKV_A - 640))


def _inmix_body(x_ref, g_ref, w_ref, perm_ref, cos_ref, sin_ref, sb_ref, lg_ref, lb_ref,
                dec_ref, qdec_ref, kdec_ref, cdm_ref, a_ref, yb_ref, yc_ref,
                zb_w, zc_w, zb_r, zc_r, state_ref, wt_ref):
    lo = _lane_lo((BLK, LANES))
    row = lax.broadcasted_iota(jnp.int32, (BLK, BLK), 0)
    col = lax.broadcasted_iota(jnp.int32, (BLK, BLK), 1)
    same_head = ((row >> 5) & 1) == (col >> 6)
    lane = lax.broadcasted_iota(jnp.int32, (BLK, LANES), 1)
    qk_head0 = (lane & (HEAD_DIM // 2)) == 0
    inv_n = 1.0 / HEAD_DIM

    def sgu_pre(rows):
        u = jax.nn.gelu(zb_r[rows, :B_WIDTH])
        v = jax.nn.gelu(zb_r[rows, B_WIDTH:])
        mu = jnp.mean(v, axis=-1, keepdims=True)
        var = jnp.mean(jnp.square(v - mu), axis=-1, keepdims=True)
        v = (v - mu) * lax.rsqrt(var + EPS) * lg_ref[...] + lb_ref[...]
        return u, v.astype(BF16)

    def sgu_post(rows, pre):
        u, v = pre
        for gp in range(B_WIDTH // LANES):
            cols = slice(gp * LANES, (gp + 1) * LANES)
            vp = v[:, cols]
            v2 = jnp.concatenate([jnp.where(lo, vp, jnp.zeros_like(vp)),
                                  jnp.where(lo, jnp.zeros_like(vp), vp)], axis=0)
            gate = _dot(wt_ref[gp], v2) + sb_ref[:, cols]
            yb_ref[rows, cols] = (u[:, cols] * gate).astype(BF16)

    def ret_scores(row_list):
        work = [(rows, p) for rows in row_list for p in range(N_PAIRS)]
        out = []
        for rows, p in work:
            cols = slice(p * LANES, (p + 1) * LANES)
            cos, sin = cos_ref[rows, :], sin_ref[rows, :]
            q = _rotate(zc_r[rows, p * LANES:(p + 1) * LANES], cos, sin)
            k = _rotate(zc_r[rows, C_WIDTH + p * LANES:C_WIDTH + (p + 1) * LANES], cos, sin)
            vb = zc_r[rows, 2 * C_WIDTH + p * LANES:2 * C_WIDTH + (p + 1) * LANES].astype(BF16)
            qb, kb = q.astype(BF16), k.astype(BF16)
            k2 = jnp.concatenate([jnp.where(qk_head0, kb, jnp.zeros_like(kb)),
                                  jnp.where(qk_head0, jnp.zeros_like(kb), kb)], axis=0)
            scores = _dot_nt(qb, k2)
            kv = _dot_tn((k * kdec_ref[:, cols]).astype(BF16), vb)
            out.append((scores, kv, (q * qdec_ref[:, cols]).astype(BF16), vb))
        return work, out

    def ret_values(work, stage1):
        out = []
        for (rows, p), (scores, kv, qd, vb) in zip(work, stage1):
            state = state_ref[p]
            y_x = _dot(qd, state.astype(BF16))
            state_ref[p] = state * cdm_ref[p] + jnp.where(same_head, kv, 0.0)
            v2 = jnp.concatenate([jnp.where(lo, vb, jnp.zeros_like(vb)),
                                  jnp.where(lo, jnp.zeros_like(vb), vb)], axis=0)
            y_in = _dot((scores * dec_ref[p]).astype(BF16), v2)
            out.append((y_in, y_x))
        return out

    def ret_norm(work, stage2):
        for (rows, p), (y_in, y_x) in zip(work, stage2):
            cols = slice(p * LANES, (p + 1) * LANES)
            y = y_in + y_x
            mu_lo = jnp.sum(jnp.where(lo, y, 0.0), axis=-1, keepdims=True) * inv_n
            mu_hi = jnp.sum(jnp.where(lo, 0.0, y), axis=-1, keepdims=True) * inv_n
            yc_ = y - jnp.where(lo, mu_lo, mu_hi)
            sq = yc_ * yc_
            var_lo = jnp.sum(jnp.where(lo, sq, 0.0), axis=-1, keepdims=True) * inv_n
            var_hi = jnp.sum(jnp.where(lo, 0.0, sq), axis=-1, keepdims=True) * inv_n
            yn = yc_ * lax.rsqrt(jnp.where(lo, var_lo, var_hi) + EPS)
            gate = zc_r[rows, 3 * C_WIDTH + p * LANES:3 * C_WIDTH + (p + 1) * LANES]
            yc_ref[rows, cols] = (jax.nn.silu(gate) * yn).astype(BF16)

    def project(lhs, piece):
        first, width = PROJ_PIECES[piece]
        y = _dot(lhs, w_ref[:, first:first + width])
        if first >= QKV_A + UV_B:
            zc_w[:, first - QKV_A - UV_B:first - QKV_A - UV_B + width] = y
        elif first >= QKV_A:
            zb_w[...] = y
        else:
            a_ref[:, first:first + width] = y.astype(BF16)

    tm = x_ref.shape[0]
    chunks = [slice(c * BLK, (c + 1) * BLK) for c in range(tm // BLK)]
    groups = [chunks[g:g + MIX_UNROLL] for g in range(0, len(chunks), MIX_UNROLL)]
    assert len(groups) == 2 and len(PROJ_PIECES) == 6

    h = _rms(x_ref[...], g_ref[0:1, :]).astype(BF16)
    hp = jnp.concatenate([_dot(perm_ref[...], h[c]).astype(BF16) for c in chunks], axis=0)
    project(h, 0)
    pre = [sgu_pre(rows) for rows in groups[0]]
    work0, s1 = ret_scores(groups[0])
    project(h, 1)
    for rows, t in zip(groups[0], pre):
        sgu_post(rows, t)
    s2 = ret_values(work0, s1)
    project(h, 2)
    ret_norm(work0, s2)
    pre = [sgu_pre(rows) for rows in groups[1]]
    work1, s1 = ret_scores(groups[1])
    project(h, 3)
    for rows, t in zip(groups[1], pre):
        sgu_post(rows, t)
    s2 = ret_values(work1, s1)
    project(hp, 4)
    ret_norm(work1, s2)
    project(hp, 5)


def _inmix_kernel(x_ref, g_ref, w_ref, perm_ref, cos_ref, sin_ref, sw_ref, sb_ref, lg_ref, lb_ref,
                  dec_ref, qdec_ref, kdec_ref, cdm_ref, a_ref, yb_ref, yc_ref,
                  zb0, zc0, zb1, zc1, state_ref, wt_ref, wb_ref, *, tiles_per_batch):
    i = pl.program_id(0)

    @pl.when(i == 0)
    def _():
        half = HEAD_DIM // 2
        lane = lax.broadcasted_iota(jnp.int32, (BLK, LANES), 1)
        from_right = (lane >= half) & (lane < 2 * half)
        from_left = (lane >= 2 * half) & (lane < 3 * half)
        qk_tiles = range(QKV_A + UV_B, QKV_A + UV_B + 2 * C_WIDTH, LANES)

        def cast_rows(r, carry):
            rows = pl.ds(pl.multiple_of(r * BLK, BLK), BLK)
            for c0 in range(0, IN_WIDTH, LANES):
                t = w_ref[rows, c0:c0 + LANES]
                if c0 in qk_tiles:
                    t = jnp.where(from_right, pltpu.roll(t, LANES - half, 1),
                                  jnp.where(from_left, pltpu.roll(t, half, 1), t))
                wb_ref[rows, c0:c0 + LANES] = t.astype(BF16)
            return carry

        lax.fori_loop(0, D_MODEL // BLK, cast_rows, 0)
        zb1[...] = jnp.zeros_like(zb1)
        zc1[...] = jnp.zeros_like(zc1)
        row = lax.broadcasted_iota(jnp.int32, (BLK, BLK), 0)
        col = lax.broadcasted_iota(jnp.int32, (BLK, BLK), 1)
        for gp in range(SGU_GROUPS // 2):
            wt_ref[gp] = jnp.concatenate([jnp.where(row >= col, sw_ref[2 * gp + e], 0.0)
                                          for e in range(2)], axis=1).astype(BF16)

    @pl.when((i == 0) | ((i - 1) % tiles_per_batch == 0))
    def _():
        state_ref[...] = jnp.zeros_like(state_ref)

    common = (x_ref, g_ref, wb_ref, perm_ref, cos_ref, sin_ref, sb_ref, lg_ref, lb_ref,
              dec_ref, qdec_ref, kdec_ref, cdm_ref, a_ref, yb_ref, yc_ref)

    @pl.when(i % 2 == 0)
    def _():
        _inmix_body(*common, zb0, zc0, zb1, zc1, state_ref, wt_ref)

    @pl.when(i % 2 == 1)
    def _():
        _inmix_body(*common, zb1, zc1, zb0, zc0, state_ref, wt_ref)


def _qk_lane_head():
    lane = np.arange(C_WIDTH)
    return 2 * (lane // LANES) + ((lane >> 5) & 1)


def _retention_tables():
    f32 = np.float32
    heads = np.arange(RET_HEADS, dtype=f32)
    log_g = np.log1p(-np.power(f32(2.0), f32(-5.0) - heads)).astype(f32)
    idx = np.arange(BLK, dtype=f32)
    rel = idx[:, None] - idx[None, :]
    scale = f32(HEAD_DIM ** -0.5)
    decay = np.where(rel[None] >= 0,
                     np.exp(np.maximum(rel, f32(0.0))[None] * log_g[:, None, None]), f32(0.0)) * scale
    k_dec = np.exp((BLK - 1 - idx)[:, None] * log_g[None, :]) * scale
    q_dec = np.exp((idx + f32(1.0))[:, None] * log_g[None, :])
    lane_head = _qk_lane_head()
    k_dec = k_dec[:, lane_head]
    q_dec = q_dec[:, lane_head]
    chunk_decay = np.exp(f32(BLK) * log_g)
    per_row = chunk_decay[lane_head].reshape(N_PAIRS, LANES)
    row_head = (np.arange(LANES) >> 5) & 1
    col_head = np.arange(LANES) >> 6
    same = row_head[:, None] == col_head[None, :]
    cdm = np.where(same[None], per_row[:, :, None], f32(0.0))
    decay = np.concatenate([decay[0::2], decay[1::2]], axis=2)
    return tuple(t.astype(f32) for t in (decay, q_dec, k_dec, cdm))


def _inproj_mixers(x, g, w, perm, cos, sin, sgu_w, sgu_bias, ln_g, ln_b, l, batch, seq):
    n = x.shape[0]
    tm = ROW_TILE
    nt = n // tm
    decay, q_dec, k_dec, cdm = _retention_tables()
    cur = lambda i: (jnp.minimum(i, nt - 1), 0)
    prev = lambda i: (jnp.maximum(i - 1, 0), 0)
    const2 = lambda i: (0, 0)
    const3 = lambda i: (0, 0, 0)
    return pl.pallas_call(
        functools.partial(_inmix_kernel, tiles_per_batch=seq // tm),
        grid=(nt + 1,),
        in_specs=[pl.BlockSpec((tm, D_MODEL), cur),
                  _layer(g, l), _layer(w, l, pipeline_mode=pl.Buffered(1)),
                  pl.BlockSpec(perm.shape, const2),
                  pl.BlockSpec((tm, LANES), prev),
                  pl.BlockSpec((tm, LANES), prev),
                  _layer(sgu_w, l), _layer(sgu_bias, l), _layer(ln_g, l), _layer(ln_b, l),
                  pl.BlockSpec(decay.shape, const3),
                  pl.BlockSpec(q_dec.shape, const2),
                  pl.BlockSpec(k_dec.shape, const2),
                  pl.BlockSpec(cdm.shape, const3)],
        out_specs=[pl.BlockSpec((tm, QKV_A), cur),
                   pl.BlockSpec((tm, B_WIDTH), prev),
                   pl.BlockSpec((tm, C_WIDTH), prev)],
        out_shape=[jax.ShapeDtypeStruct((n, QKV_A), BF16),
                   jax.ShapeDtypeStruct((n, B_WIDTH), BF16),
                   jax.ShapeDtypeStruct((n, C_WIDTH), BF16)],
        scratch_shapes=[pltpu.VMEM((tm, UV_B), F32), pltpu.VMEM((tm, QKVG_C), F32),
                        pltpu.VMEM((tm, UV_B), F32), pltpu.VMEM((tm, QKVG_C), F32),
                        pltpu.VMEM((N_PAIRS, LANES, LANES), F32),
                        pltpu.VMEM((SGU_GROUPS // 2, BLK, 2 * BLK), BF16),
                        pltpu.VMEM((D_MODEL, IN_WIDTH), BF16)],
        compiler_params=_params("arbitrary"),
        name="inproj_mixers",
    )(x, g, w, perm, cos, sin, sgu_w, sgu_bias, ln_g, ln_b, decay, q_dec, k_dec, cdm)


def _kv_kernel(m_ref, g_ref, w_ref, kv_ref):
    h = _rms(m_ref[...], g_ref[6:7, :]).astype(BF16)
    kv_ref[...] = _dot(h, w_ref[...].astype(BF16)).astype(BF16)


def _memory_kv(mem, g, w, l):
    n = mem.shape[0]
    return pl.pallas_call(
        _kv_kernel,
        grid=(n // MEM_LEN,),
        in_specs=[pl.BlockSpec((MEM_LEN, D_MODEL), lambda i: (i, 0)),
                  _layer(g, l), _layer(w, l)],
        out_specs=pl.BlockSpec((MEM_LEN, 2 * D_MODEL), lambda i: (i, 0)),
        out_shape=jax.ShapeDtypeStruct((n, 2 * D_MODEL), BF16),
        compiler_params=_params("parallel"),
        name="memory_kv",
    )(mem, g, w)


def _post_mix_kernel(ya_ref, yb_ref, yc_ref, x_ref, kv_ref, g_ref, w_ref, unperm_ref, wq_ref, wo_ref,
                     out_ref, wb_ref):
    @pl.when(pl.program_id(0) == 0)
    def _():
        def cast_rows(r, carry):
            rows = pl.ds(pl.multiple_of(r * BLK, BLK), BLK)
            for k, src in enumerate((w_ref, wq_ref, wo_ref)):
                wb_ref[k, rows, :] = src[rows, :].astype(BF16)
            return carry

        lax.fori_loop(0, D_MODEL // BLK, cast_rows, 0)

    tm = x_ref.shape[0]
    n_parts = tm // POST_PART
    halves = [slice(i * POST_PART, (i + 1) * POST_PART) for i in range(n_parts)]
    heads = [slice(hd * X_HEAD_DIM, (hd + 1) * X_HEAD_DIM) for hd in range(X_HEADS)]
    scale = X_HEAD_DIM ** -0.5

    def unpermute(rows):
        n_grp = (rows.stop - rows.start) // BLK
        return jnp.concatenate(
            [_dot(unperm_ref[...], ya_ref[rows.start + g * BLK:rows.start + (g + 1) * BLK, :]).astype(BF16)
             for g in range(n_grp)], axis=0)

    ya = [unpermute(rows) for rows in halves]
    proj = [_dot(jnp.concatenate([ya[i], yb_ref[rows, :], yc_ref[rows, :]], axis=1), wb_ref[0])
            for i, rows in enumerate(halves)]
    x1 = [x_ref[rows, :] + _rms(proj[i], g_ref[1:2, :]) for i, rows in enumerate(halves)]
    q = [_dot(_rms(x1[i], g_ref[2:3, :]).astype(BF16), wb_ref[1]).astype(BF16) for i in range(n_parts)]
    scores = [[_dot_nt(q[i][:, cols], kv_ref[:, cols]) for cols in heads] for i in range(n_parts)]
    att = []
    for i in range(n_parts):
        outs = []
        for hd, s in enumerate(scores[i]):
            s = s * scale
            m = jnp.max(s, axis=-1, keepdims=True)
            p = jnp.exp(s - m)
            p = p / jnp.sum(p, axis=-1, keepdims=True)
            v = kv_ref[:, D_MODEL + hd * X_HEAD_DIM:D_MODEL + (hd + 1) * X_HEAD_DIM]
            outs.append(_dot(p.astype(BF16), v).astype(BF16))
        att.append(jnp.concatenate(outs, axis=1))
    acc = [_dot(att[i], wb_ref[2]) for i in range(n_parts)]
    for i, rows in enumerate(halves):
        out_ref[rows, :] = x1[i] + _rms(acc[i], g_ref[3:4, :])


def _post_mix(ya, yb, yc, x, kv, g, w_out, unperm, wq, wo, l, batch, seq):
    n = x.shape[0]
    tm = POST_TILE
    per_batch = seq // tm
    row = lambda i: (i, 0)
    const = lambda i: (0, 0)
    return pl.pallas_call(
        _post_mix_kernel,
        grid=(n // tm,),
        in_specs=[pl.BlockSpec((tm, A_WIDTH), row),
                  pl.BlockSpec((tm, B_WIDTH), row),
                  pl.BlockSpec((tm, C_WIDTH), row),
                  pl.BlockSpec((tm, D_MODEL), row),
                  pl.BlockSpec((MEM_LEN, 2 * D_MODEL), lambda i: (i // per_batch, 0)),
                  _layer(g, l), _layer(w_out, l, pipeline_mode=pl.Buffered(1)),
                  pl.BlockSpec(unperm.shape, const),
                  _layer(wq, l, pipeline_mode=pl.Buffered(1)), _layer(wo, l, pipeline_mode=pl.Buffered(1))],
        out_specs=pl.BlockSpec((tm, D_MODEL), row),
        out_shape=jax.ShapeDtypeStruct((n, D_MODEL), F32),
        scratch_shapes=[pltpu.VMEM((3, D_MODEL, D_MODEL), BF16)],
        compiler_params=_params("arbitrary"),
        name="outproj_cross",
    )(ya, yb, yc, x, kv, g, w_out, unperm, wq, wo)


def _mlp_kernel(x_ref, g_ref, wu_ref, wd_ref, out_ref):
    x = x_ref[...]
    h = _rms(x, g_ref[4:5, :]).astype(BF16)
    acc = jnp.zeros(x.shape, F32)
    for c in range(D_FF // FF_CHUNK):
        cols = slice(c * FF_CHUNK, (c + 1) * FF_CHUNK)
        f = jnp.square(jnp.maximum(_dot(h, wu_ref[:, cols]), 0.0))
        acc += _dot(f.astype(BF16), wd_ref[cols, :])
    out_ref[...] = x + _rms(acc, g_ref[5:6, :])


def _mlp(x, g, w_up, w_down, l):
    n = x.shape[0]
    tm = MLP_TILE
    row = lambda i: (i, 0)
    const = lambda i: (0, 0)
    return pl.pallas_call(
        _mlp_kernel,
        grid=(n // tm,),
        in_specs=[pl.BlockSpec((tm, D_MODEL), row),
                  _layer(g, l),
                  _layer(w_up, l, pipeline_mode=pl.Buffered(1)),
                  _layer(w_down, l, pipeline_mode=pl.Buffered(1))],
        out_specs=pl.BlockSpec((tm, D_MODEL), row),
        out_shape=jax.ShapeDtypeStruct((n, D_MODEL), F32),
        compiler_params=_params("parallel"),
        name="mlp",
    )(x, g, w_up, w_down)


def kernel(x, mem, positions, norm_g, w_in, sgu_w, sgu_b, sgu_ln_g, sgu_ln_b,
           w_out, x_wq, x_wkv, x_wo, w_up, w_down):
    batch, seq, _ = x.shape
    assert x.shape == (batch, seq, D_MODEL) and seq % (BLK * DIL_PATTERNS[-1][1]) == 0
    assert (batch * seq) % ROW_TILE == 0 and seq % ROW_TILE == 0
    assert mem.shape == (batch, MEM_LEN, D_MODEL) and w_in.shape == (DEPTH, D_MODEL, IN_WIDTH)
    xf = x.reshape(batch * seq, D_MODEL)
    memf = mem.reshape(batch * MEM_LEN, D_MODEL)
    cos, sin = _rope_tables(positions)
    perm = _group_permutation()
    w_up_b, w_down_b = w_up.astype(BF16), w_down.astype(BF16)
    sgu_bias = jnp.repeat(jnp.swapaxes(sgu_b, 1, 2), HEAD_DIM, axis=2)
    ln_g = sgu_ln_g.reshape(DEPTH, 1, B_WIDTH)
    ln_b = sgu_ln_b.reshape(DEPTH, 1, B_WIDTH)
    for l in range(DEPTH):
        za, yb, yc = _inproj_mixers(xf, norm_g, w_in, perm, cos, sin, sgu_w, sgu_bias, ln_g, ln_b,
                                    l, batch, seq)
        ya = _dilated_attention(za, batch, seq)
        kv = _memory_kv(memf, norm_g, x_wkv, l)
        xf = _post_mix(ya, yb, yc, xf, kv, norm_g, w_out, perm.T, x_wq, x_wo, l, batch, seq)
        xf = _mlp(xf, norm_g, w_up_b, w_down_b, l)
    return xf.reshape(batch, seq, D_MODEL)
```

```python
import functools

import jax
import jax.numpy as jnp
import numpy as np
from jax import lax
from jax.experimental import pallas as pl
from jax.experimental.pallas import tpu as pltpu

F32 = jnp.float32
BF16 = jnp.bfloat16

D_MODEL = 1024
DEPTH = 2
MEM_LEN = 256
HEAD_DIM = 64
DIL_HEADS = 6
DIL_PATTERNS = ((128, 1), (512, 4), (2048, 16))
BLK = 128
SGU_GROUPS = 4
RET_HEADS = 6
ROPE_BASE = 10000.0
A_WIDTH = DIL_HEADS * HEAD_DIM
B_WIDTH = SGU_GROUPS * HEAD_DIM
C_WIDTH = RET_HEADS * HEAD_DIM
QKV_A = 3 * A_WIDTH
UV_B = 2 * B_WIDTH
QKVG_C = 4 * C_WIDTH
IN_WIDTH = QKV_A + UV_B + QKVG_C
X_HEADS = 4
X_HEAD_DIM = D_MODEL // X_HEADS
D_FF = 4 * D_MODEL
EPS = 1e-6

LANES = 128
N_PAIRS = A_WIDTH // LANES
VMEM_LIMIT = 56 * 1024 * 1024

SUPER = BLK * DIL_PATTERNS[-1][1]

ROW_TILE = 512
DIL_UNROLL = (4, 4, 4)
DIL_LOOKAHEAD = (12, 3, 3)
MIX_UNROLL = 2
POST_TILE = 1024
POST_PART = 256
MLP_TILE = 512
FF_CHUNK = 1024


def _params(*sem):
    return pltpu.CompilerParams(dimension_semantics=sem, vmem_limit_bytes=VMEM_LIMIT)


def _layer(arr, l, **kw):
    zeros = (0,) * (arr.ndim - 1)
    return pl.BlockSpec((None,) + arr.shape[1:], lambda *_: (l,) + zeros, **kw)


def _rms(x, g):
    ms = jnp.mean(x * x, axis=-1, keepdims=True)
    return x * lax.rsqrt(ms + EPS) * g


def _dot(a, b):
    return jnp.dot(a, b, preferred_element_type=F32)


def _dot_nt(a, b):
    return lax.dot_general(a, b, (((1,), (1,)), ((), ())), preferred_element_type=F32)


def _dot_tn(a, b):
    return lax.dot_general(a, b, (((0,), (0,)), ((), ())), preferred_element_type=F32)


def _lane_lo(shape):
    lane = lax.broadcasted_iota(jnp.int32, shape, len(shape) - 1)
    return (lane & HEAD_DIM) == 0


ROPE_PACK = LANES // (HEAD_DIM // 2)


def _rope_kernel(pos_ref, inv_ref, cos_ref, sin_ref):
    half = HEAD_DIM // 2
    ang = pos_ref[...].astype(F32) * inv_ref[...]
    cos, sin = jnp.cos(ang), jnp.sin(ang)
    lane = lax.broadcasted_iota(jnp.int32, ang.shape, 1)
    sign = jnp.where(lane < HEAD_DIM, -1.0, 1.0)
    rows = ang.shape[0]
    for t in range(ROPE_PACK):
        mine = (lane >= t * half) & (lane < (t + 1) * half)
        for table, out_ref, scale in ((cos, cos_ref, None), (sin, sin_ref, sign)):
            y = jnp.where(mine, table, 0.0)
            y = y + pltpu.roll(y, 2 * half, 1)
            y = y + pltpu.roll(y, half, 1)
            out_ref[pl.ds(t, rows, stride=ROPE_PACK), :] = y if scale is None else y * scale


def _rope_tables(positions):
    n = positions.size
    half = HEAD_DIM // 2
    inv = 1.0 / (ROPE_BASE ** jnp.linspace(0.0, 1.0, half, dtype=F32))
    inv = jnp.tile(inv, ROPE_PACK).reshape(1, LANES)
    pos = jnp.repeat(positions.reshape(n // ROPE_PACK, ROPE_PACK), half, axis=1)
    tm = 2048
    return pl.pallas_call(
        _rope_kernel,
        grid=(n // tm,),
        in_specs=[pl.BlockSpec((tm // ROPE_PACK, LANES), lambda i: (i, 0)),
                  pl.BlockSpec((1, LANES), lambda i: (0, 0))],
        out_specs=[pl.BlockSpec((tm, LANES), lambda i: (i, 0))] * 2,
        out_shape=[jax.ShapeDtypeStruct((n, LANES), F32)] * 2,
        compiler_params=_params("parallel"),
        name="rope_tables",
    )(pos, inv)


MAX_DIL = DIL_PATTERNS[-1][1]
SUB = 8


def _group_permutation():
    i = np.arange(BLK)
    src = MAX_DIL * (i % SUB) + i // SUB
    return jnp.asarray(src[:, None] == np.arange(BLK)[None, :], BF16)


def _run_offsets(d):
    return [BLK * g + SUB * d * c for g in range(d) for c in range(MAX_DIL // d)]


def _block_load(ref, p, base, d):
    if d == 1:
        return ref[p, pl.ds(pl.multiple_of(base, BLK), BLK), :]
    return jnp.concatenate(
        [ref[p, pl.ds(pl.multiple_of(base + off, SUB), SUB), :] for off in _run_offsets(d)], axis=0)


def _block_store(ref, p, base, d, val):
    if d == 1:
        ref[p, pl.ds(pl.multiple_of(base, BLK), BLK), :] = val
        return
    for i, off in enumerate(_run_offsets(d)):
        ref[p, pl.ds(pl.multiple_of(base + off, SUB), SUB), :] = val[i * SUB:(i + 1) * SUB]


def _dil_blocks(refs, branch, d, blocks, first, lookahead):
    bias_ref, qp, kring, vring, acc_o, acc_m, acc_l = refs
    lo = _lane_lo((BLK, LANES))
    units = [(blk, p) for blk in blocks for p in range(N_PAIRS)]

    def scores(unit):
        (qbase, kcbase, kpbase, _), p = unit
        q = _block_load(qp, p, qbase, d).astype(BF16)
        k = jnp.concatenate([_block_load(kring, p, kpbase, d), _block_load(kring, p, kcbase, d)],
                            axis=0).astype(BF16)
        return [_dot_nt(jnp.where(lo if e == 0 else ~lo, q, jnp.zeros_like(q)), k) for e in range(2)]

    def softmax_pv(unit, s_pair):
        (qbase, kcbase, kpbase, has_prev), p = unit
        bias_prev = jnp.where(has_prev, bias_ref[branch, :, :BLK], -jnp.inf)
        bias_cur = bias_ref[branch, :, BLK:]
        v = jnp.concatenate([_block_load(vring, p, kpbase, d), _block_load(vring, p, kcbase, d)],
                            axis=0).astype(BF16)
        o_h, m_h, l_h = [], [], []
        for s in s_pair:
            sp = s[:, :BLK] + bias_prev
            sc = s[:, BLK:] + bias_cur
            m = jnp.max(jnp.maximum(sp, sc), axis=-1, keepdims=True)
            pp = jnp.exp(sp - m)
            pc = jnp.exp(sc - m)
            l_h.append(jnp.sum(pp + pc, axis=-1, keepdims=True))
            m_h.append(m)
            o_h.append(_dot(jnp.concatenate([pp, pc], axis=1).astype(BF16), v))
        return o_h, m_h, l_h

    def merge(unit, stats):
        (qbase, _, _, _), p = unit
        o_h, m_h, l_h = stats
        o_b = jnp.where(lo, o_h[0], o_h[1])
        m_b = jnp.where(lo, m_h[0], m_h[1])
        l_b = jnp.where(lo, l_h[0], l_h[1])
        if first:
            _block_store(acc_o, p, qbase, d, o_b)
            _block_store(acc_m, p, qbase, d, m_b)
            _block_store(acc_l, p, qbase, d, l_b)
        else:
            m_old = _block_load(acc_m, p, qbase, d)
            m_new = jnp.maximum(m_old, m_b)
            a_old = jnp.exp(m_old - m_new)
            a_b = jnp.exp(m_b - m_new)
            _block_store(acc_o, p, qbase, d, _block_load(acc_o, p, qbase, d) * a_old + o_b * a_b)
            _block_store(acc_l, p, qbase, d, _block_load(acc_l, p, qbase, d) * a_old + l_b * a_b)
            _block_store(acc_m, p, qbase, d, m_new)

    ahead = min(lookahead, len(units))
    pending = [scores(u) for u in units[:ahead]]
    stats_prev = None
    for i, unit in enumerate(units):
        if i + ahead < len(units):
            pending.append(scores(units[i + ahead]))
        stats = softmax_pv(unit, pending[i])
        if stats_prev is not None:
            merge(units[i - 1], stats_prev)
        stats_prev = stats
    merge(units[-1], stats_prev)


def _dil_kernel(q_ref, k_ref, v_ref, bias_ref, o_ref, qp, kring, vring, acc_o, acc_m, acc_l):
    t = pl.program_id(1)
    cur = pl.multiple_of((t % 2) * SUPER, SUPER)
    other = pl.multiple_of(SUPER - cur, SUPER)

    @pl.when(t == 0)
    def _():
        kring[:, SUPER:, :] = jnp.zeros((N_PAIRS, SUPER, LANES), F32)
        vring[:, SUPER:, :] = jnp.zeros((N_PAIRS, SUPER, LANES), F32)

    for p in range(N_PAIRS):
        cols = slice(p * LANES, (p + 1) * LANES)
        qp[p] = q_ref[:, cols].astype(F32) * (HEAD_DIM ** -0.5)
        kring[p, pl.ds(cur, SUPER), :] = k_ref[:, cols].astype(F32)
        vring[p, pl.ds(cur, SUPER), :] = v_ref[:, cols].astype(F32)
    refs = (bias_ref, qp, kring, vring, acc_o, acc_m, acc_l)

    for branch, (window, d) in enumerate(DIL_PATTERNS):
        nblk = MAX_DIL // d
        span = BLK * d

        def block_rows(j, d=d, nblk=nblk, span=span):
            if nblk == 1:
                qbase = pl.multiple_of(j * SUB, SUB)
                return qbase, cur + qbase, other + qbase, t > 0
            r = j // nblk
            nb = j % nblk
            qbase = pl.multiple_of(nb * span + r * SUB, SUB)
            kpbase = pl.multiple_of(
                jnp.where(nb == 0, other + (nblk - 1) * span + r * SUB, cur + qbase - span), SUB)
            return qbase, cur + qbase, kpbase, (t > 0) | (nb > 0)

        unroll, lookahead = DIL_UNROLL[branch], DIL_LOOKAHEAD[branch]

        def body(i, carry, branch=branch, d=d, block_rows=block_rows, unroll=unroll, lookahead=lookahead):
            blocks = [block_rows(i * unroll + u) for u in range(unroll)]
            _dil_blocks(refs, branch, d, blocks, first=(branch == 0), lookahead=lookahead)
            return carry

        lax.fori_loop(0, SUPER // BLK // unroll, body, 0)

    def finish(i, carry):
        rows = pl.ds(pl.multiple_of(i * BLK, BLK), BLK)
        for p in range(N_PAIRS):
            o_ref[rows, p * LANES:(p + 1) * LANES] = (acc_o[p, rows, :] / acc_l[p, rows, :]).astype(BF16)
        return carry

    lax.fori_loop(0, SUPER // BLK, finish, 0)


def _dilated_bias():
    out = []
    i = np.arange(BLK)
    for _, d in DIL_PATTERNS:
        n_c = MAX_DIL // d
        pos = (BLK // d) * (i // (SUB * n_c)) + n_c * (i % SUB) + (i // SUB) % n_c
        prev_ok = pos[None, :] >= pos[:, None]
        cur_ok = pos[None, :] <= pos[:, None]
        out.append(np.where(np.concatenate([prev_ok, cur_ok], axis=1), 0.0, -np.inf))
    return np.stack(out).astype(np.float32)


def _dilated_attention(qkv, batch, seq):
    n = batch * seq
    tiles = seq // SUPER
    bias = _dilated_bias()
    spec = lambda which: pl.BlockSpec((SUPER, A_WIDTH), lambda b, t: (b * tiles + t, which))
    tile_f32 = pltpu.VMEM((N_PAIRS, SUPER, LANES), F32)
    ring_f32 = pltpu.VMEM((N_PAIRS, 2 * SUPER, LANES), F32)
    return pl.pallas_call(
        _dil_kernel,
        grid=(batch, tiles),
        in_specs=[spec(0), spec(1), spec(2), pl.BlockSpec(bias.shape, lambda b, t: (0, 0, 0))],
        out_specs=pl.BlockSpec((SUPER, A_WIDTH), lambda b, t: (b * tiles + t, 0)),
        out_shape=jax.ShapeDtypeStruct((n, A_WIDTH), BF16),
        scratch_shapes=[tile_f32, ring_f32, ring_f32, tile_f32, tile_f32, tile_f32],
        compiler_params=_params("parallel", "arbitrary"),
        name="dilated_attention",
    )(qkv, qkv, qkv, bias)


def _rotate(t, cos, sin_signed):
    return t * cos + pltpu.roll(t, HEAD_DIM, 1) * sin_signed


PROJ_PIECES = (
    (QKV_A + UV_B, 512), (QKV_A + UV_B + 512, 512), (QKV_A + UV_B + 1024, 512),
    (QKV_A, UV_B),
    (0, 640), (640, QKV_A - 640))


def _inmix_body(x_ref, g_ref, w_ref, perm_ref, cos_ref, sin_ref, sb_ref, lg_ref, lb_ref,
                dec_ref, qdec_ref, kdec_ref, cdm_ref, a_ref, yb_ref, yc_ref,
                zb_w, zc_w, zb_r, zc_r, state_ref, wt_ref):
    lo = _lane_lo((BLK, LANES))
    row = lax.broadcasted_iota(jnp.int32, (BLK, BLK), 0)
    col = lax.broadcasted_iota(jnp.int32, (BLK, BLK), 1)
    same_head = ((row >> 5) & 1) == (col >> 6)
    lane = lax.broadcasted_iota(jnp.int32, (BLK, LANES), 1)
    qk_head0 = (lane & (HEAD_DIM // 2)) == 0
    inv_n = 1.0 / HEAD_DIM

    def sgu_pre(rows):
        u = jax.nn.gelu(zb_r[rows, :B_WIDTH])
        v = jax.nn.gelu(zb_r[rows, B_WIDTH:])
        mu = jnp.mean(v, axis=-1, keepdims=True)
        var = jnp.mean(jnp.square(v - mu), axis=-1, keepdims=True)
        v = (v - mu) * lax.rsqrt(var + EPS) * lg_ref[...] + lb_ref[...]
        return u, v.astype(BF16)

    def sgu_post(rows, pre):
        u, v = pre
        for gp in range(B_WIDTH // LANES):
            cols = slice(gp * LANES, (gp + 1) * LANES)
            vp = v[:, cols]
            v2 = jnp.concatenate([jnp.where(lo, vp, jnp.zeros_like(vp)),
                                  jnp.where(lo, jnp.zeros_like(vp), vp)], axis=0)
            gate = _dot(wt_ref[gp], v2) + sb_ref[:, cols]
            yb_ref[rows, cols] = (u[:, cols] * gate).astype(BF16)

    def ret_scores(row_list):
        work = [(rows, p) for rows in row_list for p in range(N_PAIRS)]
        out = []
        for rows, p in work:
            cols = slice(p * LANES, (p + 1) * LANES)
            cos, sin = cos_ref[rows, :], sin_ref[rows, :]
            q = _rotate(zc_r[rows, p * LANES:(p + 1) * LANES], cos, sin)
            k = _rotate(zc_r[rows, C_WIDTH + p * LANES:C_WIDTH + (p + 1) * LANES], cos, sin)
            vb = zc_r[rows, 2 * C_WIDTH + p * LANES:2 * C_WIDTH + (p + 1) * LANES].astype(BF16)
            qb, kb = q.astype(BF16), k.astype(BF16)
            k2 = jnp.concatenate([jnp.where(qk_head0, kb, jnp.zeros_like(kb)),
                                  jnp.where(qk_head0, jnp.zeros_like(kb), kb)], axis=0)
            scores = _dot_nt(qb, k2)
            kv = _dot_tn((k * kdec_ref[:, cols]).astype(BF16), vb)
            out.append((scores, kv, (q * qdec_ref[:, cols]).astype(BF16), vb))
        return work, out

    def ret_values(work, stage1):
        out = []
        for (rows, p), (scores, kv, qd, vb) in zip(work, stage1):
            state = state_ref[p]
            y_x = _dot(qd, state.astype(BF16))
            state_ref[p] = state * cdm_ref[p] + jnp.where(same_head, kv, 0.0)
            v2 = jnp.concatenate([jnp.where(lo, vb, jnp.zeros_like(vb)),
                                  jnp.where(lo, jnp.zeros_like(vb), vb)], axis=0)
            y_in = _dot((scores * dec_ref[p]).astype(BF16), v2)
            out.append((y_in, y_x))
        return out

    def ret_norm(work, stage2):
        for (rows, p), (y_in, y_x) in zip(work, stage2):
            cols = slice(p * LANES, (p + 1) * LANES)
            y = y_in + y_x
            mu_lo = jnp.sum(jnp.where(lo, y, 0.0), axis=-1, keepdims=True) * inv_n
            mu_hi = jnp.sum(jnp.where(lo, 0.0, y), axis=-1, keepdims=True) * inv_n
            yc_ = y - jnp.where(lo, mu_lo, mu_hi)
            sq = yc_ * yc_
            var_lo = jnp.sum(jnp.where(lo, sq, 0.0), axis=-1, keepdims=True) * inv_n
            var_hi = jnp.sum(jnp.where(lo, 0.0, sq), axis=-1, keepdims=True) * inv_n
            yn = yc_ * lax.rsqrt(jnp.where(lo, var_lo, var_hi) + EPS)
            gate = zc_r[rows, 3 * C_WIDTH + p * LANES:3 * C_WIDTH + (p + 1) * LANES]
            yc_ref[rows, cols] = (jax.nn.silu(gate) * yn).astype(BF16)

    def project(lhs, piece):
        first, width = PROJ_PIECES[piece]
        y = _dot(lhs, w_ref[:, first:first + width])
        if first >= QKV_A + UV_B:
            zc_w[:, first - QKV_A - UV_B:first - QKV_A - UV_B + width] = y
        elif first >= QKV_A:
            zb_w[...] = y
        else:
            a_ref[:, first:first + width] = y.astype(BF16)

    tm = x_ref.shape[0]
    chunks = [slice(c * BLK, (c + 1) * BLK) for c in range(tm // BLK)]
    groups = [chunks[g:g + MIX_UNROLL] for g in range(0, len(chunks), MIX_UNROLL)]
    assert len(groups) == 2 and len(PROJ_PIECES) == 6

    h = _rms(x_ref[...], g_ref[0:1, :]).astype(BF16)
    hp = jnp.concatenate([_dot(perm_ref[...], h[c]).astype(BF16) for c in chunks], axis=0)
    project(h, 0)
    pre = [sgu_pre(rows) for rows in groups[0]]
    work0, s1 = ret_scores(groups[0])
    project(h, 1)
    for rows, t in zip(groups[0], pre):
        sgu_post(rows, t)
    s2 = ret_values(work0, s1)
    project(h, 2)
    ret_norm(work0, s2)
    pre = [sgu_pre(rows) for rows in groups[1]]
    work1, s1 = ret_scores(groups[1])
    project(h, 3)
    for rows, t in zip(groups[1], pre):
        sgu_post(rows, t)
    s2 = ret_values(work1, s1)
    project(hp, 4)
    ret_norm(work1, s2)
    project(hp, 5)


def _inmix_kernel(x_ref, g_ref, w_ref, perm_ref, cos_ref, sin_ref, sw_ref, sb_ref, lg_ref, lb_ref,
                  dec_ref, qdec_ref, kdec_ref, cdm_ref, a_ref, yb_ref, yc_ref,
                  zb0, zc0, zb1, zc1, state_ref, wt_ref, wb_ref, *, tiles_per_batch):
    i = pl.program_id(0)

    @pl.when(i == 0)
    def _():
        half = HEAD_DIM // 2
        lane = lax.broadcasted_iota(jnp.int32, (BLK, LANES), 1)
        from_right = (lane >= half) & (lane < 2 * half)
        from_left = (lane >= 2 * half) & (lane < 3 * half)
        qk_tiles = range(QKV_A + UV_B, QKV_A + UV_B + 2 * C_WIDTH, LANES)

        def cast_rows(r, carry):
            rows = pl.ds(pl.multiple_of(r * BLK, BLK), BLK)
            for c0 in range(0, IN_WIDTH, LANES):
                t = w_ref[rows, c0:c0 + LANES]
                if c0 in qk_tiles:
                    t = jnp.where(from_right, pltpu.roll(t, LANES - half, 1),
                                  jnp.where(from_left, pltpu.roll(t, half, 1), t))
                wb_ref[rows, c0:c0 + LANES] = t.astype(BF16)
            return carry

        lax.fori_loop(0, D_MODEL // BLK, cast_rows, 0)
        zb1[...] = jnp.zeros_like(zb1)
        zc1[...] = jnp.zeros_like(zc1)
        row = lax.broadcasted_iota(jnp.int32, (BLK, BLK), 0)
        col = lax.broadcasted_iota(jnp.int32, (BLK, BLK), 1)
        for gp in range(SGU_GROUPS // 2):
            wt_ref[gp] = jnp.concatenate([jnp.where(row >= col, sw_ref[2 * gp + e], 0.0)
                                          for e in range(2)], axis=1).astype(BF16)

    @pl.when((i == 0) | ((i - 1) % tiles_per_batch == 0))
    def _():
        state_ref[...] = jnp.zeros_like(state_ref)

    common = (x_ref, g_ref, wb_ref, perm_ref, cos_ref, sin_ref, sb_ref, lg_ref, lb_ref,
              dec_ref, qdec_ref, kdec_ref, cdm_ref, a_ref, yb_ref, yc_ref)

    @pl.when(i % 2 == 0)
    def _():
        _inmix_body(*common, zb0, zc0, zb1, zc1, state_ref, wt_ref)

    @pl.when(i % 2 == 1)
    def _():
        _inmix_body(*common, zb1, zc1, zb0, zc0, state_ref, wt_ref)


def _qk_lane_head():
    lane = np.arange(C_WIDTH)
    return 2 * (lane // LANES) + ((lane >> 5) & 1)


def _retention_tables():
    f32 = np.float32
    heads = np.arange(RET_HEADS, dtype=f32)
    log_g = np.log1p(-np.power(f32(2.0), f32(-5.0) - heads)).astype(f32)
    idx = np.arange(BLK, dtype=f32)
    rel = idx[:, None] - idx[None, :]
    scale = f32(HEAD_DIM ** -0.5)
    decay = np.where(rel[None] >= 0,
                     np.exp(np.maximum(rel, f32(0.0))[None] * log_g[:, None, None]), f32(0.0)) * scale
    k_dec = np.exp((BLK - 1 - idx)[:, None] * log_g[None, :]) * scale
    q_dec = np.exp((idx + f32(1.0))[:, None] * log_g[None, :])
    lane_head = _qk_lane_head()
    k_dec = k_dec[:, lane_head]
    q_dec = q_dec[:, lane_head]
    chunk_decay = np.exp(f32(BLK) * log_g)
    per_row = chunk_decay[lane_head].reshape(N_PAIRS, LANES)
    row_head = (np.arange(LANES) >> 5) & 1
    col_head = np.arange(LANES) >> 6
    same = row_head[:, None] == col_head[None, :]
    cdm = np.where(same[None], per_row[:, :, None], f32(0.0))
    decay = np.concatenate([decay[0::2], decay[1::2]], axis=2)
    return tuple(t.astype(f32) for t in (decay, q_dec, k_dec, cdm))


def _inproj_mixers(x, g, w, perm, cos, sin, sgu_w, sgu_bias, ln_g, ln_b, l, batch, seq):
    n = x.shape[0]
    tm = ROW_TILE
    nt = n // tm
    decay, q_dec, k_dec, cdm = _retention_tables()
    cur = lambda i: (jnp.minimum(i, nt - 1), 0)
    prev = lambda i: (jnp.maximum(i - 1, 0), 0)
    const2 = lambda i: (0, 0)
    const3 = lambda i: (0, 0, 0)
    return pl.pallas_call(
        functools.partial(_inmix_kernel, tiles_per_batch=seq // tm),
        grid=(nt + 1,),
        in_specs=[pl.BlockSpec((tm, D_MODEL), cur),
                  _layer(g, l), _layer(w, l, pipeline_mode=pl.Buffered(1)),
                  pl.BlockSpec(perm.shape, const2),
                  pl.BlockSpec((tm, LANES), prev),
                  pl.BlockSpec((tm, LANES), prev),
                  _layer(sgu_w, l), _layer(sgu_bias, l), _layer(ln_g, l), _layer(ln_b, l),
                  pl.BlockSpec(decay.shape, const3),
                  pl.BlockSpec(q_dec.shape, const2),
                  pl.BlockSpec(k_dec.shape, const2),
                  pl.BlockSpec(cdm.shape, const3)],
        out_specs=[pl.BlockSpec((tm, QKV_A), cur),
                   pl.BlockSpec((tm, B_WIDTH), prev),
                   pl.BlockSpec((tm, C_WIDTH), prev)],
        out_shape=[jax.ShapeDtypeStruct((n, QKV_A), BF16),
                   jax.ShapeDtypeStruct((n, B_WIDTH), BF16),
                   jax.ShapeDtypeStruct((n, C_WIDTH), BF16)],
        scratch_shapes=[pltpu.VMEM((tm, UV_B), F32), pltpu.VMEM((tm, QKVG_C), F32),
                        pltpu.VMEM((tm, UV_B), F32), pltpu.VMEM((tm, QKVG_C), F32),
                        pltpu.VMEM((N_PAIRS, LANES, LANES), F32),
                        pltpu.VMEM((SGU_GROUPS // 2, BLK, 2 * BLK), BF16),
                        pltpu.VMEM((D_MODEL, IN_WIDTH), BF16)],
        compiler_params=_params("arbitrary"),
        name="inproj_mixers",
    )(x, g, w, perm, cos, sin, sgu_w, sgu_bias, ln_g, ln_b, decay, q_dec, k_dec, cdm)


def _kv_kernel(m_ref, g_ref, w_ref, kv_ref):
    h = _rms(m_ref[...], g_ref[6:7, :]).astype(BF16)
    kv_ref[...] = _dot(h, w_ref[...].astype(BF16)).astype(BF16)


def _memory_kv(mem, g, w, l):
    n = mem.shape[0]
    return pl.pallas_call(
        _kv_kernel,
        grid=(n // MEM_LEN,),
        in_specs=[pl.BlockSpec((MEM_LEN, D_MODEL), lambda i: (i, 0)),
                  _layer(g, l), _layer(w, l)],
        out_specs=pl.BlockSpec((MEM_LEN, 2 * D_MODEL), lambda i: (i, 0)),
        out_shape=jax.ShapeDtypeStruct((n, 2 * D_MODEL), BF16),
        compiler_params=_params("parallel"),
        name="memory_kv",
    )(mem, g, w)


def _post_mix_kernel(ya_ref, yb_ref, yc_ref, x_ref, kv_ref, g_ref, w_ref, unperm_ref, wq_ref, wo_ref,
                     out_ref, wb_ref):
    @pl.when(pl.program_id(0) == 0)
    def _():
        def cast_rows(r, carry):
            rows = pl.ds(pl.multiple_of(r * BLK, BLK), BLK)
            for k, src in enumerate((w_ref, wq_ref, wo_ref)):
                wb_ref[k, rows, :] = src[rows, :].astype(BF16)
            return carry

        lax.fori_loop(0, D_MODEL // BLK, cast_rows, 0)

    tm = x_ref.shape[0]
    n_parts = tm // POST_PART
    halves = [slice(i * POST_PART, (i + 1) * POST_PART) for i in range(n_parts)]
    heads = [slice(hd * X_HEAD_DIM, (hd + 1) * X_HEAD_DIM) for hd in range(X_HEADS)]
    scale = X_HEAD_DIM ** -0.5

    def unpermute(rows):
        n_grp = (rows.stop - rows.start) // BLK
        return jnp.concatenate(
            [_dot(unperm_ref[...], ya_ref[rows.start + g * BLK:rows.start + (g + 1) * BLK, :]).astype(BF16)
             for g in range(n_grp)], axis=0)

    ya = [unpermute(rows) for rows in halves]
    proj = [_dot(jnp.concatenate([ya[i], yb_ref[rows, :], yc_ref[rows, :]], axis=1), wb_ref[0])
            for i, rows in enumerate(halves)]
    x1 = [x_ref[rows, :] + _rms(proj[i], g_ref[1:2, :]) for i, rows in enumerate(halves)]
    q = [_dot(_rms(x1[i], g_ref[2:3, :]).astype(BF16), wb_ref[1]).astype(BF16) for i in range(n_parts)]
    scores = [[_dot_nt(q[i][:, cols], kv_ref[:, cols]) for cols in heads] for i in range(n_parts)]
    att = []
    for i in range(n_parts):
        outs = []
        for hd, s in enumerate(scores[i]):
            s = s * scale
            m = jnp.max(s, axis=-1, keepdims=True)
            p = jnp.exp(s - m)
            p = p / jnp.sum(p, axis=-1, keepdims=True)
            v = kv_ref[:, D_MODEL + hd * X_HEAD_DIM:D_MODEL + (hd + 1) * X_HEAD_DIM]
            outs.append(_dot(p.astype(BF16), v).astype(BF16))
        att.append(jnp.concatenate(outs, axis=1))
    acc = [_dot(att[i], wb_ref[2]) for i in range(n_parts)]
    for i, rows in enumerate(halves):
        out_ref[rows, :] = x1[i] + _rms(acc[i], g_ref[3:4, :])


def _post_mix(ya, yb, yc, x, kv, g, w_out, unperm, wq, wo, l, batch, seq):
    n = x.shape[0]
    tm = POST_TILE
    per_batch = seq // tm
    row = lambda i: (i, 0)
    const = lambda i: (0, 0)
    return pl.pallas_call(
        _post_mix_kernel,
        grid=(n // tm,),
        in_specs=[pl.BlockSpec((tm, A_WIDTH), row),
                  pl.BlockSpec((tm, B_WIDTH), row),
                  pl.BlockSpec((tm, C_WIDTH), row),
                  pl.BlockSpec((tm, D_MODEL), row),
                  pl.BlockSpec((MEM_LEN, 2 * D_MODEL), lambda i: (i // per_batch, 0)),
                  _layer(g, l), _layer(w_out, l, pipeline_mode=pl.Buffered(1)),
                  pl.BlockSpec(unperm.shape, const),
                  _layer(wq, l, pipeline_mode=pl.Buffered(1)), _layer(wo, l, pipeline_mode=pl.Buffered(1))],
        out_specs=pl.BlockSpec((tm, D_MODEL), row),
        out_shape=jax.ShapeDtypeStruct((n, D_MODEL), F32),
        scratch_shapes=[pltpu.VMEM((3, D_MODEL, D_MODEL), BF16)],
        compiler_params=_params("arbitrary"),
        name="outproj_cross",
    )(ya, yb, yc, x, kv, g, w_out, unperm, wq, wo)


def _mlp_kernel(x_ref, g_ref, wu_ref, wd_ref, out_ref):
    x = x_ref[...]
    h = _rms(x, g_ref[4:5, :]).astype(BF16)
    acc = jnp.zeros(x.shape, F32)
    for c in range(D_FF // FF_CHUNK):
        cols = slice(c * FF_CHUNK, (c + 1) * FF_CHUNK)
        f = jnp.square(jnp.maximum(_dot(h, wu_ref[:, cols].astype(BF16)), 0.0))
        acc += _dot(f.astype(BF16), wd_ref[cols, :].astype(BF16))
    out_ref[...] = x + _rms(acc, g_ref[5:6, :])


def _mlp(x, g, w_up, w_down, l):
    n = x.shape[0]
    tm = MLP_TILE
    row = lambda i: (i, 0)
    const = lambda i: (0, 0)
    return pl.pallas_call(
        _mlp_kernel,
        grid=(n // tm,),
        in_specs=[pl.BlockSpec((tm, D_MODEL), row),
                  _layer(g, l),
                  _layer(w_up, l, pipeline_mode=pl.Buffered(1)),
                  _layer(w_down, l, pipeline_mode=pl.Buffered(1))],
        out_specs=pl.BlockSpec((tm, D_MODEL), row),
        out_shape=jax.ShapeDtypeStruct((n, D_MODEL), F32),
        compiler_params=_params("parallel"),
        name="mlp",
    )(x, g, w_up, w_down)


def kernel(x, mem, positions, norm_g, w_in, sgu_w, sgu_b, sgu_ln_g, sgu_ln_b,
           w_out, x_wq, x_wkv, x_wo, w_up, w_down):
    batch, seq, _ = x.shape
    assert x.shape == (batch, seq, D_MODEL) and seq % (BLK * DIL_PATTERNS[-1][1]) == 0
    assert (batch * seq) % ROW_TILE == 0 and seq % ROW_TILE == 0
    assert mem.shape == (batch, MEM_LEN, D_MODEL) and w_in.shape == (DEPTH, D_MODEL, IN_WIDTH)
    xf = x.reshape(batch * seq, D_MODEL)
    memf = mem.reshape(batch * MEM_LEN, D_MODEL)
    cos, sin = _rope_tables(positions)
    perm = _group_permutation()
    sgu_bias = jnp.repeat(jnp.swapaxes(sgu_b, 1, 2), HEAD_DIM, axis=2)
    ln_g = sgu_ln_g.reshape(DEPTH, 1, B_WIDTH)
    ln_b = sgu_ln_b.reshape(DEPTH, 1, B_WIDTH)
    for l in range(DEPTH):
        za, yb, yc = _inproj_mixers(xf, norm_g, w_in, perm, cos, sin, sgu_w, sgu_bias, ln_g, ln_b,
                                    l, batch, seq)
        ya = _dilated_attention(za, batch, seq)
        kv = _memory_kv(memf, norm_g, x_wkv, l)
        xf = _post_mix(ya, yb, yc, xf, kv, norm_g, w_out, perm.T, x_wq, x_wo, l, batch, seq)
        xf = _mlp(xf, norm_g, w_up, w_down, l)
    return xf.reshape(batch, seq, D_MODEL)
```

```python
import functools

import jax
import jax.numpy as jnp
import numpy as np
from jax import lax
from jax.experimental import pallas as pl
from jax.experimental.pallas import tpu as pltpu

F32 = jnp.float32
BF16 = jnp.bfloat16

D_MODEL = 1024
DEPTH = 2
MEM_LEN = 256
HEAD_DIM = 64
DIL_HEADS = 6
DIL_PATTERNS = ((128, 1), (512, 4), (2048, 16))
BLK = 128
SGU_GROUPS = 4
RET_HEADS = 6
ROPE_BASE = 10000.0
A_WIDTH = DIL_HEADS * HEAD_DIM
B_WIDTH = SGU_GROUPS * HEAD_DIM
C_WIDTH = RET_HEADS * HEAD_DIM
QKV_A = 3 * A_WIDTH
UV_B = 2 * B_WIDTH
QKVG_C = 4 * C_WIDTH
IN_WIDTH = QKV_A + UV_B + QKVG_C
X_HEADS = 4
X_HEAD_DIM = D_MODEL // X_HEADS
D_FF = 4 * D_MODEL
EPS = 1e-6

LANES = 128
N_PAIRS = A_WIDTH // LANES
VMEM_LIMIT = 56 * 1024 * 1024

SUPER = BLK * DIL_PATTERNS[-1][1]

ROW_TILE = 512
DIL_UNROLL = (4, 4, 4)
DIL_LOOKAHEAD = (12, 2, 2)
MIX_UNROLL = 2
POST_TILE = 1024
POST_PART = 256
MLP_TILE = 512
FF_CHUNK = 1024


def _params(*sem):
    return pltpu.CompilerParams(dimension_semantics=sem, vmem_limit_bytes=VMEM_LIMIT)


def _layer(arr, l, **kw):
    zeros = (0,) * (arr.ndim - 1)
    return pl.BlockSpec((None,) + arr.shape[1:], lambda *_: (l,) + zeros, **kw)


def _rms(x, g):
    ms = jnp.mean(x * x, axis=-1, keepdims=True)
    return x * lax.rsqrt(ms + EPS) * g


def _dot(a, b):
    return jnp.dot(a, b, preferred_element_type=F32)


def _dot_nt(a, b):
    return lax.dot_general(a, b, (((1,), (1,)), ((), ())), preferred_element_type=F32)


def _dot_tn(a, b):
    return lax.dot_general(a, b, (((0,), (0,)), ((), ())), preferred_element_type=F32)


def _lane_lo(shape):
    lane = lax.broadcasted_iota(jnp.int32, shape, len(shape) - 1)
    return (lane & HEAD_DIM) == 0


ROPE_PACK = LANES // (HEAD_DIM // 2)


def _rope_kernel(pos_ref, inv_ref, cos_ref, sin_ref):
    half = HEAD_DIM // 2
    ang = pos_ref[...].astype(F32) * inv_ref[...]
    cos, sin = jnp.cos(ang), jnp.sin(ang)
    lane = lax.broadcasted_iota(jnp.int32, ang.shape, 1)
    sign = jnp.where(lane < HEAD_DIM, -1.0, 1.0)
    rows = ang.shape[0]
    for t in range(ROPE_PACK):
        mine = (lane >= t * half) & (lane < (t + 1) * half)
        for table, out_ref, scale in ((cos, cos_ref, None), (sin, sin_ref, sign)):
            y = jnp.where(mine, table, 0.0)
            y = y + pltpu.roll(y, 2 * half, 1)
            y = y + pltpu.roll(y, half, 1)
            out_ref[pl.ds(t, rows, stride=ROPE_PACK), :] = y if scale is None else y * scale


def _rope_tables(positions):
    n = positions.size
    half = HEAD_DIM // 2
    inv = 1.0 / (ROPE_BASE ** jnp.linspace(0.0, 1.0, half, dtype=F32))
    inv = jnp.tile(inv, ROPE_PACK).reshape(1, LANES)
    pos = jnp.repeat(positions.reshape(n // ROPE_PACK, ROPE_PACK), half, axis=1)
    tm = 2048
    return pl.pallas_call(
        _rope_kernel,
        grid=(n // tm,),
        in_specs=[pl.BlockSpec((tm // ROPE_PACK, LANES), lambda i: (i, 0)),
                  pl.BlockSpec((1, LANES), lambda i: (0, 0))],
        out_specs=[pl.BlockSpec((tm, LANES), lambda i: (i, 0))] * 2,
        out_shape=[jax.ShapeDtypeStruct((n, LANES), F32)] * 2,
        compiler_params=_params("parallel"),
        name="rope_tables",
    )(pos, inv)


MAX_DIL = DIL_PATTERNS[-1][1]
SUB = 8


def _group_permutation():
    i = np.arange(BLK)
    src = MAX_DIL * (i % SUB) + i // SUB
    return jnp.asarray(src[:, None] == np.arange(BLK)[None, :], BF16)


def _run_offsets(d):
    return [BLK * g + SUB * d * c for g in range(d) for c in range(MAX_DIL // d)]


def _block_load(ref, p, base, d):
    if d == 1:
        return ref[p, pl.ds(pl.multiple_of(base, BLK), BLK), :]
    return jnp.concatenate(
        [ref[p, pl.ds(pl.multiple_of(base + off, SUB), SUB), :] for off in _run_offsets(d)], axis=0)


def _block_store(ref, p, base, d, val):
    if d == 1:
        ref[p, pl.ds(pl.multiple_of(base, BLK), BLK), :] = val
        return
    for i, off in enumerate(_run_offsets(d)):
        ref[p, pl.ds(pl.multiple_of(base + off, SUB), SUB), :] = val[i * SUB:(i + 1) * SUB]


def _dil_blocks(refs, branch, d, blocks, first, lookahead):
    bias_ref, qp, kring, vring, acc_o, acc_m, acc_l = refs
    lo = _lane_lo((BLK, LANES))
    units = [(blk, p) for blk in blocks for p in range(N_PAIRS)]

    def scores(unit):
        (qbase, kcbase, kpbase, _), p = unit
        q = _block_load(qp, p, qbase, d).astype(BF16)
        k = jnp.concatenate([_block_load(kring, p, kpbase, d), _block_load(kring, p, kcbase, d)],
                            axis=0).astype(BF16)
        return [_dot_nt(jnp.where(lo if e == 0 else ~lo, q, jnp.zeros_like(q)), k) for e in range(2)]

    def softmax_pv(unit, s_pair):
        (qbase, kcbase, kpbase, has_prev), p = unit
        bias_prev = jnp.where(has_prev, bias_ref[branch, :, :BLK], -jnp.inf)
        bias_cur = bias_ref[branch, :, BLK:]
        v = jnp.concatenate([_block_load(vring, p, kpbase, d), _block_load(vring, p, kcbase, d)],
                            axis=0).astype(BF16)
        o_h, m_h, l_h = [], [], []
        for s in s_pair:
            sp = s[:, :BLK] + bias_prev
            sc = s[:, BLK:] + bias_cur
            m = jnp.max(jnp.maximum(sp, sc), axis=-1, keepdims=True)
            pp = jnp.exp(sp - m)
            pc = jnp.exp(sc - m)
            l_h.append(jnp.sum(pp + pc, axis=-1, keepdims=True))
            m_h.append(m)
            o_h.append(_dot(jnp.concatenate([pp, pc], axis=1).astype(BF16), v))
        return o_h, m_h, l_h

    def merge(unit, stats):
        (qbase, _, _, _), p = unit
        o_h, m_h, l_h = stats
        o_b = jnp.where(lo, o_h[0], o_h[1])
        m_b = jnp.where(lo, m_h[0], m_h[1])
        l_b = jnp.where(lo, l_h[0], l_h[1])
        if first:
            _block_store(acc_o, p, qbase, d, o_b)
            _block_store(acc_m, p, qbase, d, m_b)
            _block_store(acc_l, p, qbase, d, l_b)
        else:
            m_old = _block_load(acc_m, p, qbase, d)
            m_new = jnp.maximum(m_old, m_b)
            a_old = jnp.exp(m_old - m_new)
            a_b = jnp.exp(m_b - m_new)
            _block_store(acc_o, p, qbase, d, _block_load(acc_o, p, qbase, d) * a_old + o_b * a_b)
            _block_store(acc_l, p, qbase, d, _block_load(acc_l, p, qbase, d) * a_old + l_b * a_b)
            _block_store(acc_m, p, qbase, d, m_new)

    ahead = min(lookahead, len(units))
    pending = [scores(u) for u in units[:ahead]]
    stats_prev = None
    for i, unit in enumerate(units):
        if i + ahead < len(units):
            pending.append(scores(units[i + ahead]))
        stats = softmax_pv(unit, pending[i])
        if stats_prev is not None:
            merge(units[i - 1], stats_prev)
        stats_prev = stats
    merge(units[-1], stats_prev)


def _dil_kernel(q_ref, k_ref, v_ref, bias_ref, o_ref, qp, kring, vring, acc_o, acc_m, acc_l):
    t = pl.program_id(1)
    cur = pl.multiple_of((t % 2) * SUPER, SUPER)
    other = pl.multiple_of(SUPER - cur, SUPER)

    @pl.when(t == 0)
    def _():
        kring[:, SUPER:, :] = jnp.zeros((N_PAIRS, SUPER, LANES), F32)
        vring[:, SUPER:, :] = jnp.zeros((N_PAIRS, SUPER, LANES), F32)

    for p in range(N_PAIRS):
        cols = slice(p * LANES, (p + 1) * LANES)
        qp[p] = q_ref[:, cols].astype(F32) * (HEAD_DIM ** -0.5)
        kring[p, pl.ds(cur, SUPER), :] = k_ref[:, cols].astype(F32)
        vring[p, pl.ds(cur, SUPER), :] = v_ref[:, cols].astype(F32)
    refs = (bias_ref, qp, kring, vring, acc_o, acc_m, acc_l)

    for branch, (window, d) in enumerate(DIL_PATTERNS):
        nblk = MAX_DIL // d
        span = BLK * d

        def block_rows(j, d=d, nblk=nblk, span=span):
            if nblk == 1:
                qbase = pl.multiple_of(j * SUB, SUB)
                return qbase, cur + qbase, other + qbase, t > 0
            r = j // nblk
            nb = j % nblk
            qbase = pl.multiple_of(nb * span + r * SUB, SUB)
            kpbase = pl.multiple_of(
                jnp.where(nb == 0, other + (nblk - 1) * span + r * SUB, cur + qbase - span), SUB)
            return qbase, cur + qbase, kpbase, (t > 0) | (nb > 0)

        unroll, lookahead = DIL_UNROLL[branch], DIL_LOOKAHEAD[branch]

        def body(i, carry, branch=branch, d=d, block_rows=block_rows, unroll=unroll, lookahead=lookahead):
            blocks = [block_rows(i * unroll + u) for u in range(unroll)]
            _dil_blocks(refs, branch, d, blocks, first=(branch == 0), lookahead=lookahead)
            return carry

        lax.fori_loop(0, SUPER // BLK // unroll, body, 0)

    def finish(i, carry):
        rows = pl.ds(pl.multiple_of(i * BLK, BLK), BLK)
        for p in range(N_PAIRS):
            o_ref[rows, p * LANES:(p + 1) * LANES] = (acc_o[p, rows, :] / acc_l[p, rows, :]).astype(BF16)
        return carry

    lax.fori_loop(0, SUPER // BLK, finish, 0)


def _dilated_bias():
    out = []
    i = np.arange(BLK)
    for _, d in DIL_PATTERNS:
        n_c = MAX_DIL // d
        pos = (BLK // d) * (i // (SUB * n_c)) + n_c * (i % SUB) + (i // SUB) % n_c
        prev_ok = pos[None, :] >= pos[:, None]
        cur_ok = pos[None, :] <= pos[:, None]
        out.append(np.where(np.concatenate([prev_ok, cur_ok], axis=1), 0.0, -np.inf))
    return np.stack(out).astype(np.float32)


def _dilated_attention(qkv, batch, seq):
    n = batch * seq
    tiles = seq // SUPER
    bias = _dilated_bias()
    spec = lambda which: pl.BlockSpec((SUPER, A_WIDTH), lambda b, t: (b * tiles + t, which))
    tile_f32 = pltpu.VMEM((N_PAIRS, SUPER, LANES), F32)
    ring_f32 = pltpu.VMEM((N_PAIRS, 2 * SUPER, LANES), F32)
    return pl.pallas_call(
        _dil_kernel,
        grid=(batch, tiles),
        in_specs=[spec(0), spec(1), spec(2), pl.BlockSpec(bias.shape, lambda b, t: (0, 0, 0))],
        out_specs=pl.BlockSpec((SUPER, A_WIDTH), lambda b, t: (b * tiles + t, 0)),
        out_shape=jax.ShapeDtypeStruct((n, A_WIDTH), BF16),
        scratch_shapes=[tile_f32, ring_f32, ring_f32, tile_f32, tile_f32, tile_f32],
        compiler_params=_params("parallel", "arbitrary"),
        name="dilated_attention",
    )(qkv, qkv, qkv, bias)


def _rotate(t, cos, sin_signed):
    return t * cos + pltpu.roll(t, HEAD_DIM, 1) * sin_signed


PROJ_PIECES = (
    (QKV_A + UV_B, 512), (QKV_A + UV_B + 512, 512), (QKV_A + UV_B + 1024, 512),
    (QKV_A, UV_B),
    (0, 640), (640, QKV_A - 640))


def _inmix_body(x_ref, g_ref, w_ref, perm_ref, cos_ref, sin_ref, sb_ref, lg_ref, lb_ref,
                dec_ref, qdec_ref, kdec_ref, cdm_ref, a_ref, yb_ref, yc_ref,
                zb_w, zc_w, zb_r, zc_r, state_ref, wt_ref):
    lo = _lane_lo((BLK, LANES))
    row = lax.broadcasted_iota(jnp.int32, (BLK, BLK), 0)
    col = lax.broadcasted_iota(jnp.int32, (BLK, BLK), 1)
    same_head = ((row >> 5) & 1) == (col >> 6)
    lane = lax.broadcasted_iota(jnp.int32, (BLK, LANES), 1)
    qk_head0 = (lane & (HEAD_DIM // 2)) == 0
    inv_n = 1.0 / HEAD_DIM

    def sgu_pre(rows):
        u = jax.nn.gelu(zb_r[rows, :B_WIDTH])
        v = jax.nn.gelu(zb_r[rows, B_WIDTH:])
        mu = jnp.mean(v, axis=-1, keepdims=True)
        var = jnp.mean(jnp.square(v - mu), axis=-1, keepdims=True)
        v = (v - mu) * lax.rsqrt(var + EPS) * lg_ref[...] + lb_ref[...]
        return u, v.astype(BF16)

    def sgu_post(rows, pre):
        u, v = pre
        for gp in range(B_WIDTH // LANES):
            cols = slice(gp * LANES, (gp + 1) * LANES)
            vp = v[:, cols]
            v2 = jnp.concatenate([jnp.where(lo, vp, jnp.zeros_like(vp)),
                                  jnp.where(lo, jnp.zeros_like(vp), vp)], axis=0)
            gate = _dot(wt_ref[gp], v2) + sb_ref[:, cols]
            yb_ref[rows, cols] = (u[:, cols] * gate).astype(BF16)

    def ret_scores(row_list):
        work = [(rows, p) for rows in row_list for p in range(N_PAIRS)]
        out = []
        for rows, p in work:
            cols = slice(p * LANES, (p + 1) * LANES)
            cos, sin = cos_ref[rows, :], sin_ref[rows, :]
            q = _rotate(zc_r[rows, p * LANES:(p + 1) * LANES], cos, sin)
            k = _rotate(zc_r[rows, C_WIDTH + p * LANES:C_WIDTH + (p + 1) * LANES], cos, sin)
            vb = zc_r[rows, 2 * C_WIDTH + p * LANES:2 * C_WIDTH + (p + 1) * LANES].astype(BF16)
            qb, kb = q.astype(BF16), k.astype(BF16)
            k2 = jnp.concatenate([jnp.where(qk_head0, kb, jnp.zeros_like(kb)),
                                  jnp.where(qk_head0, jnp.zeros_like(kb), kb)], axis=0)
            scores = _dot_nt(qb, k2)
            kv = _dot_tn((k * kdec_ref[:, cols]).astype(BF16), vb)
            out.append((scores, kv, (q * qdec_ref[:, cols]).astype(BF16), vb))
        return work, out

    def ret_values(work, stage1):
        out = []
        for (rows, p), (scores, kv, qd, vb) in zip(work, stage1):
            state = state_ref[p]
            y_x = _dot(qd, state.astype(BF16))
            state_ref[p] = state * cdm_ref[p] + jnp.where(same_head, kv, 0.0)
            v2 = jnp.concatenate([jnp.where(lo, vb, jnp.zeros_like(vb)),
                                  jnp.where(lo, jnp.zeros_like(vb), vb)], axis=0)
            y_in = _dot((scores * dec_ref[p]).astype(BF16), v2)
            out.append((y_in, y_x))
        return out

    def ret_norm(work, stage2):
        for (rows, p), (y_in, y_x) in zip(work, stage2):
            cols = slice(p * LANES, (p + 1) * LANES)
            y = y_in + y_x
            mu_lo = jnp.sum(jnp.where(lo, y, 0.0), axis=-1, keepdims=True) * inv_n
            mu_hi = jnp.sum(jnp.where(lo, 0.0, y), axis=-1, keepdims=True) * inv_n
            yc_ = y - jnp.where(lo, mu_lo, mu_hi)
            sq = yc_ * yc_
            var_lo = jnp.sum(jnp.where(lo, sq, 0.0), axis=-1, keepdims=True) * inv_n
            var_hi = jnp.sum(jnp.where(lo, 0.0, sq), axis=-1, keepdims=True) * inv_n
            yn = yc_ * lax.rsqrt(jnp.where(lo, var_lo, var_hi) + EPS)
            gate = zc_r[rows, 3 * C_WIDTH + p * LANES:3 * C_WIDTH + (p + 1) * LANES]
            yc_ref[rows, cols] = (jax.nn.silu(gate) * yn).astype(BF16)

    def project(lhs, piece):
        first, width = PROJ_PIECES[piece]
        y = _dot(lhs, w_ref[:, first:first + width])
        if first >= QKV_A + UV_B:
            zc_w[:, first - QKV_A - UV_B:first - QKV_A - UV_B + width] = y
        elif first >= QKV_A:
            zb_w[...] = y
        else:
            a_ref[:, first:first + width] = y.astype(BF16)

    tm = x_ref.shape[0]
    chunks = [slice(c * BLK, (c + 1) * BLK) for c in range(tm // BLK)]
    groups = [chunks[g:g + MIX_UNROLL] for g in range(0, len(chunks), MIX_UNROLL)]
    assert len(groups) == 2 and len(PROJ_PIECES) == 6

    h = _rms(x_ref[...], g_ref[0:1, :]).astype(BF16)
    hp = jnp.concatenate([_dot(perm_ref[...], h[c]).astype(BF16) for c in chunks], axis=0)
    project(h, 0)
    pre = [sgu_pre(rows) for rows in groups[0]]
    work0, s1 = ret_scores(groups[0])
    project(h, 1)
    for rows, t in zip(groups[0], pre):
        sgu_post(rows, t)
    s2 = ret_values(work0, s1)
    project(h, 2)
    ret_norm(work0, s2)
    pre = [sgu_pre(rows) for rows in groups[1]]
    work1, s1 = ret_scores(groups[1])
    project(h, 3)
    for rows, t in zip(groups[1], pre):
        sgu_post(rows, t)
    s2 = ret_values(work1, s1)
    project(hp, 4)
    ret_norm(work1, s2)
    project(hp, 5)


def _inmix_kernel(x_ref, g_ref, w_ref, perm_ref, cos_ref, sin_ref, sw_ref, sb_ref, lg_ref, lb_ref,
                  dec_ref, qdec_ref, kdec_ref, cdm_ref, a_ref, yb_ref, yc_ref,
                  zb0, zc0, zb1, zc1, state_ref, wt_ref, wb_ref, *, tiles_per_batch):
    i = pl.program_id(0)

    @pl.when(i == 0)
    def _():
        half = HEAD_DIM // 2
        lane = lax.broadcasted_iota(jnp.int32, (BLK, LANES), 1)
        from_right = (lane >= half) & (lane < 2 * half)
        from_left = (lane >= 2 * half) & (lane < 3 * half)
        qk_tiles = range(QKV_A + UV_B, QKV_A + UV_B + 2 * C_WIDTH, LANES)

        def cast_rows(r, carry):
            rows = pl.ds(pl.multiple_of(r * BLK, BLK), BLK)
            for c0 in range(0, IN_WIDTH, LANES):
                t = w_ref[rows, c0:c0 + LANES]
                if c0 in qk_tiles:
                    t = jnp.where(from_right, pltpu.roll(t, LANES - half, 1),
                                  jnp.where(from_left, pltpu.roll(t, half, 1), t))
                wb_ref[rows, c0:c0 + LANES] = t.astype(BF16)
            return carry

        lax.fori_loop(0, D_MODEL // BLK, cast_rows, 0)
        zb1[...] = jnp.zeros_like(zb1)
        zc1[...] = jnp.zeros_like(zc1)
        row = lax.broadcasted_iota(jnp.int32, (BLK, BLK), 0)
        col = lax.broadcasted_iota(jnp.int32, (BLK, BLK), 1)
        for gp in range(SGU_GROUPS // 2):
            wt_ref[gp] = jnp.concatenate([jnp.where(row >= col, sw_ref[2 * gp + e], 0.0)
                                          for e in range(2)], axis=1).astype(BF16)

    @pl.when((i == 0) | ((i - 1) % tiles_per_batch == 0))
    def _():
        state_ref[...] = jnp.zeros_like(state_ref)

    common = (x_ref, g_ref, wb_ref, perm_ref, cos_ref, sin_ref, sb_ref, lg_ref, lb_ref,
              dec_ref, qdec_ref, kdec_ref, cdm_ref, a_ref, yb_ref, yc_ref)

    @pl.when(i % 2 == 0)
    def _():
        _inmix_body(*common, zb0, zc0, zb1, zc1, state_ref, wt_ref)

    @pl.when(i % 2 == 1)
    def _():
        _inmix_body(*common, zb1, zc1, zb0, zc0, state_ref, wt_ref)


def _qk_lane_head():
    lane = np.arange(C_WIDTH)
    return 2 * (lane // LANES) + ((lane >> 5) & 1)


def _retention_tables():
    f32 = np.float32
    heads = np.arange(RET_HEADS, dtype=f32)
    log_g = np.log1p(-np.power(f32(2.0), f32(-5.0) - heads)).astype(f32)
    idx = np.arange(BLK, dtype=f32)
    rel = idx[:, None] - idx[None, :]
    scale = f32(HEAD_DIM ** -0.5)
    decay = np.where(rel[None] >= 0,
                     np.exp(np.maximum(rel, f32(0.0))[None] * log_g[:, None, None]), f32(0.0)) * scale
    k_dec = np.exp((BLK - 1 - idx)[:, None] * log_g[None, :]) * scale
    q_dec = np.exp((idx + f32(1.0))[:, None] * log_g[None, :])
    lane_head = _qk_lane_head()
    k_dec = k_dec[:, lane_head]
    q_dec = q_dec[:, lane_head]
    chunk_decay = np.exp(f32(BLK) * log_g)
    per_row = chunk_decay[lane_head].reshape(N_PAIRS, LANES)
    row_head = (np.arange(LANES) >> 5) & 1
    col_head = np.arange(LANES) >> 6
    same = row_head[:, None] == col_head[None, :]
    cdm = np.where(same[None], per_row[:, :, None], f32(0.0))
    decay = np.concatenate([decay[0::2], decay[1::2]], axis=2)
    return tuple(t.astype(f32) for t in (decay, q_dec, k_dec, cdm))


def _inproj_mixers(x, g, w, perm, cos, sin, sgu_w, sgu_bias, ln_g, ln_b, l, batch, seq):
    n = x.shape[0]
    tm = ROW_TILE
    nt = n // tm
    decay, q_dec, k_dec, cdm = _retention_tables()
    cur = lambda i: (jnp.minimum(i, nt - 1), 0)
    prev = lambda i: (jnp.maximum(i - 1, 0), 0)
    const2 = lambda i: (0, 0)
    const3 = lambda i: (0, 0, 0)
    return pl.pallas_call(
        functools.partial(_inmix_kernel, tiles_per_batch=seq // tm),
        grid=(nt + 1,),
        in_specs=[pl.BlockSpec((tm, D_MODEL), cur),
                  _layer(g, l), _layer(w, l, pipeline_mode=pl.Buffered(1)),
                  pl.BlockSpec(perm.shape, const2),
                  pl.BlockSpec((tm, LANES), prev),
                  pl.BlockSpec((tm, LANES), prev),
                  _layer(sgu_w, l), _layer(sgu_bias, l), _layer(ln_g, l), _layer(ln_b, l),
                  pl.BlockSpec(decay.shape, const3),
                  pl.BlockSpec(q_dec.shape, const2),
                  pl.BlockSpec(k_dec.shape, const2),
                  pl.BlockSpec(cdm.shape, const3)],
        out_specs=[pl.BlockSpec((tm, QKV_A), cur),
                   pl.BlockSpec((tm, B_WIDTH), prev),
                   pl.BlockSpec((tm, C_WIDTH), prev)],
        out_shape=[jax.ShapeDtypeStruct((n, QKV_A), BF16),
                   jax.ShapeDtypeStruct((n, B_WIDTH), BF16),
                   jax.ShapeDtypeStruct((n, C_WIDTH), BF16)],
        scratch_shapes=[pltpu.VMEM((tm, UV_B), F32), pltpu.VMEM((tm, QKVG_C), F32),
                        pltpu.VMEM((tm, UV_B), F32), pltpu.VMEM((tm, QKVG_C), F32),
                        pltpu.VMEM((N_PAIRS, LANES, LANES), F32),
                        pltpu.VMEM((SGU_GROUPS // 2, BLK, 2 * BLK), BF16),
                        pltpu.VMEM((D_MODEL, IN_WIDTH), BF16)],
        compiler_params=_params("arbitrary"),
        name="inproj_mixers",
    )(x, g, w, perm, cos, sin, sgu_w, sgu_bias, ln_g, ln_b, decay, q_dec, k_dec, cdm)


def _kv_kernel(m_ref, g_ref, w_ref, kv_ref):
    h = _rms(m_ref[...], g_ref[6:7, :]).astype(BF16)
    kv_ref[...] = _dot(h, w_ref[...].astype(BF16)).astype(BF16)


def _memory_kv(mem, g, w, l):
    n = mem.shape[0]
    return pl.pallas_call(
        _kv_kernel,
        grid=(n // MEM_LEN,),
        in_specs=[pl.BlockSpec((MEM_LEN, D_MODEL), lambda i: (i, 0)),
                  _layer(g, l), _layer(w, l)],
        out_specs=pl.BlockSpec((MEM_LEN, 2 * D_MODEL), lambda i: (i, 0)),
        out_shape=jax.ShapeDtypeStruct((n, 2 * D_MODEL), BF16),
        compiler_params=_params("parallel"),
        name="memory_kv",
    )(mem, g, w)


def _post_mix_kernel(ya_ref, yb_ref, yc_ref, x_ref, kv_ref, g_ref, w_ref, unperm_ref, wq_ref, wo_ref,
                     out_ref, wb_ref):
    @pl.when(pl.program_id(0) == 0)
    def _():
        def cast_rows(r, carry):
            rows = pl.ds(pl.multiple_of(r * BLK, BLK), BLK)
            for k, src in enumerate((w_ref, wq_ref, wo_ref)):
                wb_ref[k, rows, :] = src[rows, :].astype(BF16)
            return carry

        lax.fori_loop(0, D_MODEL // BLK, cast_rows, 0)

    tm = x_ref.shape[0]
    n_parts = tm // POST_PART
    halves = [slice(i * POST_PART, (i + 1) * POST_PART) for i in range(n_parts)]
    heads = [slice(hd * X_HEAD_DIM, (hd + 1) * X_HEAD_DIM) for hd in range(X_HEADS)]
    scale = X_HEAD_DIM ** -0.5

    def unpermute(rows):
        n_grp = (rows.stop - rows.start) // BLK
        return jnp.concatenate(
            [_dot(unperm_ref[...], ya_ref[rows.start + g * BLK:rows.start + (g + 1) * BLK, :]).astype(BF16)
             for g in range(n_grp)], axis=0)

    ya = [unpermute(rows) for rows in halves]
    proj = [_dot(jnp.concatenate([ya[i], yb_ref[rows, :], yc_ref[rows, :]], axis=1), wb_ref[0])
            for i, rows in enumerate(halves)]
    x1 = [x_ref[rows, :] + _rms(proj[i], g_ref[1:2, :]) for i, rows in enumerate(halves)]
    q = [_dot(_rms(x1[i], g_ref[2:3, :]).astype(BF16), wb_ref[1]).astype(BF16) for i in range(n_parts)]
    scores = [[_dot_nt(q[i][:, cols], kv_ref[:, cols]) for cols in heads] for i in range(n_parts)]
    att = []
    for i in range(n_parts):
        outs = []
        for hd, s in enumerate(scores[i]):
            s = s * scale
            m = jnp.max(s, axis=-1, keepdims=True)
            p = jnp.exp(s - m)
            p = p / jnp.sum(p, axis=-1, keepdims=True)
            v = kv_ref[:, D_MODEL + hd * X_HEAD_DIM:D_MODEL + (hd + 1) * X_HEAD_DIM]
            outs.append(_dot(p.astype(BF16), v).astype(BF16))
        att.append(jnp.concatenate(outs, axis=1))
    acc = [_dot(att[i], wb_ref[2]) for i in range(n_parts)]
    for i, rows in enumerate(halves):
        out_ref[rows, :] = x1[i] + _rms(acc[i], g_ref[3:4, :])


def _post_mix(ya, yb, yc, x, kv, g, w_out, unperm, wq, wo, l, batch, seq):
    n = x.shape[0]
    tm = POST_TILE
    per_batch = seq // tm
    row = lambda i: (i, 0)
    const = lambda i: (0, 0)
    return pl.pallas_call(
        _post_mix_kernel,
        grid=(n // tm,),
        in_specs=[pl.BlockSpec((tm, A_WIDTH), row),
                  pl.BlockSpec((tm, B_WIDTH), row),
                  pl.BlockSpec((tm, C_WIDTH), row),
                  pl.BlockSpec((tm, D_MODEL), row),
                  pl.BlockSpec((MEM_LEN, 2 * D_MODEL), lambda i: (i // per_batch, 0)),
                  _layer(g, l), _layer(w_out, l, pipeline_mode=pl.Buffered(1)),
                  pl.BlockSpec(unperm.shape, const),
                  _layer(wq, l, pipeline_mode=pl.Buffered(1)), _layer(wo, l, pipeline_mode=pl.Buffered(1))],
        out_specs=pl.BlockSpec((tm, D_MODEL), row),
        out_shape=jax.ShapeDtypeStruct((n, D_MODEL), F32),
        scratch_shapes=[pltpu.VMEM((3, D_MODEL, D_MODEL), BF16)],
        compiler_params=_params("arbitrary"),
        name="outproj_cross",
    )(ya, yb, yc, x, kv, g, w_out, unperm, wq, wo)


def _mlp_kernel(x_ref, g_ref, wu_ref, wd_ref, out_ref):
    x = x_ref[...]
    h = _rms(x, g_ref[4:5, :]).astype(BF16)
    acc = jnp.zeros(x.shape, F32)
    for c in range(D_FF // FF_CHUNK):
        cols = slice(c * FF_CHUNK, (c + 1) * FF_CHUNK)
        f = jnp.square(jnp.maximum(_dot(h, wu_ref[:, cols].astype(BF16)), 0.0))
        acc += _dot(f.astype(BF16), wd_ref[cols, :].astype(BF16))
    out_ref[...] = x + _rms(acc, g_ref[5:6, :])


def _mlp(x, g, w_up, w_down, l):
    n = x.shape[0]
    tm = MLP_TILE
    row = lambda i: (i, 0)
    const = lambda i: (0, 0)
    return pl.pallas_call(
        _mlp_kernel,
        grid=(n // tm,),
        in_specs=[pl.BlockSpec((tm, D_MODEL), row),
                  _layer(g, l),
                  _layer(w_up, l, pipeline_mode=pl.Buffered(1)),
                  _layer(w_down, l, pipeline_mode=pl.Buffered(1))],
        out_specs=pl.BlockSpec((tm, D_MODEL), row),
        out_shape=jax.ShapeDtypeStruct((n, D_MODEL), F32),
        compiler_params=_params("parallel"),
        name="mlp",
    )(x, g, w_up, w_down)


def kernel(x, mem, positions, norm_g, w_in, sgu_w, sgu_b, sgu_ln_g, sgu_ln_b,
           w_out, x_wq, x_wkv, x_wo, w_up, w_down):
    batch, seq, _ = x.shape
    assert x.shape == (batch, seq, D_MODEL) and seq % (BLK * DIL_PATTERNS[-1][1]) == 0
    assert (batch * seq) % ROW_TILE == 0 and seq % ROW_TILE == 0
    assert mem.shape == (batch, MEM_LEN, D_MODEL) and w_in.shape == (DEPTH, D_MODEL, IN_WIDTH)
    xf = x.reshape(batch * seq, D_MODEL)
    memf = mem.reshape(batch * MEM_LEN, D_MODEL)
    cos, sin = _rope_tables(positions)
    perm = _group_permutation()
    sgu_bias = jnp.repeat(jnp.swapaxes(sgu_b, 1, 2), HEAD_DIM, axis=2)
    ln_g = sgu_ln_g.reshape(DEPTH, 1, B_WIDTH)
    ln_b = sgu_ln_b.reshape(DEPTH, 1, B_WIDTH)
    for l in range(DEPTH):
        za, yb, yc = _inproj_mixers(xf, norm_g, w_in, perm, cos, sin, sgu_w, sgu_bias, ln_g, ln_b,
                                    l, batch, seq)
        ya = _dilated_attention(za, batch, seq)
        kv = _memory_kv(memf, norm_g, x_wkv, l)
        xf = _post_mix(ya, yb, yc, xf, kv, norm_g, w_out, perm.T, x_wq, x_wo, l, batch, seq)
        xf = _mlp(xf, norm_g, w_up, w_down, l)
    return xf.reshape(batch, seq, D_MODEL)
```

```python
import functools

import jax
import jax.numpy as jnp
import numpy as np
from jax import lax
from jax.experimental import pallas as pl
from jax.experimental.pallas import tpu as pltpu

F32 = jnp.float32
BF16 = jnp.bfloat16

D_MODEL = 1024
DEPTH = 2
MEM_LEN = 256
HEAD_DIM = 64
DIL_HEADS = 6
DIL_PATTERNS = ((128, 1), (512, 4), (2048, 16))
BLK = 128
SGU_GROUPS = 4
RET_HEADS = 6
ROPE_BASE = 10000.0
A_WIDTH = DIL_HEADS * HEAD_DIM
B_WIDTH = SGU_GROUPS * HEAD_DIM
C_WIDTH = RET_HEADS * HEAD_DIM
QKV_A = 3 * A_WIDTH
UV_B = 2 * B_WIDTH
QKVG_C = 4 * C_WIDTH
IN_WIDTH = QKV_A + UV_B + QKVG_C
X_HEADS = 4
X_HEAD_DIM = D_MODEL // X_HEADS
D_FF = 4 * D_MODEL
EPS = 1e-6

LANES = 128
N_PAIRS = A_WIDTH // LANES
VMEM_LIMIT = 56 * 1024 * 1024

SUPER = BLK * DIL_PATTERNS[-1][1]

ROW_TILE = 512
DIL_UNROLL = (4, 8, 8)
DIL_LOOKAHEAD = (12, 2, 2)
MIX_UNROLL = 2
POST_TILE = 1024
POST_PART = 256
MLP_TILE = 512
FF_CHUNK = 1024


def _params(*sem):
    return pltpu.CompilerParams(dimension_semantics=sem, vmem_limit_bytes=VMEM_LIMIT)


def _layer(arr, l, **kw):
    zeros = (0,) * (arr.ndim - 1)
    return pl.BlockSpec((None,) + arr.shape[1:], lambda *_: (l,) + zeros, **kw)


def _rms(x, g):
    ms = jnp.mean(x * x, axis=-1, keepdims=True)
    return x * lax.rsqrt(ms + EPS) * g


def _dot(a, b):
    return jnp.dot(a, b, preferred_element_type=F32)


def _dot_nt(a, b):
    return lax.dot_general(a, b, (((1,), (1,)), ((), ())), preferred_element_type=F32)


def _dot_tn(a, b):
    return lax.dot_general(a, b, (((0,), (0,)), ((), ())), preferred_element_type=F32)


def _lane_lo(shape):
    lane = lax.broadcasted_iota(jnp.int32, shape, len(shape) - 1)
    return (lane & HEAD_DIM) == 0


ROPE_PACK = LANES // (HEAD_DIM // 2)


def _rope_kernel(pos_ref, inv_ref, cos_ref, sin_ref):
    half = HEAD_DIM // 2
    ang = pos_ref[...].astype(F32) * inv_ref[...]
    cos, sin = jnp.cos(ang), jnp.sin(ang)
    lane = lax.broadcasted_iota(jnp.int32, ang.shape, 1)
    sign = jnp.where(lane < HEAD_DIM, -1.0, 1.0)
    rows = ang.shape[0]
    for t in range(ROPE_PACK):
        mine = (lane >= t * half) & (lane < (t + 1) * half)
        for table, out_ref, scale in ((cos, cos_ref, None), (sin, sin_ref, sign)):
            y = jnp.where(mine, table, 0.0)
            y = y + pltpu.roll(y, 2 * half, 1)
            y = y + pltpu.roll(y, half, 1)
            out_ref[pl.ds(t, rows, stride=ROPE_PACK), :] = y if scale is None else y * scale


def _rope_tables(positions):
    n = positions.size
    half = HEAD_DIM // 2
    inv = 1.0 / (ROPE_BASE ** jnp.linspace(0.0, 1.0, half, dtype=F32))
    inv = jnp.tile(inv, ROPE_PACK).reshape(1, LANES)
    pos = jnp.repeat(positions.reshape(n // ROPE_PACK, ROPE_PACK), half, axis=1)
    tm = 2048
    return pl.pallas_call(
        _rope_kernel,
        grid=(n // tm,),
        in_specs=[pl.BlockSpec((tm // ROPE_PACK, LANES), lambda i: (i, 0)),
                  pl.BlockSpec((1, LANES), lambda i: (0, 0))],
        out_specs=[pl.BlockSpec((tm, LANES), lambda i: (i, 0))] * 2,
        out_shape=[jax.ShapeDtypeStruct((n, LANES), F32)] * 2,
        compiler_params=_params("parallel"),
        name="rope_tables",
    )(pos, inv)


MAX_DIL = DIL_PATTERNS[-1][1]
SUB = 8


def _group_permutation():
    i = np.arange(BLK)
    src = MAX_DIL * (i % SUB) + i // SUB
    return jnp.asarray(src[:, None] == np.arange(BLK)[None, :], BF16)


def _run_offsets(d):
    return [BLK * g + SUB * d * c for g in range(d) for c in range(MAX_DIL // d)]


def _block_load(ref, p, base, d):
    if d == 1:
        return ref[p, pl.ds(pl.multiple_of(base, BLK), BLK), :]
    return jnp.concatenate(
        [ref[p, pl.ds(pl.multiple_of(base + off, SUB), SUB), :] for off in _run_offsets(d)], axis=0)


def _block_store(ref, p, base, d, val):
    if d == 1:
        ref[p, pl.ds(pl.multiple_of(base, BLK), BLK), :] = val
        return
    for i, off in enumerate(_run_offsets(d)):
        ref[p, pl.ds(pl.multiple_of(base + off, SUB), SUB), :] = val[i * SUB:(i + 1) * SUB]


def _dil_blocks(refs, branch, d, blocks, first, lookahead):
    bias_ref, qp, kring, vring, acc_o, acc_m, acc_l = refs
    lo = _lane_lo((BLK, LANES))
    units = [(blk, p) for blk in blocks for p in range(N_PAIRS)]

    def scores(unit):
        (qbase, kcbase, kpbase, _), p = unit
        q = _block_load(qp, p, qbase, d).astype(BF16)
        k = jnp.concatenate([_block_load(kring, p, kpbase, d), _block_load(kring, p, kcbase, d)],
                            axis=0).astype(BF16)
        return [_dot_nt(jnp.where(lo if e == 0 else ~lo, q, jnp.zeros_like(q)), k) for e in range(2)]

    def softmax_pv(unit, s_pair):
        (qbase, kcbase, kpbase, has_prev), p = unit
        bias_prev = jnp.where(has_prev, bias_ref[branch, :, :BLK], -jnp.inf)
        bias_cur = bias_ref[branch, :, BLK:]
        v = jnp.concatenate([_block_load(vring, p, kpbase, d), _block_load(vring, p, kcbase, d)],
                            axis=0).astype(BF16)
        o_h, m_h, l_h = [], [], []
        for s in s_pair:
            sp = s[:, :BLK] + bias_prev
            sc = s[:, BLK:] + bias_cur
            m = jnp.max(jnp.maximum(sp, sc), axis=-1, keepdims=True)
            pp = jnp.exp(sp - m)
            pc = jnp.exp(sc - m)
            l_h.append(jnp.sum(pp + pc, axis=-1, keepdims=True))
            m_h.append(m)
            o_h.append(_dot(jnp.concatenate([pp, pc], axis=1).astype(BF16), v))
        return o_h, m_h, l_h

    def merge(unit, stats):
        (qbase, _, _, _), p = unit
        o_h, m_h, l_h = stats
        o_b = jnp.where(lo, o_h[0], o_h[1])
        m_b = jnp.where(lo, m_h[0], m_h[1])
        l_b = jnp.where(lo, l_h[0], l_h[1])
        if first:
            _block_store(acc_o, p, qbase, d, o_b)
            _block_store(acc_m, p, qbase, d, m_b)
            _block_store(acc_l, p, qbase, d, l_b)
        else:
            m_old = _block_load(acc_m, p, qbase, d)
            m_new = jnp.maximum(m_old, m_b)
            a_old = jnp.exp(m_old - m_new)
            a_b = jnp.exp(m_b - m_new)
            _block_store(acc_o, p, qbase, d, _block_load(acc_o, p, qbase, d) * a_old + o_b * a_b)
            _block_store(acc_l, p, qbase, d, _block_load(acc_l, p, qbase, d) * a_old + l_b * a_b)
            _block_store(acc_m, p, qbase, d, m_new)

    ahead = min(lookahead, len(units))
    pending = [scores(u) for u in units[:ahead]]
    stats_prev = None
    for i, unit in enumerate(units):
        if i + ahead < len(units):
            pending.append(scores(units[i + ahead]))
        stats = softmax_pv(unit, pending[i])
        if stats_prev is not None:
            merge(units[i - 1], stats_prev)
        stats_prev = stats
    merge(units[-1], stats_prev)


def _dil_kernel(q_ref, k_ref, v_ref, bias_ref, o_ref, qp, kring, vring, acc_o, acc_m, acc_l):
    t = pl.program_id(1)
    cur = pl.multiple_of((t % 2) * SUPER, SUPER)
    other = pl.multiple_of(SUPER - cur, SUPER)

    @pl.when(t == 0)
    def _():
        kring[:, SUPER:, :] = jnp.zeros((N_PAIRS, SUPER, LANES), F32)
        vring[:, SUPER:, :] = jnp.zeros((N_PAIRS, SUPER, LANES), F32)

    for p in range(N_PAIRS):
        cols = slice(p * LANES, (p + 1) * LANES)
        qp[p] = q_ref[:, cols].astype(F32) * (HEAD_DIM ** -0.5)
        kring[p, pl.ds(cur, SUPER), :] = k_ref[:, cols].astype(F32)
        vring[p, pl.ds(cur, SUPER), :] = v_ref[:, cols].astype(F32)
    refs = (bias_ref, qp, kring, vring, acc_o, acc_m, acc_l)

    for branch, (window, d) in enumerate(DIL_PATTERNS):
        nblk = MAX_DIL // d
        span = BLK * d

        def block_rows(j, d=d, nblk=nblk, span=span):
            if nblk == 1:
                qbase = pl.multiple_of(j * SUB, SUB)
                return qbase, cur + qbase, other + qbase, t > 0
            r = j // nblk
            nb = j % nblk
            qbase = pl.multiple_of(nb * span + r * SUB, SUB)
            kpbase = pl.multiple_of(
                jnp.where(nb == 0, other + (nblk - 1) * span + r * SUB, cur + qbase - span), SUB)
            return qbase, cur + qbase, kpbase, (t > 0) | (nb > 0)

        unroll, lookahead = DIL_UNROLL[branch], DIL_LOOKAHEAD[branch]

        def body(i, carry, branch=branch, d=d, block_rows=block_rows, unroll=unroll, lookahead=lookahead):
            blocks = [block_rows(i * unroll + u) for u in range(unroll)]
            _dil_blocks(refs, branch, d, blocks, first=(branch == 0), lookahead=lookahead)
            return carry

        lax.fori_loop(0, SUPER // BLK // unroll, body, 0)

    def finish(i, carry):
        rows = pl.ds(pl.multiple_of(i * BLK, BLK), BLK)
        for p in range(N_PAIRS):
            o_ref[rows, p * LANES:(p + 1) * LANES] = (acc_o[p, rows, :] / acc_l[p, rows, :]).astype(BF16)
        return carry

    lax.fori_loop(0, SUPER // BLK, finish, 0)


def _dilated_bias():
    out = []
    i = np.arange(BLK)
    for _, d in DIL_PATTERNS:
        n_c = MAX_DIL // d
        pos = (BLK // d) * (i // (SUB * n_c)) + n_c * (i % SUB) + (i // SUB) % n_c
        prev_ok = pos[None, :] >= pos[:, None]
        cur_ok = pos[None, :] <= pos[:, None]
        out.append(np.where(np.concatenate([prev_ok, cur_ok], axis=1), 0.0, -np.inf))
    return np.stack(out).astype(np.float32)


def _dilated_attention(qkv, batch, seq):
    n = batch * seq
    tiles = seq // SUPER
    bias = _dilated_bias()
    spec = lambda which: pl.BlockSpec((SUPER, A_WIDTH), lambda b, t: (b * tiles + t, which))
    tile_f32 = pltpu.VMEM((N_PAIRS, SUPER, LANES), F32)
    ring_f32 = pltpu.VMEM((N_PAIRS, 2 * SUPER, LANES), F32)
    return pl.pallas_call(
        _dil_kernel,
        grid=(batch, tiles),
        in_specs=[spec(0), spec(1), spec(2), pl.BlockSpec(bias.shape, lambda b, t: (0, 0, 0))],
        out_specs=pl.BlockSpec((SUPER, A_WIDTH), lambda b, t: (b * tiles + t, 0)),
        out_shape=jax.ShapeDtypeStruct((n, A_WIDTH), BF16),
        scratch_shapes=[tile_f32, ring_f32, ring_f32, tile_f32, tile_f32, tile_f32],
        compiler_params=_params("parallel", "arbitrary"),
        name="dilated_attention",
    )(qkv, qkv, qkv, bias)


def _rotate(t, cos, sin_signed):
    return t * cos + pltpu.roll(t, HEAD_DIM, 1) * sin_signed


PROJ_PIECES = (
    (QKV_A + UV_B, 512), (QKV_A + UV_B + 512, 512), (QKV_A + UV_B + 1024, 512),
    (QKV_A, UV_B),
    (0, 640), (640, QKV_A - 640))


def _inmix_body(x_ref, g_ref, w_ref, perm_ref, cos_ref, sin_ref, sb_ref, lg_ref, lb_ref,
                dec_ref, qdec_ref, kdec_ref, cdm_ref, a_ref, yb_ref, yc_ref,
                zb_w, zc_w, zb_r, zc_r, state_ref, wt_ref):
    lo = _lane_lo((BLK, LANES))
    row = lax.broadcasted_iota(jnp.int32, (BLK, BLK), 0)
    col = lax.broadcasted_iota(jnp.int32, (BLK, BLK), 1)
    same_head = ((row >> 5) & 1) == (col >> 6)
    lane = lax.broadcasted_iota(jnp.int32, (BLK, LANES), 1)
    qk_head0 = (lane & (HEAD_DIM // 2)) == 0
    inv_n = 1.0 / HEAD_DIM

    def sgu_pre(rows):
        u = jax.nn.gelu(zb_r[rows, :B_WIDTH])
        v = jax.nn.gelu(zb_r[rows, B_WIDTH:])
        mu = jnp.mean(v, axis=-1, keepdims=True)
        var = jnp.mean(jnp.square(v - mu), axis=-1, keepdims=True)
        v = (v - mu) * lax.rsqrt(var + EPS) * lg_ref[...] + lb_ref[...]
        return u, v.astype(BF16)

    def sgu_post(rows, pre):
        u, v = pre
        for gp in range(B_WIDTH // LANES):
            cols = slice(gp * LANES, (gp + 1) * LANES)
            vp = v[:, cols]
            v2 = jnp.concatenate([jnp.where(lo, vp, jnp.zeros_like(vp)),
                                  jnp.where(lo, jnp.zeros_like(vp), vp)], axis=0)
            gate = _dot(wt_ref[gp], v2) + sb_ref[:, cols]
            yb_ref[rows, cols] = (u[:, cols] * gate).astype(BF16)

    def ret_scores(row_list):
        work = [(rows, p) for rows in row_list for p in range(N_PAIRS)]
        out = []
        for rows, p in work:
            cols = slice(p * LANES, (p + 1) * LANES)
            cos, sin = cos_ref[rows, :], sin_ref[rows, :]
            q = _rotate(zc_r[rows, p * LANES:(p + 1) * LANES], cos, sin)
            k = _rotate(zc_r[rows, C_WIDTH + p * LANES:C_WIDTH + (p + 1) * LANES], cos, sin)
            vb = zc_r[rows, 2 * C_WIDTH + p * LANES:2 * C_WIDTH + (p + 1) * LANES].astype(BF16)
            qb, kb = q.astype(BF16), k.astype(BF16)
            k2 = jnp.concatenate([jnp.where(qk_head0, kb, jnp.zeros_like(kb)),
                                  jnp.where(qk_head0, jnp.zeros_like(kb), kb)], axis=0)
            scores = _dot_nt(qb, k2)
            kv = _dot_tn((k * kdec_ref[:, cols]).astype(BF16), vb)
            out.append((scores, kv, (q * qdec_ref[:, cols]).astype(BF16), vb))
        return work, out

    def ret_values(work, stage1):
        out = []
        for (rows, p), (scores, kv, qd, vb) in zip(work, stage1):
            state = state_ref[p]
            y_x = _dot(qd, state.astype(BF16))
            state_ref[p] = state * cdm_ref[p] + jnp.where(same_head, kv, 0.0)
            v2 = jnp.concatenate([jnp.where(lo, vb, jnp.zeros_like(vb)),
                                  jnp.where(lo, jnp.zeros_like(vb), vb)], axis=0)
            y_in = _dot((scores * dec_ref[p]).astype(BF16), v2)
            out.append((y_in, y_x))
        return out

    def ret_norm(work, stage2):
        for (rows, p), (y_in, y_x) in zip(work, stage2):
            cols = slice(p * LANES, (p + 1) * LANES)
            y = y_in + y_x
            mu_lo = jnp.sum(jnp.where(lo, y, 0.0), axis=-1, keepdims=True) * inv_n
            mu_hi = jnp.sum(jnp.where(lo, 0.0, y), axis=-1, keepdims=True) * inv_n
            yc_ = y - jnp.where(lo, mu_lo, mu_hi)
            sq = yc_ * yc_
            var_lo = jnp.sum(jnp.where(lo, sq, 0.0), axis=-1, keepdims=True) * inv_n
            var_hi = jnp.sum(jnp.where(lo, 0.0, sq), axis=-1, keepdims=True) * inv_n
            yn = yc_ * lax.rsqrt(jnp.where(lo, var_lo, var_hi) + EPS)
            gate = zc_r[rows, 3 * C_WIDTH + p * LANES:3 * C_WIDTH + (p + 1) * LANES]
            yc_ref[rows, cols] = (jax.nn.silu(gate) * yn).astype(BF16)

    def project(lhs, piece):
        first, width = PROJ_PIECES[piece]
        y = _dot(lhs, w_ref[:, first:first + width])
        if first >= QKV_A + UV_B:
            zc_w[:, first - QKV_A - UV_B:first - QKV_A - UV_B + width] = y
        elif first >= QKV_A:
            zb_w[...] = y
        else:
            a_ref[:, first:first + width] = y.astype(BF16)

    tm = x_ref.shape[0]
    chunks = [slice(c * BLK, (c + 1) * BLK) for c in range(tm // BLK)]
    groups = [chunks[g:g + MIX_UNROLL] for g in range(0, len(chunks), MIX_UNROLL)]
    assert len(groups) == 2 and len(PROJ_PIECES) == 6

    h = _rms(x_ref[...], g_ref[0:1, :]).astype(BF16)
    hp = jnp.concatenate([_dot(perm_ref[...], h[c]).astype(BF16) for c in chunks], axis=0)
    project(h, 0)
    pre = [sgu_pre(rows) for rows in groups[0]]
    work0, s1 = ret_scores(groups[0])
    project(h, 1)
    for rows, t in zip(groups[0], pre):
        sgu_post(rows, t)
    s2 = ret_values(work0, s1)
    project(h, 2)
    ret_norm(work0, s2)
    pre = [sgu_pre(rows) for rows in groups[1]]
    work1, s1 = ret_scores(groups[1])
    project(h, 3)
    for rows, t in zip(groups[1], pre):
        sgu_post(rows, t)
    s2 = ret_values(work1, s1)
    project(hp, 4)
    ret_norm(work1, s2)
    project(hp, 5)


def _inmix_kernel(x_ref, g_ref, w_ref, perm_ref, cos_ref, sin_ref, sw_ref, sb_ref, lg_ref, lb_ref,
                  dec_ref, qdec_ref, kdec_ref, cdm_ref, a_ref, yb_ref, yc_ref,
                  zb0, zc0, zb1, zc1, state_ref, wt_ref, wb_ref, *, tiles_per_batch):
    i = pl.program_id(0)

    @pl.when(i == 0)
    def _():
        half = HEAD_DIM // 2
        lane = lax.broadcasted_iota(jnp.int32, (BLK, LANES), 1)
        from_right = (lane >= half) & (lane < 2 * half)
        from_left = (lane >= 2 * half) & (lane < 3 * half)
        qk_tiles = range(QKV_A + UV_B, QKV_A + UV_B + 2 * C_WIDTH, LANES)

        def cast_rows(r, carry):
            rows = pl.ds(pl.multiple_of(r * BLK, BLK), BLK)
            for c0 in range(0, IN_WIDTH, LANES):
                t = w_ref[rows, c0:c0 + LANES]
                if c0 in qk_tiles:
                    t = jnp.where(from_right, pltpu.roll(t, LANES - half, 1),
                                  jnp.where(from_left, pltpu.roll(t, half, 1), t))
                wb_ref[rows, c0:c0 + LANES] = t.astype(BF16)
            return carry

        lax.fori_loop(0, D_MODEL // BLK, cast_rows, 0)
        zb1[...] = jnp.zeros_like(zb1)
        zc1[...] = jnp.zeros_like(zc1)
        row = lax.broadcasted_iota(jnp.int32, (BLK, BLK), 0)
        col = lax.broadcasted_iota(jnp.int32, (BLK, BLK), 1)
        for gp in range(SGU_GROUPS // 2):
            wt_ref[gp] = jnp.concatenate([jnp.where(row >= col, sw_ref[2 * gp + e], 0.0)
                                          for e in range(2)], axis=1).astype(BF16)

    @pl.when((i == 0) | ((i - 1) % tiles_per_batch == 0))
    def _():
        state_ref[...] = jnp.zeros_like(state_ref)

    common = (x_ref, g_ref, wb_ref, perm_ref, cos_ref, sin_ref, sb_ref, lg_ref, lb_ref,
              dec_ref, qdec_ref, kdec_ref, cdm_ref, a_ref, yb_ref, yc_ref)

    @pl.when(i % 2 == 0)
    def _():
        _inmix_body(*common, zb0, zc0, zb1, zc1, state_ref, wt_ref)

    @pl.when(i % 2 == 1)
    def _():
        _inmix_body(*common, zb1, zc1, zb0, zc0, state_ref, wt_ref)


def _qk_lane_head():
    lane = np.arange(C_WIDTH)
    return 2 * (lane // LANES) + ((lane >> 5) & 1)


def _retention_tables():
    f32 = np.float32
    heads = np.arange(RET_HEADS, dtype=f32)
    log_g = np.log1p(-np.power(f32(2.0), f32(-5.0) - heads)).astype(f32)
    idx = np.arange(BLK, dtype=f32)
    rel = idx[:, None] - idx[None, :]
    scale = f32(HEAD_DIM ** -0.5)
    decay = np.where(rel[None] >= 0,
                     np.exp(np.maximum(rel, f32(0.0))[None] * log_g[:, None, None]), f32(0.0)) * scale
    k_dec = np.exp((BLK - 1 - idx)[:, None] * log_g[None, :]) * scale
    q_dec = np.exp((idx + f32(1.0))[:, None] * log_g[None, :])
    lane_head = _qk_lane_head()
    k_dec = k_dec[:, lane_head]
    q_dec = q_dec[:, lane_head]
    chunk_decay = np.exp(f32(BLK) * log_g)
    per_row = chunk_decay[lane_head].reshape(N_PAIRS, LANES)
    row_head = (np.arange(LANES) >> 5) & 1
    col_head = np.arange(LANES) >> 6
    same = row_head[:, None] == col_head[None, :]
    cdm = np.where(same[None], per_row[:, :, None], f32(0.0))
    decay = np.concatenate([decay[0::2], decay[1::2]], axis=2)
    return tuple(t.astype(f32) for t in (decay, q_dec, k_dec, cdm))


def _inproj_mixers(x, g, w, perm, cos, sin, sgu_w, sgu_bias, ln_g, ln_b, l, batch, seq):
    n = x.shape[0]
    tm = ROW_TILE
    nt = n // tm
    decay, q_dec, k_dec, cdm = _retention_tables()
    cur = lambda i: (jnp.minimum(i, nt - 1), 0)
    prev = lambda i: (jnp.maximum(i - 1, 0), 0)
    const2 = lambda i: (0, 0)
    const3 = lambda i: (0, 0, 0)
    return pl.pallas_call(
        functools.partial(_inmix_kernel, tiles_per_batch=seq // tm),
        grid=(nt + 1,),
        in_specs=[pl.BlockSpec((tm, D_MODEL), cur),
                  _layer(g, l), _layer(w, l, pipeline_mode=pl.Buffered(1)),
                  pl.BlockSpec(perm.shape, const2),
                  pl.BlockSpec((tm, LANES), prev),
                  pl.BlockSpec((tm, LANES), prev),
                  _layer(sgu_w, l), _layer(sgu_bias, l), _layer(ln_g, l), _layer(ln_b, l),
                  pl.BlockSpec(decay.shape, const3),
                  pl.BlockSpec(q_dec.shape, const2),
                  pl.BlockSpec(k_dec.shape, const2),
                  pl.BlockSpec(cdm.shape, const3)],
        out_specs=[pl.BlockSpec((tm, QKV_A), cur),
                   pl.BlockSpec((tm, B_WIDTH), prev),
                   pl.BlockSpec((tm, C_WIDTH), prev)],
        out_shape=[jax.ShapeDtypeStruct((n, QKV_A), BF16),
                   jax.ShapeDtypeStruct((n, B_WIDTH), BF16),
                   jax.ShapeDtypeStruct((n, C_WIDTH), BF16)],
        scratch_shapes=[pltpu.VMEM((tm, UV_B), F32), pltpu.VMEM((tm, QKVG_C), F32),
                        pltpu.VMEM((tm, UV_B), F32), pltpu.VMEM((tm, QKVG_C), F32),
                        pltpu.VMEM((N_PAIRS, LANES, LANES), F32),
                        pltpu.VMEM((SGU_GROUPS // 2, BLK, 2 * BLK), BF16),
                        pltpu.VMEM((D_MODEL, IN_WIDTH), BF16)],
        compiler_params=_params("arbitrary"),
        name="inproj_mixers",
    )(x, g, w, perm, cos, sin, sgu_w, sgu_bias, ln_g, ln_b, decay, q_dec, k_dec, cdm)


def _kv_kernel(m_ref, g_ref, w_ref, kv_ref):
    h = _rms(m_ref[...], g_ref[6:7, :]).astype(BF16)
    kv_ref[...] = _dot(h, w_ref[...].astype(BF16)).astype(BF16)


def _memory_kv(mem, g, w, l):
    n = mem.shape[0]
    return pl.pallas_call(
        _kv_kernel,
        grid=(n // MEM_LEN,),
        in_specs=[pl.BlockSpec((MEM_LEN, D_MODEL), lambda i: (i, 0)),
                  _layer(g, l), _layer(w, l)],
        out_specs=pl.BlockSpec((MEM_LEN, 2 * D_MODEL), lambda i: (i, 0)),
        out_shape=jax.ShapeDtypeStruct((n, 2 * D_MODEL), BF16),
        compiler_params=_params("parallel"),
        name="memory_kv",
    )(mem, g, w)


def _post_mix_kernel(ya_ref, yb_ref, yc_ref, x_ref, kv_ref, g_ref, w_ref, unperm_ref, wq_ref, wo_ref,
                     out_ref, wb_ref):
    @pl.when(pl.program_id(0) == 0)
    def _():
        def cast_rows(r, carry):
            rows = pl.ds(pl.multiple_of(r * BLK, BLK), BLK)
            for k, src in enumerate((w_ref, wq_ref, wo_ref)):
                wb_ref[k, rows, :] = src[rows, :].astype(BF16)
            return carry

        lax.fori_loop(0, D_MODEL // BLK, cast_rows, 0)

    tm = x_ref.shape[0]
    n_parts = tm // POST_PART
    halves = [slice(i * POST_PART, (i + 1) * POST_PART) for i in range(n_parts)]
    heads = [slice(hd * X_HEAD_DIM, (hd + 1) * X_HEAD_DIM) for hd in range(X_HEADS)]
    scale = X_HEAD_DIM ** -0.5

    def unpermute(rows):
        n_grp = (rows.stop - rows.start) // BLK
        return jnp.concatenate(
            [_dot(unperm_ref[...], ya_ref[rows.start + g * BLK:rows.start + (g + 1) * BLK, :]).astype(BF16)
             for g in range(n_grp)], axis=0)

    ya = [unpermute(rows) for rows in halves]
    proj = [_dot(jnp.concatenate([ya[i], yb_ref[rows, :], yc_ref[rows, :]], axis=1), wb_ref[0])
            for i, rows in enumerate(halves)]
    x1 = [x_ref[rows, :] + _rms(proj[i], g_ref[1:2, :]) for i, rows in enumerate(halves)]
    q = [_dot(_rms(x1[i], g_ref[2:3, :]).astype(BF16), wb_ref[1]).astype(BF16) for i in range(n_parts)]
    scores = [[_dot_nt(q[i][:, cols], kv_ref[:, cols]) for cols in heads] for i in range(n_parts)]
    att = []
    for i in range(n_parts):
        outs = []
        for hd, s in enumerate(scores[i]):
            s = s * scale
            m = jnp.max(s, axis=-1, keepdims=True)
            p = jnp.exp(s - m)
            p = p / jnp.sum(p, axis=-1, keepdims=True)
            v = kv_ref[:, D_MODEL + hd * X_HEAD_DIM:D_MODEL + (hd + 1) * X_HEAD_DIM]
            outs.append(_dot(p.astype(BF16), v).astype(BF16))
        att.append(jnp.concatenate(outs, axis=1))
    acc = [_dot(att[i], wb_ref[2]) for i in range(n_parts)]
    for i, rows in enumerate(halves):
        out_ref[rows, :] = x1[i] + _rms(acc[i], g_ref[3:4, :])


def _post_mix(ya, yb, yc, x, kv, g, w_out, unperm, wq, wo, l, batch, seq):
    n = x.shape[0]
    tm = POST_TILE
    per_batch = seq // tm
    row = lambda i: (i, 0)
    const = lambda i: (0, 0)
    return pl.pallas_call(
        _post_mix_kernel,
        grid=(n // tm,),
        in_specs=[pl.BlockSpec((tm, A_WIDTH), row),
                  pl.BlockSpec((tm, B_WIDTH), row),
                  pl.BlockSpec((tm, C_WIDTH), row),
                  pl.BlockSpec((tm, D_MODEL), row),
                  pl.BlockSpec((MEM_LEN, 2 * D_MODEL), lambda i: (i // per_batch, 0)),
                  _layer(g, l), _layer(w_out, l, pipeline_mode=pl.Buffered(1)),
                  pl.BlockSpec(unperm.shape, const),
                  _layer(wq, l, pipeline_mode=pl.Buffered(1)), _layer(wo, l, pipeline_mode=pl.Buffered(1))],
        out_specs=pl.BlockSpec((tm, D_MODEL), row),
        out_shape=jax.ShapeDtypeStruct((n, D_MODEL), F32),
        scratch_shapes=[pltpu.VMEM((3, D_MODEL, D_MODEL), BF16)],
        compiler_params=_params("arbitrary"),
        name="outproj_cross",
    )(ya, yb, yc, x, kv, g, w_out, unperm, wq, wo)


def _mlp_kernel(x_ref, g_ref, wu_ref, wd_ref, out_ref):
    x = x_ref[...]
    h = _rms(x, g_ref[4:5, :]).astype(BF16)
    acc = jnp.zeros(x.shape, F32)
    for c in range(D_FF // FF_CHUNK):
        cols = slice(c * FF_CHUNK, (c + 1) * FF_CHUNK)
        f = jnp.square(jnp.maximum(_dot(h, wu_ref[:, cols].astype(BF16)), 0.0))
        acc += _dot(f.astype(BF16), wd_ref[cols, :].astype(BF16))
    out_ref[...] = x + _rms(acc, g_ref[5:6, :])


def _mlp(x, g, w_up, w_down, l):
    n = x.shape[0]
    tm = MLP_TILE
    row = lambda i: (i, 0)
    const = lambda i: (0, 0)
    return pl.pallas_call(
        _mlp_kernel,
        grid=(n // tm,),
        in_specs=[pl.BlockSpec((tm, D_MODEL), row),
                  _layer(g, l),
                  _layer(w_up, l, pipeline_mode=pl.Buffered(1)),
                  _layer(w_down, l, pipeline_mode=pl.Buffered(1))],
        out_specs=pl.BlockSpec((tm, D_MODEL), row),
        out_shape=jax.ShapeDtypeStruct((n, D_MODEL), F32),
        compiler_params=_params("parallel"),
        name="mlp",
    )(x, g, w_up, w_down)


def kernel(x, mem, positions, norm_g, w_in, sgu_w, sgu_b, sgu_ln_g, sgu_ln_b,
           w_out, x_wq, x_wkv, x_wo, w_up, w_down):
    batch, seq, _ = x.shape
    assert x.shape == (batch, seq, D_MODEL) and seq % (BLK * DIL_PATTERNS[-1][1]) == 0
    assert (batch * seq) % ROW_TILE == 0 and seq % ROW_TILE == 0
    assert mem.shape == (batch, MEM_LEN, D_MODEL) and w_in.shape == (DEPTH, D_MODEL, IN_WIDTH)
    xf = x.reshape(batch * seq, D_MODEL)
    memf = mem.reshape(batch * MEM_LEN, D_MODEL)
    cos, sin = _rope_tables(positions)
    perm = _group_permutation()
    sgu_bias = jnp.repeat(jnp.swapaxes(sgu_b, 1, 2), HEAD_DIM, axis=2)
    ln_g = sgu_ln_g.reshape(DEPTH, 1, B_WIDTH)
    ln_b = sgu_ln_b.reshape(DEPTH, 1, B_WIDTH)
    for l in range(DEPTH):
        za, yb, yc = _inproj_mixers(xf, norm_g, w_in, perm, cos, sin, sgu_w, sgu_bias, ln_g, ln_b,
                                    l, batch, seq)
        ya = _dilated_attention(za, batch, seq)
        kv = _memory_kv(memf, norm_g, x_wkv, l)
        xf = _post_mix(ya, yb, yc, xf, kv, norm_g, w_out, perm.T, x_wq, x_wo, l, batch, seq)
        xf = _mlp(xf, norm_g, w_up, w_down, l)
    return xf.reshape(batch, seq, D_MODEL)
```
